```python
import jax, jax.numpy as jnp
from jax import lax
import numpy as np

D_MODEL = 1024
BATCH = 4
SEQ = 4096
DEPTH = 1
DEC_BATCH = 128
DEC_SEQ = 1
PAST_LEN = 2048
PAGE_SIZE = 128

BRANCH_W = 512
CONV_W = BRANCH_W
CONV_K = 3
SB_HEADS = 8
SB_HEAD_DIM = 64
SB_W = SB_HEADS * SB_HEAD_DIM
SB_BIAS_INIT = -6.0
X_HEADS = 4
X_HEAD_DIM = 128
X_W = X_HEADS * X_HEAD_DIM
N_MEM = 256
N_BRANCH = 3
Q_BLOCK = 128
N_GROUPS = 4
EXPERTS_PER_GROUP = 8
N_EXPERTS = N_GROUPS * EXPERTS_PER_GROUP
TOP_K = 2
D_EXPERT = 512
MOE_BLOCK = 128
EPS = 1e-6
SPLIT_SIZES = (CONV_W, CONV_W, CONV_W, SB_W, SB_W, SB_W, X_W, N_BRANCH * D_MODEL)
SPLIT_POINTS = tuple(int(i) for i in np.cumsum(SPLIT_SIZES)[:-1])
D_IN = sum(SPLIT_SIZES)

kernel_name = 'hybrid_conv_stickbreak_memxattn_hmoe_step'


def rms_norm(x, g):
    xf = x.astype(jnp.float32)
    y = xf * lax.rsqrt(jnp.mean(xf * xf, axis=-1, keepdims=True) + EPS)
    return y.astype(x.dtype) * g


def mixer_inputs(x, norm1, w_in, q_norm):
    B, T, _ = x.shape
    u_in, c_gate, b_gate, q, k, v, qx, gate_logits = jnp.split(
        rms_norm(x, norm1) @ w_in, SPLIT_POINTS, axis=-1)
    sb = lambda t: t.reshape(B, T, SB_HEADS, SB_HEAD_DIM)
    qx = rms_norm(qx.reshape(B, T, X_HEADS, X_HEAD_DIM), q_norm)
    return c_gate * u_in, b_gate, sb(q), sb(k), sb(v), qx, gate_logits


def causal_conv(u_ext, w):
    T = u_ext.shape[1] - (CONV_K - 1)
    out = u_ext[:, 0:T] * w[0]
    for j in range(1, CONV_K):
        out = out + u_ext[:, j:j + T] * w[j]
    return out


def stick_breaking(q, k, v, sb_bias, q_pos, k_pos):
    z = jnp.einsum('bqhd,bkhd->bhqk', q, k, preferred_element_type=jnp.float32) * (SB_HEAD_DIM ** -0.5)
    z = z + sb_bias.astype(jnp.float32)[None, :, None, None]
    causal = k_pos[None, :] < q_pos[:, None]
    log_keep = jnp.where(causal, jax.nn.log_sigmoid(-z), 0.0)
    log_a = jax.nn.log_sigmoid(z) + lax.cumsum(log_keep, axis=3, reverse=True) - log_keep
    a = jnp.where(causal, jnp.exp(log_a), 0.0)
    return jnp.einsum('bhqk,bkhd->bqhd', a, v.astype(jnp.float32)).astype(v.dtype)


def stick_breaking_prompt(q, k, v, sb_bias):
    B, S, H, d = q.shape
    nb = S // Q_BLOCK
    qb = q.reshape(B, nb, Q_BLOCK, H, d).swapaxes(0, 1)
    pos = jnp.arange(S)
    out = lax.map(lambda a: stick_breaking(a[0], k, v, sb_bias, a[1], pos),
                  (qb, pos.reshape(nb, Q_BLOCK)))
    return out.swapaxes(0, 1).reshape(B, S, H, d)


def memory_kv(mem, mem_norm, w_mem_kv, k_norm):
    B, M, _ = mem.shape
    mk, mv = jnp.split(rms_norm(mem, mem_norm) @ w_mem_kv, 2, axis=-1)
    mk = rms_norm(mk.reshape(B, M, X_HEADS, X_HEAD_DIM), k_norm)
    return mk, mv.reshape(B, M, X_HEADS, X_HEAD_DIM)


def cross_attend(qx, mk, mv):
    s = jnp.einsum('bthd,bmhd->bhtm', qx, mk, preferred_element_type=jnp.float32) * (X_HEAD_DIM ** -0.5)
    p = jax.nn.softmax(s, axis=-1).astype(mv.dtype)
    return jnp.einsum('bhtm,bmhd->bthd', p, mv)


def grouped_experts(hf, expert, gate, w1, w3, w2):
    T, K = expert.shape
    n = T * K
    e_flat = expert.reshape(-1).astype(jnp.int32)
    tok = jnp.repeat(jnp.arange(T), K)
    order = jnp.argsort(e_flat * n + jnp.arange(n))
    e_sorted = e_flat[order]
    tok_sorted = tok[order]
    counts = jnp.bincount(e_flat, length=N_EXPERTS)
    starts = jnp.cumsum(counts) - counts
    padded = (counts + MOE_BLOCK - 1) // MOE_BLOCK * MOE_BLOCK
    ends = jnp.cumsum(padded)
    dest = (ends - padded)[e_sorted] + jnp.arange(n) - starts[e_sorted]
    n_blocks = -(-(n + N_EXPERTS * (MOE_BLOCK - 1)) // MOE_BLOCK)
    buf = jnp.zeros((n_blocks * MOE_BLOCK, hf.shape[1]), hf.dtype).at[dest].set(hf[tok_sorted])
    block_e = jnp.clip(jnp.searchsorted(ends, jnp.arange(n_blocks) * MOE_BLOCK, side='right'),
                       0, N_EXPERTS - 1)

    def run_block(a):
        xb, e = a
        return (jax.nn.silu(xb @ w1[e]) * (xb @ w3[e])) @ w2[e]

    ybuf = lax.map(run_block, (buf.reshape(n_blocks, MOE_BLOCK, -1), block_e)).reshape(n_blocks * MOE_BLOCK, -1)
    y_sorted = ybuf[dest] * gate.reshape(-1)[order][:, None].astype(hf.dtype)
    return jnp.zeros((T, ybuf.shape[1]), hf.dtype).at[tok_sorted].add(y_sorted)


def hier_moe(h, w_group, b_group, w_router, b_router, w1, w3, w2):
    shp = h.shape
    hf = h.reshape(-1, shp[-1])
    T = hf.shape[0]
    g_logits = (hf @ w_group + b_group).astype(jnp.float32)
    g_prob = jax.nn.softmax(g_logits, axis=-1)
    g_idx = jnp.argmax(g_logits, axis=-1)
    e_logits = (hf @ w_router + b_router).astype(jnp.float32).reshape(T, N_GROUPS, EXPERTS_PER_GROUP)
    e_in = jnp.take_along_axis(e_logits, g_idx[:, None, None], axis=1)[:, 0]
    top_v, top_i = lax.top_k(e_in, TOP_K)
    gate = jax.nn.softmax(top_v, axis=-1) * jnp.take_along_axis(g_prob, g_idx[:, None], axis=1)
    expert = g_idx[:, None] * EXPERTS_PER_GROUP + top_i
    return grouped_experts(hf, expert, gate, w1, w3, w2).reshape(shp)


def merge_and_ffn(x, y_conv, y_sb, y_x, gate_logits, w_branch, w_o, norm2,
                  w_group, b_group, w_router, b_router, w1, w3, w2):
    B, T, _ = x.shape
    ys = jnp.stack([y_conv, y_sb.reshape(B, T, SB_W), y_x.reshape(B, T, X_W)], axis=2)
    branch = jnp.einsum('btgc,gcd->btgd', ys, w_branch)
    gates = jax.nn.sigmoid(gate_logits.reshape(B, T, N_BRANCH, D_MODEL))
    x1 = x + jnp.sum(gates * branch, axis=2) @ w_o
    return x1 + hier_moe(rms_norm(x1, norm2), w_group, b_group, w_router, b_router, w1, w3, w2)


def setup_inputs(seed: int = 0) -> dict:
    key = jax.random.key(seed)
    ks = jax.random.split(key, 32)
    n_pages = PAST_LEN // PAGE_SIZE
    n_phys = (DEC_BATCH * n_pages * 5) // 4
    nrm = lambda k, shape, s: jax.random.normal(k, shape, jnp.float32) * s
    page_table = jax.random.permutation(ks[7], n_phys)[:DEC_BATCH * n_pages].reshape(
        DEC_BATCH, n_pages).astype(jnp.int32)
    return {
        'x_prompt': nrm(ks[0], (BATCH, SEQ, D_MODEL), 1.0),
        'x_sample': nrm(ks[1], (DEC_BATCH, DEC_SEQ, D_MODEL), 1.0),
        'mem_prompt': nrm(ks[2], (BATCH, N_MEM, D_MODEL), 1.0),
        'cache_k': nrm(ks[3], (n_phys, PAGE_SIZE, SB_HEADS, SB_HEAD_DIM), 1.0),
        'cache_v': nrm(ks[4], (n_phys, PAGE_SIZE, SB_HEADS, SB_HEAD_DIM), 1.0),
        'cache_mem_k': nrm(ks[5], (DEC_BATCH, N_MEM, X_HEADS, X_HEAD_DIM), 1.0),
        'cache_mem_v': nrm(ks[6], (DEC_BATCH, N_MEM, X_HEADS, X_HEAD_DIM), 1.0),
        'state_conv': nrm(ks[8], (DEC_BATCH, CONV_K - 1, CONV_W), 1.0),
        'page_table': page_table,
        'norm1': 1.0 + nrm(ks[9], (D_MODEL,), 0.02),
        'w_in': nrm(ks[10], (D_MODEL, D_IN), D_MODEL ** -0.5),
        'conv_w': nrm(ks[11], (CONV_K, CONV_W), CONV_K ** -0.5),
        'sb_bias': SB_BIAS_INIT + nrm(ks[26], (SB_HEADS,), 0.1),
        'mem_norm': 1.0 + nrm(ks[12], (D_MODEL,), 0.02),
        'w_mem_kv': nrm(ks[13], (D_MODEL, 2 * X_W), D_MODEL ** -0.5),
        'q_norm': 1.0 + nrm(ks[14], (X_HEAD_DIM,), 0.02),
        'k_norm': 1.0 + nrm(ks[15], (X_HEAD_DIM,), 0.02),
        'w_branch': nrm(ks[16], (N_BRANCH, BRANCH_W, D_MODEL), BRANCH_W ** -0.5),
        'w_o': nrm(ks[17], (D_MODEL, D_MODEL), D_MODEL ** -0.5),
        'norm2': 1.0 + nrm(ks[18], (D_MODEL,), 0.02),
        'w_group': nrm(ks[19], (D_MODEL, N_GROUPS), D_MODEL ** -0.5),
        'b_group': nrm(ks[20], (N_GROUPS,), 0.01),
        'w_router': nrm(ks[21], (D_MODEL, N_EXPERTS), D_MODEL ** -0.5),
        'b_router': nrm(ks[22], (N_EXPERTS,), 0.01),
        'w1': nrm(ks[23], (N_EXPERTS, D_MODEL, D_EXPERT), D_MODEL ** -0.5),
        'w3': nrm(ks[24], (N_EXPERTS, D_MODEL, D_EXPERT), D_MODEL ** -0.5),
        'w2': nrm(ks[25], (N_EXPERTS, D_EXPERT, D_MODEL), D_EXPERT ** -0.5),
    }


def reference(x_prompt, x_sample, mem_prompt, cache_k, cache_v, cache_mem_k, cache_mem_v,
              state_conv, page_table, norm1, w_in, conv_w, sb_bias, mem_norm, w_mem_kv, q_norm, k_norm,
              w_branch, w_o, norm2, w_group, b_group, w_router, b_router, w1, w3, w2):
    moe = (w_group, b_group, w_router, b_router, w1, w3, w2)
    u_p, bg_p, q_p, k_p, v_p, qx_p, gl_p = mixer_inputs(x_prompt, norm1, w_in, q_norm)
    y_conv_p = bg_p * causal_conv(jnp.pad(u_p, ((0, 0), (CONV_K - 1, 0), (0, 0))), conv_w)
    conv_prompt = u_p[:, -(CONV_K - 1):]
    y_sb_p = stick_breaking_prompt(q_p, k_p, v_p, sb_bias)
    mem_k_p, mem_v_p = memory_kv(mem_prompt, mem_norm, w_mem_kv, k_norm)
    y_x_p = cross_attend(qx_p, mem_k_p, mem_v_p)
    y_prompt = merge_and_ffn(x_prompt, y_conv_p, y_sb_p, y_x_p, gl_p, w_branch, w_o, norm2, *moe)
    n_batch, dec_seq = x_sample.shape[0], x_sample.shape[1]
    past = page_table.shape[1] * PAGE_SIZE
    u_s, bg_s, q_s, k_s, v_s, qx_s, gl_s = mixer_inputs(x_sample, norm1, w_in, q_norm)
    u_ext = jnp.concatenate([state_conv.astype(u_s.dtype), u_s], axis=1)
    y_conv_s = bg_s * causal_conv(u_ext, conv_w)
    conv_sample = u_ext[:, -(CONV_K - 1):]
    k_all = jnp.concatenate([cache_k[page_table].reshape(n_batch, past, SB_HEADS, SB_HEAD_DIM), k_s], axis=1)
    v_all = jnp.concatenate([cache_v[page_table].reshape(n_batch, past, SB_HEADS, SB_HEAD_DIM), v_s], axis=1)
    y_sb_s = stick_breaking(q_s, k_all, v_all, sb_bias, past + jnp.arange(dec_seq), jnp.arange(past + dec_seq))
    y_x_s = cross_attend(qx_s, cache_mem_k, cache_mem_v)
    y_sample = merge_and_ffn(x_sample, y_conv_s, y_sb_s, y_x_s, gl_s, w_branch, w_o, norm2, *moe)
    return (y_prompt, y_sample, k_p, v_p, conv_prompt, mem_k_p, mem_v_p, k_s, v_s, conv_sample)
```

```python
import functools

import jax
import jax.numpy as jnp
from jax import lax
from jax.experimental import pallas as pl
from jax.experimental.pallas import tpu as pltpu

F32 = jnp.float32
BF16 = jnp.bfloat16
I32 = jnp.int32

EPS = 1e-6
D_MODEL = 1024
BRANCH_W = 512
SB_HEADS = 8
SB_HEAD_DIM = 64
X_HEADS = 4
X_HEAD_DIM = 128
N_MEM = 256
N_GROUPS = 4
EXPERTS_PER_GROUP = 8
N_EXPERTS = N_GROUPS * EXPERTS_PER_GROUP
D_EXPERT = 512
PAGE_SIZE = 128

LANES = 128
SUBLANES = 8
VMEM_LIMIT = 56 * 1024 * 1024

SB_TQ = 256
SB_TK = 256
MOE_BM = 256


def _dot(a, b):
    return jnp.dot(a, b, preferred_element_type=F32)


def _dot_nt(a, b):
    return lax.dot_general(a, b, (((1,), (1,)), ((), ())), preferred_element_type=F32)


def _rms(x, g):
    return x * lax.rsqrt(jnp.mean(x * x, axis=-1, keepdims=True) + EPS) * g


def _softplus(z):
    return jnp.maximum(z, 0.0) + jnp.log1p(jnp.exp(-jnp.abs(z)))


def _params(sem, vmem=VMEM_LIMIT):
    return pltpu.CompilerParams(dimension_semantics=sem, vmem_limit_bytes=vmem)


def _const_spec(shape):
    nd = len(shape)
    return pl.BlockSpec(shape, lambda *_: (0,) * nd, pipeline_mode=pl.Buffered(1))


def _memkv_kernel(mem_ref, g_ref, w_ref, kn_ref, mk_ref, mv_ref):
    xn = _rms(mem_ref[...], g_ref[...]).astype(BF16)
    kv = _dot(xn, w_ref[...])
    xw = X_HEADS * X_HEAD_DIM
    for h in range(X_HEADS):
        sl = slice(h * X_HEAD_DIM, (h + 1) * X_HEAD_DIM)
        mk_ref[:, sl] = _rms(kv[:, sl], kn_ref[...])
    mv_ref[...] = kv[:, xw:]


def memkv(mem2d, mem_norm, w_mem_kv_bf, k_norm, n_mem):
    n = mem2d.shape[0]
    xw = X_HEADS * X_HEAD_DIM
    return pl.pallas_call(
        _memkv_kernel,
        grid=(n // n_mem,),
        in_specs=[pl.BlockSpec((n_mem, D_MODEL), lambda i: (i, 0)),
                  _const_spec((1, D_MODEL)),
                  _const_spec((D_MODEL, 2 * xw)),
                  _const_spec((1, X_HEAD_DIM))],
        out_specs=[pl.BlockSpec((n_mem, xw), lambda i: (i, 0)),
                   pl.BlockSpec((n_mem, xw), lambda i: (i, 0))],
        out_shape=[jax.ShapeDtypeStruct((n, xw), F32), jax.ShapeDtypeStruct((n, xw), F32)],
        compiler_params=_params(("arbitrary",)),
        name="memkv",
    )(mem2d, mem_norm.reshape(1, -1), w_mem_kv_bf, k_norm.reshape(1, -1))


_C_UIN, _C_CG, _C_BG, _C_Q, _C_QX, _C_GL = 0, 512, 1024, 1536, 2048, 2560
_W_MAIN = _C_GL + 3 * D_MODEL


def _inproj_kernel(carry_conv, tiles_per_seq, tm, *refs):
    if carry_conv:
        (x_ref, n1_ref, wm_ref, wkvt_ref, wb0_ref, cw_ref, qn_ref,
         kt_ref, vt_ref, q_ref, qx_ref, g1_ref, g2_ref, mp_ref, tail_ref, ktb_ref, vtb_ref,
         carry_ref) = refs
    else:
        (x_ref, n1_ref, wm_ref, wkvt_ref, wb0_ref, cw_ref, qn_ref, um2_ref, um1_ref,
         kt_ref, vt_ref, q_ref, qx_ref, g1_ref, g2_ref, mp_ref, tail_ref) = refs

    xn = _rms(x_ref[...], n1_ref[...]).astype(BF16)

    def proj(c0, width):
        return _dot(xn, wm_ref[:, c0:c0 + width])

    kvt = _dot_nt(wkvt_ref[...], xn)
    sbw = SB_HEADS * SB_HEAD_DIM
    kt_ref[...] = kvt[:sbw]
    vt_ref[...] = kvt[sbw:]
    if carry_conv:
        for c in range(tm // SB_TK):
            ktb_ref[c] = kvt[:sbw, c * SB_TK:(c + 1) * SB_TK].astype(BF16)
            vtb_ref[c] = kvt[sbw:, c * SB_TK:(c + 1) * SB_TK].astype(BF16)

    q_ref[...] = (proj(_C_Q, sbw) * (SB_HEAD_DIM ** -0.5)).astype(BF16)

    qx = proj(_C_QX, X_HEADS * X_HEAD_DIM)
    for h in range(X_HEADS):
        sl = slice(h * X_HEAD_DIM, (h + 1) * X_HEAD_DIM)
        qx_ref[:, sl] = _rms(qx[:, sl], qn_ref[...]).astype(BF16)

    u = proj(_C_CG, BRANCH_W) * proj(_C_UIN, BRANCH_W)
    if carry_conv:
        @pl.when(pl.program_id(0) % tiles_per_seq == 0)
        def _():
            carry_ref[...] = jnp.zeros_like(carry_ref)
        ext = jnp.concatenate([carry_ref[...], u], axis=0)
        um1 = pltpu.roll(ext, 1, 0)[SUBLANES:]
        um2 = pltpu.roll(ext, 2, 0)[SUBLANES:]
        carry_ref[...] = u[tm - SUBLANES:]
        tail_ref[...] = u[tm - SUBLANES:]
    else:
        um1 = um1_ref[...]
        um2 = um2_ref[...]
        tail_ref[...] = u
    cw = cw_ref[...]
    y_conv = proj(_C_BG, BRANCH_W) * (um2 * cw[0:1] + um1 * cw[1:2] + u * cw[2:3])

    g0 = jax.nn.sigmoid(proj(_C_GL, D_MODEL))
    mp_ref[...] = g0 * _dot(y_conv.astype(BF16), wb0_ref[...])
    g1_ref[...] = jax.nn.sigmoid(proj(_C_GL + D_MODEL, D_MODEL))
    g2_ref[...] = jax.nn.sigmoid(proj(_C_GL + 2 * D_MODEL, D_MODEL))


def inproj(x2d, norm1, w_main_bf, w_kvt_bf, wb0_bf, conv_w, q_norm, seq_len, tm, conv_state=None):
    n = x2d.shape[0]
    carry = conv_state is None
    nseq = n // seq_len if carry else 1
    slen = seq_len if carry else n
    tps = slen // tm
    sbw = SB_HEADS * SB_HEAD_DIM
    xw = X_HEADS * X_HEAD_DIM
    nkb = max(tm // SB_TK, 1)
    tail_rows = SUBLANES if carry else tm

    row = lambda w: pl.BlockSpec((tm, w), lambda i: (i, 0))
    kt_spec = pl.BlockSpec((None, sbw, tm), lambda i: (i // tps, 0, i % tps))
    ktb_spec = pl.BlockSpec((None, nkb, sbw, SB_TK), lambda i: (i // tps, i % tps, 0, 0))
    in_specs = [row(D_MODEL), _const_spec((1, D_MODEL)), _const_spec((D_MODEL, _W_MAIN)),
                _const_spec((2 * sbw, D_MODEL)), _const_spec((BRANCH_W, D_MODEL)),
                _const_spec((3, BRANCH_W)), _const_spec((1, X_HEAD_DIM))]
    args = [x2d, norm1.reshape(1, -1), w_main_bf, w_kvt_bf, wb0_bf, conv_w, q_norm.reshape(1, -1)]
    scratch = []
    if carry:
        scratch = [pltpu.VMEM((SUBLANES, BRANCH_W), F32)]
    else:
        in_specs += [row(BRANCH_W), row(BRANCH_W)]
        args += list(conv_state)
    out_specs = [kt_spec, kt_spec, row(sbw), row(xw),
                 row(D_MODEL), row(D_MODEL), row(D_MODEL),
                 pl.BlockSpec((None, tail_rows, BRANCH_W), lambda i: (i // tps, 0, 0))]
    out_shape = [jax.ShapeDtypeStruct((nseq, sbw, slen), F32), jax.ShapeDtypeStruct((nseq, sbw, slen), F32),
                 jax.ShapeDtypeStruct((n, sbw), BF16), jax.ShapeDtypeStruct((n, xw), BF16),
                 jax.ShapeDtypeStruct((n, D_MODEL), F32), jax.ShapeDtypeStruct((n, D_MODEL), F32),
                 jax.ShapeDtypeStruct((n, D_MODEL), F32),
                 jax.ShapeDtypeStruct((nseq, tail_rows, BRANCH_W), F32)]
    if carry:
        out_specs += [ktb_spec, ktb_spec]
        out_shape += [jax.ShapeDtypeStruct((nseq, slen // SB_TK, sbw, SB_TK), BF16)] * 2
    return pl.pallas_call(
        functools.partial(_inproj_kernel, carry, tps, tm),
        grid=(n // tm,),
        in_specs=in_specs, out_specs=out_specs, out_shape=out_shape,
        scratch_shapes=scratch,
        compiler_params=_params(("arbitrary",)),
        name="inproj_prompt" if carry else "inproj_sample",
    )(*args)


def _sbp_kernel(bias_ref, q_ref, kt_ref, vt_ref, o_ref, acc_ref, car_ref):
    pair = pl.program_id(1)
    qi = pl.program_id(2)
    tq, tk = SB_TQ, SB_TK
    lane = lax.broadcasted_iota(I32, (tq, LANES), 1)
    q = q_ref[...]
    zero = jnp.zeros_like(q)
    qh = (jnp.where(lane < SB_HEAD_DIM, q, zero), jnp.where(lane >= SB_HEAD_DIM, q, zero))
    r = lax.broadcasted_iota(I32, (tk, tk), 0)
    c = lax.broadcasted_iota(I32, (tk, tk), 1)
    tri = (r >= c).astype(BF16)
    causal = c < r
    acc_ref[...] = jnp.zeros_like(acc_ref)
    car_ref[...] = jnp.zeros_like(car_ref)

    def block(j, masked):
        kt = kt_ref[j]
        vt = vt_ref[j]
        for h in range(2):
            z = _dot(qh[h], kt) + bias_ref[2 * pair + h]
            sp = _softplus(z)
            if masked:
                sp = jnp.where(causal, sp, 0.0)
            car = car_ref[h]
            cum = _dot(sp.astype(BF16), tri)
            a = jnp.exp(z - (cum + jnp.concatenate([car] * (tk // LANES), axis=1)))
            if masked:
                a = jnp.where(causal, a, 0.0)
            acc_ref[h] += _dot_nt(a.astype(BF16), vt)
            car_ref[h] = car + jnp.broadcast_to(cum[:, 0:1], (tq, LANES))

    block(qi, True)

    def body(t, carry):
        block(qi - 1 - t, False)
        return carry

    lax.fori_loop(0, qi, body, 0)
    o_ref[...] = jnp.where(lane < SB_HEAD_DIM, acc_ref[0], acc_ref[1]).astype(o_ref.dtype)


def sb_prompt(q_bf, ktb, vtb, sb_bias, seq_len):
    n = q_bf.shape[0]
    nseq = n // seq_len
    nq = seq_len // SB_TQ
    nk = seq_len // SB_TK
    npair = SB_HEADS // 2
    grid_spec = pltpu.PrefetchScalarGridSpec(
        num_scalar_prefetch=0,
        grid=(nseq, npair, nq),
        in_specs=[pl.BlockSpec(memory_space=pltpu.SMEM),
                  pl.BlockSpec((SB_TQ, LANES), lambda b, p, i: (b * nq + i, p)),
                  pl.BlockSpec((None, nk, LANES, SB_TK), lambda b, p, i: (b, 0, p, 0)),
                  pl.BlockSpec((None, nk, LANES, SB_TK), lambda b, p, i: (b, 0, p, 0))],
        out_specs=pl.BlockSpec((SB_TQ, LANES), lambda b, p, i: (b * nq + i, p)),
        scratch_shapes=[pltpu.VMEM((2, SB_TQ, LANES), F32), pltpu.VMEM((2, SB_TQ, LANES), F32)],
    )
    return pl.pallas_call(
        _sbp_kernel, grid_spec=grid_spec,
        out_shape=jax.ShapeDtypeStruct((n, SB_HEADS * SB_HEAD_DIM), BF16),
        compiler_params=_params(("arbitrary", "arbitrary", "arbitrary")),
        name="sb_prompt",
    )(sb_bias.astype(F32), q_bf, ktb, vtb)


def _xp_kernel(qx_ref, mk_ref, mv_ref, o_ref):
    for h in range(X_HEADS):
        sl = slice(h * X_HEAD_DIM, (h + 1) * X_HEAD_DIM)
        s = _dot_nt(qx_ref[:, sl], mk_ref[:, sl]) * (X_HEAD_DIM ** -0.5)
        e = jnp.exp(s - jnp.max(s, axis=-1, keepdims=True))
        p = e / jnp.sum(e, axis=-1, keepdims=True)
        o_ref[:, sl] = _dot(p.astype(BF16), mv_ref[:, sl]).astype(o_ref.dtype)


def xattn_prompt(qx_bf, mk_bf, mv_bf, seq_len, n_mem, tm):
    n = qx_bf.shape[0]
    tps = seq_len // tm
    xw = X_HEADS * X_HEAD_DIM
    return pl.pallas_call(
        _xp_kernel, grid=(n // tm,),
        in_specs=[pl.BlockSpec((tm, xw), lambda i: (i, 0)),
                  pl.BlockSpec((n_mem, xw), lambda i: (i // tps, 0)),
                  pl.BlockSpec((n_mem, xw), lambda i: (i // tps, 0))],
        out_specs=pl.BlockSpec((tm, xw), lambda i: (i, 0)),
        out_shape=jax.ShapeDtypeStruct((n, xw), BF16),
        compiler_params=_params(("arbitrary",)),
        name="xattn_prompt",
    )(qx_bf, mk_bf, mv_bf)


def _split_bf16(x):
    hi = x.astype(BF16)
    return hi, (x - hi.astype(F32)).astype(BF16)


def _sbs_kernel(new_key_visible, pt_ref, bias_ref, qbd_ref, kn_ref, vn_ref, kp_ref, vp_ref,
                o_ref, acc_ref, car_ref):
    del pt_ref
    b = pl.program_id(0)
    j = pl.program_id(1)
    npages = pl.num_programs(1)
    psz = PAGE_SIZE

    def expand(a):
        return jnp.concatenate(
            [jnp.broadcast_to(a[h:h + 1], (SB_HEAD_DIM, a.shape[1])) for h in range(SB_HEADS)], axis=0)

    @pl.when(j == 0)
    def _():
        mine = lax.broadcasted_iota(I32, (SB_HEADS, kn_ref.shape[1]), 1) == b
        zs = _dot(qbd_ref[0], kn_ref[...].astype(BF16)) + bias_ref[:, 0:1]
        seen = jnp.logical_and(mine, new_key_visible)
        sp0 = jnp.where(seen, _softplus(zs), 0.0)
        a0 = jnp.where(seen, jnp.exp(zs - sp0), 0.0)
        car_ref[...] = jnp.broadcast_to(jnp.sum(sp0, axis=-1, keepdims=True), car_ref.shape)
        new = jnp.sum(expand(a0) * vn_ref[...], axis=-1, keepdims=True)
        lane0 = lax.broadcasted_iota(I32, acc_ref.shape, 1) == 0
        acc_ref[...] = jnp.where(lane0, new, 0.0)

    kp = kp_ref[...]
    z = _dot(qbd_ref[0], kp.astype(BF16)) + bias_ref[...]
    sp = _softplus(z)
    r = lax.broadcasted_iota(I32, (psz, psz), 0)
    c = lax.broadcasted_iota(I32, (psz, psz), 1)
    tri = (r >= c).astype(BF16)
    hi, lo = _split_bf16(sp)
    cum = _dot(hi, tri) + _dot(lo, tri)
    a = jnp.exp(z - (cum + car_ref[...]))
    acc_ref[...] += expand(a) * vp_ref[...]
    car_ref[...] = car_ref[...] + jnp.broadcast_to(cum[:, 0:1], car_ref.shape)

    @pl.when(j == npages - 1)
    def _():
        hi2, lo2 = _split_bf16(acc_ref[...])
        ones = jnp.ones((SUBLANES, psz), BF16)
        y = _dot_nt(ones, hi2) + _dot_nt(ones, lo2)
        o_ref[pl.ds(b, 1), :] = y[0:1]


def sb_sample(q_s, kn_t, vn_t, cache_k_t, cache_v_t, page_table, sb_bias, new_key_visible):
    nb, npages = page_table.shape
    sbw = SB_HEADS * SB_HEAD_DIM
    head_of_col = jnp.arange(sbw) // SB_HEAD_DIM
    sel = head_of_col[None, :] == jnp.arange(SB_HEADS)[:, None]
    qbd = jnp.where(sel[None], q_s[:, None, :], 0.0).astype(BF16)
    assert PAGE_SIZE == LANES
    bias = jnp.broadcast_to(sb_bias.astype(F32)[:, None], (SB_HEADS, LANES))
    grid_spec = pltpu.PrefetchScalarGridSpec(
        num_scalar_prefetch=1,
        grid=(nb, npages),
        in_specs=[pl.BlockSpec((SB_HEADS, LANES), lambda b, j, pt: (0, 0)),
                  pl.BlockSpec((1, SB_HEADS, sbw), lambda b, j, pt: (b, 0, 0)),
                  pl.BlockSpec((sbw, nb), lambda b, j, pt: (0, 0)),
                  pl.BlockSpec((sbw, nb), lambda b, j, pt: (0, 0)),
                  pl.BlockSpec((None, sbw, PAGE_SIZE), lambda b, j, pt: (pt[b, npages - 1 - j], 0, 0)),
                  pl.BlockSpec((None, sbw, PAGE_SIZE), lambda b, j, pt: (pt[b, npages - 1 - j], 0, 0))],
        out_specs=pl.BlockSpec((nb, sbw), lambda b, j, pt: (0, 0)),
        scratch_shapes=[pltpu.VMEM((sbw, PAGE_SIZE), F32), pltpu.VMEM((SB_HEADS, PAGE_SIZE), F32)],
    )
    return pl.pallas_call(
        functools.partial(_sbs_kernel, new_key_visible), grid_spec=grid_spec,
        out_shape=jax.ShapeDtypeStruct((nb, sbw), F32),
        compiler_params=_params(("arbitrary", "arbitrary")),
        name="sb_sample",
    )(page_table, bias, qbd, kn_t, vn_t, cache_k_t, cache_v_t)


def _xs_kernel(qx_ref, mk_ref, mv_ref, o_ref):
    b = pl.program_id(0)
    rows = N_MEM * X_HEADS
    mk = mk_ref[...].astype(BF16)
    mv = mv_ref[...].astype(BF16)
    s = _dot_nt(qx_ref[0], mk) * (X_HEAD_DIM ** -0.5)
    own = ((lax.broadcasted_iota(I32, (SUBLANES, rows), 1) & (X_HEADS - 1))
           == lax.broadcasted_iota(I32, (SUBLANES, rows), 0))
    s = jnp.where(own, s, -jnp.inf)
    m = jnp.max(s, axis=-1, keepdims=True)
    m = jnp.where(m == -jnp.inf, 0.0, m)
    e = jnp.exp(s - m)
    den = jnp.sum(e, axis=-1, keepdims=True)
    p = e / jnp.where(den == 0.0, 1.0, den)
    y = _dot(p.astype(BF16), mv)
    o_ref[pl.ds(b, 1)] = y[None, 0:X_HEADS]


def xattn_sample(qx_bf, cache_mem_k, cache_mem_v):
    nb = qx_bf.shape[0]
    rows = N_MEM * X_HEADS
    mk = cache_mem_k.reshape(nb, rows, X_HEAD_DIM)
    mv = cache_mem_v.reshape(nb, rows, X_HEAD_DIM)
    qx = qx_bf.reshape(nb, X_HEADS, X_HEAD_DIM)
    qx = jnp.concatenate([qx, jnp.zeros((nb, SUBLANES - X_HEADS, X_HEAD_DIM), BF16)], axis=1)
    out = pl.pallas_call(
        _xs_kernel, grid=(nb,),
        in_specs=[pl.BlockSpec((1, SUBLANES, X_HEAD_DIM), lambda b: (b, 0, 0)),
                  pl.BlockSpec((None, rows, X_HEAD_DIM), lambda b: (b, 0, 0)),
                  pl.BlockSpec((None, rows, X_HEAD_DIM), lambda b: (b, 0, 0))],
        out_specs=pl.BlockSpec((nb, X_HEADS, X_HEAD_DIM), lambda b: (0, 0, 0)),
        out_shape=jax.ShapeDtypeStruct((nb, X_HEADS, X_HEAD_DIM), F32),
        compiler_params=_params(("arbitrary",)),
        name="xattn_sample",
    )(qx, mk, mv)
    return out.reshape(nb, X_HEADS * X_HEAD_DIM)


_R_E1, _R_E2, _R_R1, _R_R2, _R_G1, _R_G2 = 0, 1, 2, 3, 4, 5
_GROUP_LANE0 = N_EXPERTS


def _merge_kernel(tm, x_ref, mp_ref, ysb_ref, yx_ref, g1_ref, g2_ref, wb1_ref, wb2_ref, wo_ref,
                  n2_ref, wr_ref, br_ref, cin_ref, x1_ref, h_ref, route_ref, cout_ref, cnt_ref):
    @pl.when(pl.program_id(0) == 0)
    def _():
        cnt_ref[...] = cin_ref[...]

    m = (mp_ref[...] + g1_ref[...] * _dot(ysb_ref[...].astype(BF16), wb1_ref[...])
         + g2_ref[...] * _dot(yx_ref[...].astype(BF16), wb2_ref[...]))
    x1 = x_ref[...] + _dot(m.astype(BF16), wo_ref[...])
    x1_ref[...] = x1
    hn = _rms(x1, n2_ref[...])
    h_ref[...] = hn
    logits = _dot(hn.astype(BF16), wr_ref[...]) + br_ref[...]

    lane = lax.broadcasted_iota(I32, (tm, LANES), 1).astype(F32)
    big = float(LANES)
    ninf = -jnp.inf

    def first_argmax(v):
        mx = jnp.max(v, axis=-1, keepdims=True)
        idx = jnp.min(jnp.where(v == mx, lane, big), axis=-1, keepdims=True)
        return mx, idx

    gl = jnp.where((lane >= _GROUP_LANE0) & (lane < _GROUP_LANE0 + N_GROUPS), logits, ninf)
    gmax, glane = first_argmax(gl)
    gprob = 1.0 / jnp.sum(jnp.exp(gl - gmax), axis=-1, keepdims=True)
    e0 = (glane - _GROUP_LANE0) * EXPERTS_PER_GROUP
    el = jnp.where((lane >= e0) & (lane < e0 + EXPERTS_PER_GROUP), logits, ninf)
    m1, i1 = first_argmax(el)
    m2, i2 = first_argmax(jnp.where(lane == i1, ninf, el))
    e2 = jnp.exp(m2 - m1)
    gate1 = gprob / (1.0 + e2)
    gate2 = gprob * e2 / (1.0 + e2)

    oh1 = (lane == i1).astype(F32)
    oh2 = (lane == i2).astype(F32)
    cnt = oh1 + oh2
    r = lax.broadcasted_iota(I32, (tm, tm), 0)
    c = lax.broadcasted_iota(I32, (tm, tm), 1)
    before = _dot((c < r).astype(BF16), cnt.astype(BF16)) + cnt_ref[...]
    rank1 = jnp.sum(oh1 * before, axis=-1, keepdims=True)
    rank2 = jnp.sum(oh2 * before, axis=-1, keepdims=True)
    cnt_ref[...] = cnt_ref[...] + jnp.sum(cnt, axis=0, keepdims=True)
    cout_ref[...] = cnt_ref[...]

    rec = jnp.zeros((tm, LANES), F32)
    for ln, val in ((_R_E1, i1.astype(F32)), (_R_E2, i2.astype(F32)), (_R_R1, rank1), (_R_R2, rank2),
                    (_R_G1, gate1), (_R_G2, gate2)):
        rec = jnp.where(lane == ln, val, rec)
    route_ref[...] = rec


def merge(x2d, mp, ysb, yx, g1, g2, wb1_bf, wb2_bf, wo_bf, norm2, w_route_bf, b_route, cnt_in, tm):
    n = x2d.shape[0]
    row = lambda w: pl.BlockSpec((tm, w), lambda i: (i, 0))
    return pl.pallas_call(
        functools.partial(_merge_kernel, tm), grid=(n // tm,),
        in_specs=[row(D_MODEL), row(D_MODEL), row(BRANCH_W), row(BRANCH_W), row(D_MODEL), row(D_MODEL),
                  _const_spec((BRANCH_W, D_MODEL)), _const_spec((BRANCH_W, D_MODEL)),
                  _const_spec((D_MODEL, D_MODEL)), _const_spec((1, D_MODEL)),
                  _const_spec((D_MODEL, LANES)), _const_spec((1, LANES)), _const_spec((1, LANES))],
        out_specs=[row(D_MODEL), row(D_MODEL), row(LANES), pl.BlockSpec((1, LANES), lambda i: (0, 0))],
        out_shape=[jax.ShapeDtypeStruct((n, D_MODEL), F32), jax.ShapeDtypeStruct((n, D_MODEL), F32),
                   jax.ShapeDtypeStruct((n, LANES), F32), jax.ShapeDtypeStruct((1, LANES), F32)],
        scratch_shapes=[pltpu.VMEM((1, LANES), F32)],
        compiler_params=_params(("arbitrary",)),
        name="merge",
    )(x2d, mp, ysb, yx, g1, g2, wb1_bf, wb2_bf, wo_bf, norm2.reshape(1, -1), w_route_bf, b_route, cnt_in)


def _row_copy(src, src_row, dst, dst_row, sem):
    return pltpu.make_async_copy(src.at[pl.ds(src_row, 1)], dst.at[pl.ds(dst_row, 1)], sem)


def _dispatch_kernel(tm, dest_ref, h_ref, buf_in_ref, buf_ref, sem):
    del buf_in_ref

    def issue(t, carry):
        _row_copy(h_ref, t, buf_ref, dest_ref[0, 0, 2 * t], sem).start()
        _row_copy(h_ref, t, buf_ref, dest_ref[0, 0, 2 * t + 1], sem).start()
        return carry

    lax.fori_loop(0, tm, issue, 0)
    for _ in range(2):
        pltpu.make_async_copy(h_ref, buf_ref.at[pl.ds(0, tm)], sem).wait()


def dispatch(h, dest, buf_zero, tm):
    n = h.shape[0]
    nt = n // tm
    return pl.pallas_call(
        functools.partial(_dispatch_kernel, tm), grid=(nt,),
        in_specs=[pl.BlockSpec((1, 1, 2 * tm), lambda i: (i, 0, 0), memory_space=pltpu.SMEM),
                  pl.BlockSpec((tm, D_MODEL), lambda i: (i, 0)),
                  pl.BlockSpec(memory_space=pl.ANY)],
        out_specs=pl.BlockSpec(memory_space=pl.ANY),
        out_shape=jax.ShapeDtypeStruct(buf_zero.shape, buf_zero.dtype),
        scratch_shapes=[pltpu.SemaphoreType.DMA(())],
        input_output_aliases={2: 0},
        compiler_params=_params(("arbitrary",)),
        name="dispatch",
    )(dest.reshape(nt, 1, 2 * tm), h, buf_zero)


def _experts_kernel(be_ref, nu_ref, x_ref, w1_ref, w3_ref, w2_ref, y_ref):
    del be_ref

    @pl.when(pl.program_id(0) < nu_ref[0])
    def _():
        xb = x_ref[...].astype(BF16)
        h1 = _dot(xb, w1_ref[...].astype(BF16))
        h3 = _dot(xb, w3_ref[...].astype(BF16))
        act = (h1 * jax.nn.sigmoid(h1)) * h3
        y_ref[...] = _dot(act.astype(BF16), w2_ref[...].astype(BF16))

    @pl.when(pl.program_id(0) >= nu_ref[0])
    def _():
        y_ref[...] = jnp.zeros_like(y_ref)


def experts(buf, block_e, n_used, w1, w3, w2):
    nblk = buf.shape[0] // MOE_BM
    clamp = lambda i, nu: jnp.minimum(i, nu[0] - 1)
    grid_spec = pltpu.PrefetchScalarGridSpec(
        num_scalar_prefetch=2,
        grid=(nblk,),
        in_specs=[pl.BlockSpec((MOE_BM, D_MODEL), lambda i, be, nu: (clamp(i, nu), 0)),
                  pl.BlockSpec((None, D_MODEL, D_EXPERT), lambda i, be, nu: (be[i], 0, 0)),
                  pl.BlockSpec((None, D_MODEL, D_EXPERT), lambda i, be, nu: (be[i], 0, 0)),
                  pl.BlockSpec((None, D_EXPERT, D_MODEL), lambda i, be, nu: (be[i], 0, 0))],
        out_specs=pl.BlockSpec((MOE_BM, D_MODEL), lambda i, be, nu: (i, 0)),
    )
    return pl.pallas_call(
        _experts_kernel, grid_spec=grid_spec,
        out_shape=jax.ShapeDtypeStruct(buf.shape, F32),
        compiler_params=_params(("arbitrary",)),
        name="experts",
    )(block_e, n_used, buf, w1, w3, w2)


def _combine_kernel(tm, dest_ref, x1_ref, route_ref, ybuf_ref, o_ref, rows_ref, sem):
    def issue(t, carry):
        _row_copy(ybuf_ref, dest_ref[0, 0, 2 * t], rows_ref.at[0], t, sem).start()
        _row_copy(ybuf_ref, dest_ref[0, 0, 2 * t + 1], rows_ref.at[1], t, sem).start()
        return carry

    lax.fori_loop(0, tm, issue, 0)
    for k in range(2):
        pltpu.make_async_copy(ybuf_ref.at[pl.ds(0, tm)], rows_ref.at[k], sem).wait()
    route = route_ref[...]
    o_ref[...] = (x1_ref[...] + route[:, _R_G1:_R_G1 + 1] * rows_ref[0]
                  + route[:, _R_G2:_R_G2 + 1] * rows_ref[1])


def combine(x1, route, dest, ybuf, tm):
    n = x1.shape[0]
    nt = n // tm
    return pl.pallas_call(
        functools.partial(_combine_kernel, tm), grid=(nt,),
        in_specs=[pl.BlockSpec((1, 1, 2 * tm), lambda i: (i, 0, 0), memory_space=pltpu.SMEM),
                  pl.BlockSpec((tm, D_MODEL), lambda i: (i, 0)),
                  pl.BlockSpec((tm, LANES), lambda i: (i, 0)),
                  pl.BlockSpec(memory_space=pl.ANY)],
        out_specs=pl.BlockSpec((tm, D_MODEL), lambda i: (i, 0)),
        out_shape=jax.ShapeDtypeStruct((n, D_MODEL), F32),
        scratch_shapes=[pltpu.VMEM((2, tm, D_MODEL), F32), pltpu.SemaphoreType.DMA(())],
        compiler_params=_params(("arbitrary",)),
        name="combine",
    )(dest.reshape(nt, 1, 2 * tm), x1, route, ybuf)


def kernel(x_prompt, x_sample, mem_prompt, cache_k, cache_v, cache_mem_k, cache_mem_v, state_conv, page_table, norm1, w_in, conv_w, sb_bias, mem_norm, w_mem_kv, q_norm, k_norm, w_branch, w_o, norm2, w_group, b_group, w_router, b_router, w1, w3, w2):
    nb, seq, _ = x_prompt.shape
    ns, dec_seq, _ = x_sample.shape
    assert dec_seq == 1 and seq % SB_TQ == 0 and SB_TQ == SB_TK
    n_mem = mem_prompt.shape[1]
    sbw = SB_HEADS * SB_HEAD_DIM
    xw = X_HEADS * X_HEAD_DIM
    past = page_table.shape[1] * PAGE_SIZE

    cw = BRANCH_W
    c_k, c_v, c_qx, c_gl = 3 * cw + sbw, 3 * cw + 2 * sbw, 3 * cw + 3 * sbw, 3 * cw + 3 * sbw + xw
    w_main = jnp.concatenate([w_in[:, :3 * cw + sbw], w_in[:, c_qx:]], axis=1).astype(BF16)
    w_kvt = w_in[:, c_k:c_qx].T.astype(BF16)
    del c_v, c_gl
    wb = w_branch.astype(BF16)
    wo_bf = w_o.astype(BF16)
    w_route = jnp.zeros((D_MODEL, LANES), F32)
    w_route = w_route.at[:, :N_EXPERTS].set(w_router).at[:, N_EXPERTS:N_EXPERTS + N_GROUPS].set(w_group)
    b_route = jnp.zeros((1, LANES), F32)
    b_route = b_route.at[0, :N_EXPERTS].set(b_router).at[0, N_EXPERTS:N_EXPERTS + N_GROUPS].set(b_group)
    w_route = w_route.astype(BF16)

    xp = x_prompt.reshape(nb * seq, D_MODEL)
    mk_p, mv_p = memkv(mem_prompt.reshape(nb * n_mem, D_MODEL), mem_norm, w_mem_kv.astype(BF16), k_norm, n_mem)
    (kt_p, vt_p, q_p, qx_p, g1_p, g2_p, mp_p, tail_p, ktb_p, vtb_p) = inproj(
        xp, norm1, w_main, w_kvt, wb[0], conv_w, q_norm, seq, 512)
    ysb_p = sb_prompt(q_p, ktb_p, vtb_p, sb_bias, seq)
    yx_p = xattn_prompt(qx_p, mk_p.astype(BF16), mv_p.astype(BF16), seq, n_mem, 512)

    xs = x_sample.reshape(ns, D_MODEL)
    (kt_s, vt_s, q_s, qx_s, g1_s, g2_s, mp_s, u_s) = inproj(
        xs, norm1, w_main, w_kvt, wb[0], conv_w, q_norm, ns, ns,
        conv_state=(state_conv[:, 0], state_conv[:, 1]))
    ck_t = cache_k.transpose(0, 2, 3, 1).reshape(cache_k.shape[0], sbw, PAGE_SIZE)
    cv_t = cache_v.transpose(0, 2, 3, 1).reshape(cache_v.shape[0], sbw, PAGE_SIZE)
    ysb_s = sb_sample(q_s.astype(F32), kt_s[0], vt_s[0], ck_t, cv_t, page_table, sb_bias,
                      new_key_visible=(past < past))
    yx_s = xattn_sample(qx_s, cache_mem_k, cache_mem_v)

    cnt0 = jnp.zeros((1, LANES), F32)
    x1_p, h_p, route_p, cnt1 = merge(xp, mp_p, ysb_p, yx_p, g1_p, g2_p, wb[1], wb[2], wo_bf, norm2,
                                     w_route, b_route, cnt0, 512)
    x1_s, h_s, route_s, cnt2 = merge(xs, mp_s, ysb_s, yx_s, g1_s, g2_s, wb[1], wb[2], wo_bf, norm2,
                                     w_route, b_route, cnt1, ns)

    n_slots = 2 * (nb * seq + ns)
    nblk = -(-(n_slots + N_EXPERTS * (MOE_BM - 1)) // MOE_BM)
    counts = cnt2[0, :N_EXPERTS].astype(I32)
    padded = (counts + MOE_BM - 1) // MOE_BM * MOE_BM
    ends = jnp.cumsum(padded)
    starts = ends - padded
    n_used = (ends[-1] // MOE_BM).astype(I32).reshape(1)
    blk_start = jnp.arange(nblk, dtype=I32) * MOE_BM
    block_e = jnp.sum(blk_start[:, None] >= ends[None, :], axis=1).astype(I32)
    block_e = jnp.minimum(block_e, block_e[jnp.maximum(n_used[0] - 1, 0)])

    def dests(route):
        e = route[:, _R_E1:_R_E2 + 1].astype(I32)
        rk = route[:, _R_R1:_R_R2 + 1].astype(I32)
        return (starts[e] + rk).reshape(-1)

    dest_p, dest_s = dests(route_p), dests(route_s)
    buf = jnp.zeros((nblk * MOE_BM, D_MODEL), F32)
    buf = dispatch(h_p, dest_p, buf, 512)
    buf = dispatch(h_s, dest_s, buf, ns)
    ybuf = experts(buf, block_e, n_used, w1, w3, w2)
    y_p = combine(x1_p, route_p, dest_p, ybuf, 256)
    y_s = combine(x1_s, route_s, dest_s, ybuf, ns)

    def heads_last(t, b, s):
        return t.reshape(b, SB_HEADS, SB_HEAD_DIM, s).transpose(0, 3, 1, 2)

    conv_prompt = tail_p[:, SUBLANES - 2:, :]
    conv_sample = jnp.stack([state_conv[:, 1], u_s[0]], axis=1)
    return (y_p.reshape(nb, seq, D_MODEL), y_s.reshape(ns, 1, D_MODEL),
            heads_last(kt_p, nb, seq), heads_last(vt_p, nb, seq), conv_prompt,
            mk_p.reshape(nb, n_mem, X_HEADS, X_HEAD_DIM), mv_p.reshape(nb, n_mem, X_HEADS, X_HEAD_DIM),
            heads_last(kt_s, 1, ns).reshape(ns, 1, SB_HEADS, SB_HEAD_DIM),
            heads_last(vt_s, 1, ns).reshape(ns, 1, SB_HEADS, SB_HEAD_DIM),
            conv_sample)
```

```python
import functools

import jax
import jax.numpy as jnp
from jax import lax
from jax.experimental import pallas as pl
from jax.experimental.pallas import tpu as pltpu

F32 = jnp.float32
BF16 = jnp.bfloat16
I32 = jnp.int32

EPS = 1e-6
D_MODEL = 1024
BRANCH_W = 512
SB_HEADS = 8
SB_HEAD_DIM = 64
X_HEADS = 4
X_HEAD_DIM = 128
N_MEM = 256
N_GROUPS = 4
EXPERTS_PER_GROUP = 8
N_EXPERTS = N_GROUPS * EXPERTS_PER_GROUP
D_EXPERT = 512
PAGE_SIZE = 128

LANES = 128
SUBLANES = 8
VMEM_LIMIT = 56 * 1024 * 1024

SB_TQ = 256
SB_TK = 256
MOE_BM = 256
ROW_DMA_UNROLL = 8


def _dot(a, b):
    return jnp.dot(a, b, preferred_element_type=F32)


def _dot_nt(a, b):
    return lax.dot_general(a, b, (((1,), (1,)), ((), ())), preferred_element_type=F32)


def _rms(x, g):
    return x * lax.rsqrt(jnp.mean(x * x, axis=-1, keepdims=True) + EPS) * g


def _softplus(z):
    return jnp.maximum(z, 0.0) + jnp.log(1.0 + jnp.exp(-jnp.abs(z)))


def _params(sem, vmem=VMEM_LIMIT):
    return pltpu.CompilerParams(dimension_semantics=sem, vmem_limit_bytes=vmem)


def _const_spec(shape):
    nd = len(shape)
    return pl.BlockSpec(shape, lambda *_: (0,) * nd, pipeline_mode=pl.Buffered(1))


def _memkv_kernel(mem_ref, g_ref, w_ref, kn_ref, mk_ref, mv_ref):
    xn = _rms(mem_ref[...], g_ref[...]).astype(BF16)
    kv = _dot(xn, w_ref[...])
    xw = X_HEADS * X_HEAD_DIM
    for h in range(X_HEADS):
        sl = slice(h * X_HEAD_DIM, (h + 1) * X_HEAD_DIM)
        mk_ref[:, sl] = _rms(kv[:, sl], kn_ref[...])
    mv_ref[...] = kv[:, xw:]


def memkv(mem2d, mem_norm, w_mem_kv_bf, k_norm, n_mem):
    n = mem2d.shape[0]
    xw = X_HEADS * X_HEAD_DIM
    return pl.pallas_call(
        _memkv_kernel,
        grid=(n // n_mem,),
        in_specs=[pl.BlockSpec((n_mem, D_MODEL), lambda i: (i, 0)),
                  _const_spec((1, D_MODEL)),
                  _const_spec((D_MODEL, 2 * xw)),
                  _const_spec((1, X_HEAD_DIM))],
        out_specs=[pl.BlockSpec((n_mem, xw), lambda i: (i, 0)),
                   pl.BlockSpec((n_mem, xw), lambda i: (i, 0))],
        out_shape=[jax.ShapeDtypeStruct((n, xw), F32), jax.ShapeDtypeStruct((n, xw), F32)],
        compiler_params=_params(("arbitrary",)),
        name="memkv",
    )(mem2d, mem_norm.reshape(1, -1), w_mem_kv_bf, k_norm.reshape(1, -1))


_C_UIN, _C_CG, _C_BG, _C_Q, _C_K, _C_QX, _C_GL = 0, 512, 1024, 1536, 2048, 3072, 3584
_W_MAIN = _C_GL + 3 * D_MODEL


def _inproj_kernel(carry_conv, tiles_per_seq, tm, *refs):
    if carry_conv:
        (x_ref, n1_ref, wm_ref, wkvt_ref, wb0_ref, cw_ref, qn_ref,
         kt_ref, vt_ref, q_ref, qx_ref, g1_ref, g2_ref, mp_ref, tail_ref, ktb_ref, vtb_ref,
         carry_ref) = refs
    else:
        (x_ref, n1_ref, wm_ref, wkvt_ref, wb0_ref, cw_ref, qn_ref, um2_ref, um1_ref,
         kt_ref, vt_ref, q_ref, qx_ref, g1_ref, g2_ref, mp_ref, tail_ref) = refs

    xn = _rms(x_ref[...], n1_ref[...]).astype(BF16)

    def proj(c0, width):
        return _dot(xn, wm_ref[:, c0:c0 + width])

    kvt = _dot_nt(wkvt_ref[...], xn)
    sbw = SB_HEADS * SB_HEAD_DIM
    kt_ref[...] = kvt[:sbw]
    vt_ref[...] = kvt[sbw:]
    if carry_conv:
        for c in range(tm // SB_TK):
            ktb_ref[c] = kvt[:sbw, c * SB_TK:(c + 1) * SB_TK].astype(BF16)
            vtb_ref[c] = kvt[sbw:, c * SB_TK:(c + 1) * SB_TK].astype(BF16)

    q_ref[...] = (proj(_C_Q, sbw) * (SB_HEAD_DIM ** -0.5)).astype(BF16)

    qx = proj(_C_QX, X_HEADS * X_HEAD_DIM)
    for h in range(X_HEADS):
        sl = slice(h * X_HEAD_DIM, (h + 1) * X_HEAD_DIM)
        qx_ref[:, sl] = _rms(qx[:, sl], qn_ref[...]).astype(BF16)

    u = proj(_C_CG, BRANCH_W) * proj(_C_UIN, BRANCH_W)
    if carry_conv:
        @pl.when(pl.program_id(0) % tiles_per_seq == 0)
        def _():
            carry_ref[...] = jnp.zeros_like(carry_ref)
        ext = jnp.concatenate([carry_ref[...], u], axis=0)
        um1 = pltpu.roll(ext, 1, 0)[SUBLANES:]
        um2 = pltpu.roll(ext, 2, 0)[SUBLANES:]
        carry_ref[...] = u[tm - SUBLANES:]
        tail_ref[...] = u[tm - SUBLANES:]
    else:
        um1 = um1_ref[...]
        um2 = um2_ref[...]
        tail_ref[...] = u
    cw = cw_ref[...]
    y_conv = proj(_C_BG, BRANCH_W) * (um2 * cw[0:1] + um1 * cw[1:2] + u * cw[2:3])

    g0 = jax.nn.sigmoid(proj(_C_GL, D_MODEL))
    mp_ref[...] = g0 * _dot(y_conv.astype(BF16), wb0_ref[...])
    g1_ref[...] = jax.nn.sigmoid(proj(_C_GL + D_MODEL, D_MODEL))
    g2_ref[...] = jax.nn.sigmoid(proj(_C_GL + 2 * D_MODEL, D_MODEL))


def inproj(x2d, norm1, w_main_bf, w_kvt_bf, wb0_bf, conv_w, q_norm, seq_len, tm, conv_state=None):
    n = x2d.shape[0]
    carry = conv_state is None
    nseq = n // seq_len if carry else 1
    slen = seq_len if carry else n
    tps = slen // tm
    sbw = SB_HEADS * SB_HEAD_DIM
    xw = X_HEADS * X_HEAD_DIM
    nkb = max(tm // SB_TK, 1)
    tail_rows = SUBLANES if carry else tm

    row = lambda w: pl.BlockSpec((tm, w), lambda i: (i, 0))
    kt_spec = pl.BlockSpec((None, sbw, tm), lambda i: (i // tps, 0, i % tps))
    ktb_spec = pl.BlockSpec((None, nkb, sbw, SB_TK), lambda i: (i // tps, i % tps, 0, 0))
    in_specs = [row(D_MODEL), _const_spec((1, D_MODEL)), _const_spec((D_MODEL, _W_MAIN)),
                _const_spec((2 * sbw, D_MODEL)), _const_spec((BRANCH_W, D_MODEL)),
                _const_spec((3, BRANCH_W)), _const_spec((1, X_HEAD_DIM))]
    args = [x2d, norm1.reshape(1, -1), w_main_bf, w_kvt_bf, wb0_bf, conv_w, q_norm.reshape(1, -1)]
    scratch = []
    if carry:
        scratch = [pltpu.VMEM((SUBLANES, BRANCH_W), F32)]
    else:
        in_specs += [row(BRANCH_W), row(BRANCH_W)]
        args += list(conv_state)
    out_specs = [kt_spec, kt_spec, row(sbw), row(xw),
                 row(D_MODEL), row(D_MODEL), row(D_MODEL),
                 pl.BlockSpec((None, tail_rows, BRANCH_W), lambda i: (i // tps, 0, 0))]
    out_shape = [jax.ShapeDtypeStruct((nseq, sbw, slen), F32), jax.ShapeDtypeStruct((nseq, sbw, slen), F32),
                 jax.ShapeDtypeStruct((n, sbw), BF16), jax.ShapeDtypeStruct((n, xw), BF16),
                 jax.ShapeDtypeStruct((n, D_MODEL), F32), jax.ShapeDtypeStruct((n, D_MODEL), F32),
                 jax.ShapeDtypeStruct((n, D_MODEL), F32),
                 jax.ShapeDtypeStruct((nseq, tail_rows, BRANCH_W), F32)]
    if carry:
        out_specs += [ktb_spec, ktb_spec]
        out_shape += [jax.ShapeDtypeStruct((nseq, slen // SB_TK, sbw, SB_TK), BF16)] * 2
    return pl.pallas_call(
        functools.partial(_inproj_kernel, carry, tps, tm),
        grid=(n // tm,),
        in_specs=in_specs, out_specs=out_specs, out_shape=out_shape,
        scratch_shapes=scratch,
        compiler_params=_params(("arbitrary",)),
        name="inproj_prompt" if carry else "inproj_sample",
    )(*args)


def _sbp_kernel(bias_ref, q_ref, kt_ref, vt_ref, o_ref, acc_ref, car_ref):
    pair = pl.program_id(1)
    qi = pl.program_id(2)
    tq, tk = SB_TQ, SB_TK
    lane = lax.broadcasted_iota(I32, (tq, LANES), 1)
    q = q_ref[...]
    zero = jnp.zeros_like(q)
    qh = (jnp.where(lane < SB_HEAD_DIM, q, zero), jnp.where(lane >= SB_HEAD_DIM, q, zero))
    r = lax.broadcasted_iota(I32, (tk, tk), 0)
    c = lax.broadcasted_iota(I32, (tk, tk), 1)
    tri = (r >= c).astype(BF16)
    causal = c < r
    acc_ref[...] = jnp.zeros_like(acc_ref)
    car_ref[...] = jnp.zeros_like(car_ref)

    def block(j, masked):
        kt = kt_ref[j]
        vt = vt_ref[j]
        for h in range(2):
            z = _dot(qh[h], kt) + bias_ref[2 * pair + h]
            sp = _softplus(z)
            if masked:
                sp = jnp.where(causal, sp, 0.0)
            car = car_ref[h]
            cum = _dot(sp.astype(BF16), tri)
            a = jnp.exp(z - (cum + jnp.concatenate([car] * (tk // LANES), axis=1)))
            if masked:
                a = jnp.where(causal, a, 0.0)
            acc_ref[h] += _dot_nt(a.astype(BF16), vt)
            car_ref[h] = car + jnp.broadcast_to(cum[:, 0:1], (tq, LANES))

    block(qi, True)

    def body(t, carry):
        block(qi - 1 - 2 * t, False)
        block(qi - 2 - 2 * t, False)
        return carry

    lax.fori_loop(0, lax.shift_right_logical(qi, 1), body, 0)

    @pl.when((qi & 1) == 1)
    def _():
        block(0, False)
    o_ref[...] = jnp.where(lane < SB_HEAD_DIM, acc_ref[0], acc_ref[1]).astype(o_ref.dtype)


def sb_prompt(q_bf, ktb, vtb, sb_bias, seq_len):
    n = q_bf.shape[0]
    nseq = n // seq_len
    nq = seq_len // SB_TQ
    nk = seq_len // SB_TK
    npair = SB_HEADS // 2
    grid_spec = pltpu.PrefetchScalarGridSpec(
        num_scalar_prefetch=0,
        grid=(nseq, npair, nq),
        in_specs=[pl.BlockSpec(memory_space=pltpu.SMEM),
                  pl.BlockSpec((SB_TQ, LANES), lambda b, p, i: (b * nq + i, p)),
                  pl.BlockSpec((None, nk, LANES, SB_TK), lambda b, p, i: (b, 0, p, 0)),
                  pl.BlockSpec((None, nk, LANES, SB_TK), lambda b, p, i: (b, 0, p, 0))],
        out_specs=pl.BlockSpec((SB_TQ, LANES), lambda b, p, i: (b * nq + i, p)),
        scratch_shapes=[pltpu.VMEM((2, SB_TQ, LANES), F32), pltpu.VMEM((2, SB_TQ, LANES), F32)],
    )
    return pl.pallas_call(
        _sbp_kernel, grid_spec=grid_spec,
        out_shape=jax.ShapeDtypeStruct((n, SB_HEADS * SB_HEAD_DIM), BF16),
        compiler_params=_params(("arbitrary", "arbitrary", "arbitrary")),
        name="sb_prompt",
    )(sb_bias.astype(F32), q_bf, ktb, vtb)


def _xp_kernel(qx_ref, mk_ref, mv_ref, o_ref):
    for h in range(X_HEADS):
        sl = slice(h * X_HEAD_DIM, (h + 1) * X_HEAD_DIM)
        s = _dot_nt(qx_ref[:, sl], mk_ref[:, sl]) * (X_HEAD_DIM ** -0.5)
        e = jnp.exp(s - jnp.max(s, axis=-1, keepdims=True))
        p = e / jnp.sum(e, axis=-1, keepdims=True)
        o_ref[:, sl] = _dot(p.astype(BF16), mv_ref[:, sl]).astype(o_ref.dtype)


def xattn_prompt(qx_bf, mk_bf, mv_bf, seq_len, n_mem, tm):
    n = qx_bf.shape[0]
    tps = seq_len // tm
    xw = X_HEADS * X_HEAD_DIM
    return pl.pallas_call(
        _xp_kernel, grid=(n // tm,),
        in_specs=[pl.BlockSpec((tm, xw), lambda i: (i, 0)),
                  pl.BlockSpec((n_mem, xw), lambda i: (i // tps, 0)),
                  pl.BlockSpec((n_mem, xw), lambda i: (i // tps, 0))],
        out_specs=pl.BlockSpec((tm, xw), lambda i: (i, 0)),
        out_shape=jax.ShapeDtypeStruct((n, xw), BF16),
        compiler_params=_params(("arbitrary",)),
        name="xattn_prompt",
    )(qx_bf, mk_bf, mv_bf)


def _split_bf16(x):
    hi = x.astype(BF16)
    return hi, (x - hi.astype(F32)).astype(BF16)


def _sbs_kernel(new_key_visible, npages, pt_ref, bias_ref, qbd_ref, kn_ref, vn_ref, *refs):
    del pt_ref
    k_refs, v_refs, o_ref = refs[:npages], refs[npages:2 * npages], refs[2 * npages]
    b = pl.program_id(0)
    psz = PAGE_SIZE
    sbw = SB_HEADS * SB_HEAD_DIM
    rows = npages * SB_HEADS
    head_bits = SB_HEADS.bit_length() - 1
    qbd = qbd_ref[0]
    bias = bias_ref[...]

    mine = lax.broadcasted_iota(I32, (SB_HEADS, kn_ref.shape[1]), 1) == b
    zs = _dot(qbd, kn_ref[...].astype(BF16)) + bias[:, 0:1]
    seen = jnp.logical_and(mine, new_key_visible)
    sp0 = jnp.where(seen, _softplus(zs), 0.0)
    a0 = jnp.where(seen, jnp.exp(zs - sp0), 0.0)
    sp0_tot = jnp.sum(sp0, axis=-1, keepdims=True)
    a0_rows = jnp.concatenate(
        [jnp.broadcast_to(a0[h:h + 1], (SB_HEAD_DIM, a0.shape[1])) for h in range(SB_HEADS)], axis=0)
    new = jnp.sum(a0_rows * vn_ref[...], axis=-1, keepdims=True)

    z = jnp.concatenate([_dot(qbd, k_refs[j][...].astype(BF16)) + bias for j in range(npages)], axis=0)
    sp = _softplus(z)
    r = lax.broadcasted_iota(I32, (psz, psz), 0)
    c = lax.broadcasted_iota(I32, (psz, psz), 1)
    tri = (r >= c).astype(BF16)
    hi, lo = _split_bf16(sp)
    cum = _dot(hi, tri) + _dot(lo, tri)
    tot = jnp.broadcast_to(cum[:, 0:1], (rows, psz))
    rr = lax.broadcasted_iota(I32, (rows, rows), 0)
    cc = lax.broadcasted_iota(I32, (rows, rows), 1)
    later = jnp.logical_and((cc >> head_bits) > (rr >> head_bits),
                            (cc & (SB_HEADS - 1)) == (rr & (SB_HEADS - 1))).astype(BF16)
    th, tl = _split_bf16(tot)
    carry = _dot(later, th) + _dot(later, tl) + jnp.concatenate([sp0_tot] * npages, axis=0)
    a = jnp.exp(z - cum - carry)

    accs = []
    for h in range(SB_HEADS):
        hs = slice(h * SB_HEAD_DIM, (h + 1) * SB_HEAD_DIM)
        acc = jnp.zeros((SB_HEAD_DIM, psz), F32)
        for j in range(npages):
            rj = j * SB_HEADS + h
            acc = acc + jnp.broadcast_to(a[rj:rj + 1], (SB_HEAD_DIM, psz)) * v_refs[j][hs, :]
        accs.append(acc)
    acc = jnp.concatenate(accs, axis=0)
    lane0 = lax.broadcasted_iota(I32, (sbw, psz), 1) == 0
    acc = acc + jnp.where(lane0, new, 0.0)
    hi2, lo2 = _split_bf16(acc)
    ones = jnp.ones((SUBLANES, psz), BF16)
    y = _dot_nt(ones, hi2) + _dot_nt(ones, lo2)
    o_ref[pl.ds(b, 1), :] = y[0:1]


def sb_sample(q_s, kn_t, vn_t, cache_k_t, cache_v_t, page_table, sb_bias, new_key_visible):
    nb, npages = page_table.shape
    sbw = SB_HEADS * SB_HEAD_DIM
    head_of_col = jnp.arange(sbw) // SB_HEAD_DIM
    sel = head_of_col[None, :] == jnp.arange(SB_HEADS)[:, None]
    qbd = jnp.where(sel[None], q_s[:, None, :], 0.0).astype(BF16)
    assert PAGE_SIZE == LANES
    bias = jnp.broadcast_to(sb_bias.astype(F32)[:, None], (SB_HEADS, LANES))
    page = lambda j: pl.BlockSpec((None, sbw, PAGE_SIZE), lambda b, pt: (pt[b, j], 0, 0))
    grid_spec = pltpu.PrefetchScalarGridSpec(
        num_scalar_prefetch=1,
        grid=(nb,),
        in_specs=[pl.BlockSpec((SB_HEADS, LANES), lambda b, pt: (0, 0)),
                  pl.BlockSpec((1, SB_HEADS, sbw), lambda b, pt: (b, 0, 0)),
                  pl.BlockSpec((sbw, nb), lambda b, pt: (0, 0)),
                  pl.BlockSpec((sbw, nb), lambda b, pt: (0, 0))]
                 + [page(j) for j in range(npages)] * 2,
        out_specs=pl.BlockSpec((nb, sbw), lambda b, pt: (0, 0)),
    )
    return pl.pallas_call(
        functools.partial(_sbs_kernel, new_key_visible, npages), grid_spec=grid_spec,
        out_shape=jax.ShapeDtypeStruct((nb, sbw), F32),
        compiler_params=_params(("arbitrary",)),
        name="sb_sample",
    )(page_table, bias, qbd, kn_t, vn_t, *([cache_k_t] * npages), *([cache_v_t] * npages))


def _xs_kernel(qx_ref, mk_ref, mv_ref, o_ref):
    b = pl.program_id(0)
    rows = N_MEM * X_HEADS
    mk = mk_ref[...].astype(BF16)
    mv = mv_ref[...].astype(BF16)
    s = _dot_nt(qx_ref[0], mk) * (X_HEAD_DIM ** -0.5)
    own = ((lax.broadcasted_iota(I32, (SUBLANES, rows), 1) & (X_HEADS - 1))
           == lax.broadcasted_iota(I32, (SUBLANES, rows), 0))
    s = jnp.where(own, s, -jnp.inf)
    m = jnp.max(s, axis=-1, keepdims=True)
    m = jnp.where(m == -jnp.inf, 0.0, m)
    e = jnp.exp(s - m)
    den = jnp.sum(e, axis=-1, keepdims=True)
    p = e / jnp.where(den == 0.0, 1.0, den)
    y = _dot(p.astype(BF16), mv)
    o_ref[pl.ds(b, 1)] = y[None, 0:X_HEADS]


def xattn_sample(qx_bf, cache_mem_k, cache_mem_v):
    nb = qx_bf.shape[0]
    rows = N_MEM * X_HEADS
    mk = cache_mem_k.reshape(nb, rows, X_HEAD_DIM)
    mv = cache_mem_v.reshape(nb, rows, X_HEAD_DIM)
    qx = qx_bf.reshape(nb, X_HEADS, X_HEAD_DIM)
    qx = jnp.concatenate([qx, jnp.zeros((nb, SUBLANES - X_HEADS, X_HEAD_DIM), BF16)], axis=1)
    out = pl.pallas_call(
        _xs_kernel, grid=(nb,),
        in_specs=[pl.BlockSpec((1, SUBLANES, X_HEAD_DIM), lambda b: (b, 0, 0)),
                  pl.BlockSpec((None, rows, X_HEAD_DIM), lambda b: (b, 0, 0)),
                  pl.BlockSpec((None, rows, X_HEAD_DIM), lambda b: (b, 0, 0))],
        out_specs=pl.BlockSpec((nb, X_HEADS, X_HEAD_DIM), lambda b: (0, 0, 0)),
        out_shape=jax.ShapeDtypeStruct((nb, X_HEADS, X_HEAD_DIM), F32),
        compiler_params=_params(("arbitrary",)),
        name="xattn_sample",
    )(qx, mk, mv)
    return out.reshape(nb, X_HEADS * X_HEAD_DIM)


_R_E1, _R_E2, _R_R1, _R_R2, _R_G1, _R_G2 = 0, 1, 2, 3, 4, 5
_GROUP_LANE0 = N_EXPERTS


def _merge_kernel(tm, x_ref, mp_ref, ysb_ref, yx_ref, g1_ref, g2_ref, wb1_ref, wb2_ref, wo_ref,
                  n2_ref, wr_ref, br_ref, cin_ref, x1_ref, h_ref, route_ref, cout_ref, cnt_ref):
    @pl.when(pl.program_id(0) == 0)
    def _():
        cnt_ref[...] = cin_ref[...]

    m = (mp_ref[...] + g1_ref[...] * _dot(ysb_ref[...].astype(BF16), wb1_ref[...])
         + g2_ref[...] * _dot(yx_ref[...].astype(BF16), wb2_ref[...]))
    x1 = x_ref[...] + _dot(m.astype(BF16), wo_ref[...])
    x1_ref[...] = x1
    hn = _rms(x1, n2_ref[...])
    h_ref[...] = hn
    logits = _dot(hn.astype(BF16), wr_ref[...]) + br_ref[...]

    lane = lax.broadcasted_iota(I32, (tm, LANES), 1).astype(F32)
    big = float(LANES)
    ninf = -jnp.inf

    def first_argmax(v):
        mx = jnp.max(v, axis=-1, keepdims=True)
        idx = jnp.min(jnp.where(v == mx, lane, big), axis=-1, keepdims=True)
        return mx, idx

    gl = jnp.where((lane >= _GROUP_LANE0) & (lane < _GROUP_LANE0 + N_GROUPS), logits, ninf)
    gmax, glane = first_argmax(gl)
    gprob = 1.0 / jnp.sum(jnp.exp(gl - gmax), axis=-1, keepdims=True)
    e0 = (glane - _GROUP_LANE0) * EXPERTS_PER_GROUP
    el = jnp.where((lane >= e0) & (lane < e0 + EXPERTS_PER_GROUP), logits, ninf)
    m1, i1 = first_argmax(el)
    m2, i2 = first_argmax(jnp.where(lane == i1, ninf, el))
    e2 = jnp.exp(m2 - m1)
    gate1 = gprob / (1.0 + e2)
    gate2 = gprob * e2 / (1.0 + e2)

    oh1 = (lane == i1).astype(F32)
    oh2 = (lane == i2).astype(F32)
    cnt = oh1 + oh2
    r = lax.broadcasted_iota(I32, (tm, tm), 0)
    c = lax.broadcasted_iota(I32, (tm, tm), 1)
    before = _dot((c < r).astype(BF16), cnt.astype(BF16)) + cnt_ref[...]
    rank1 = jnp.sum(oh1 * before, axis=-1, keepdims=True)
    rank2 = jnp.sum(oh2 * before, axis=-1, keepdims=True)
    cnt_ref[...] = cnt_ref[...] + jnp.sum(cnt, axis=0, keepdims=True)
    cout_ref[...] = cnt_ref[...]

    rec = jnp.zeros((tm, LANES), F32)
    for ln, val in ((_R_E1, i1.astype(F32)), (_R_E2, i2.astype(F32)), (_R_R1, rank1), (_R_R2, rank2),
                    (_R_G1, gate1), (_R_G2, gate2)):
        rec = jnp.where(lane == ln, val, rec)
    route_ref[...] = rec


def merge(x2d, mp, ysb, yx, g1, g2, wb1_bf, wb2_bf, wo_bf, norm2, w_route_bf, b_route, cnt_in, tm):
    n = x2d.shape[0]
    row = lambda w: pl.BlockSpec((tm, w), lambda i: (i, 0))
    return pl.pallas_call(
        functools.partial(_merge_kernel, tm), grid=(n // tm,),
        in_specs=[row(D_MODEL), row(D_MODEL), row(BRANCH_W), row(BRANCH_W), row(D_MODEL), row(D_MODEL),
                  _const_spec((BRANCH_W, D_MODEL)), _const_spec((BRANCH_W, D_MODEL)),
                  _const_spec((D_MODEL, D_MODEL)), _const_spec((1, D_MODEL)),
                  _const_spec((D_MODEL, LANES)), _const_spec((1, LANES)), _const_spec((1, LANES))],
        out_specs=[row(D_MODEL), row(D_MODEL), row(LANES), pl.BlockSpec((1, LANES), lambda i: (0, 0))],
        out_shape=[jax.ShapeDtypeStruct((n, D_MODEL), F32), jax.ShapeDtypeStruct((n, D_MODEL), F32),
                   jax.ShapeDtypeStruct((n, LANES), F32), jax.ShapeDtypeStruct((1, LANES), F32)],
        scratch_shapes=[pltpu.VMEM((1, LANES), F32)],
        compiler_params=_params(("arbitrary",)),
        name="merge",
    )(x2d, mp, ysb, yx, g1, g2, wb1_bf, wb2_bf, wo_bf, norm2.reshape(1, -1), w_route_bf, b_route, cnt_in)


def _row_copy(src, src_row, dst, dst_row, sem):
    return pltpu.make_async_copy(src.at[pl.ds(src_row, 1)], dst.at[pl.ds(dst_row, 1)], sem)


def _dispatch_kernel(tm, dest_ref, h_ref, buf_in_ref, buf_ref, sem):
    del buf_in_ref

    def issue(t, carry):
        _row_copy(h_ref, t, buf_ref, dest_ref[0, 0, 2 * t], sem).start()
        _row_copy(h_ref, t, buf_ref, dest_ref[0, 0, 2 * t + 1], sem).start()
        return carry

    lax.fori_loop(0, tm, issue, 0, unroll=ROW_DMA_UNROLL)
    for _ in range(2):
        pltpu.make_async_copy(h_ref, buf_ref.at[pl.ds(0, tm)], sem).wait()


def dispatch(h, dest, buf_zero, tm):
    n = h.shape[0]
    nt = n // tm
    return pl.pallas_call(
        functools.partial(_dispatch_kernel, tm), grid=(nt,),
        in_specs=[pl.BlockSpec((1, 1, 2 * tm), lambda i: (i, 0, 0), memory_space=pltpu.SMEM),
                  pl.BlockSpec((tm, D_MODEL), lambda i: (i, 0)),
                  pl.BlockSpec(memory_space=pl.ANY)],
        out_specs=pl.BlockSpec(memory_space=pl.ANY),
        out_shape=jax.ShapeDtypeStruct(buf_zero.shape, buf_zero.dtype),
        scratch_shapes=[pltpu.SemaphoreType.DMA(())],
        input_output_aliases={2: 0},
        compiler_params=_params(("arbitrary",)),
        name="dispatch",
    )(dest.reshape(nt, 1, 2 * tm), h, buf_zero)


def _experts_kernel(be_ref, nu_ref, x_ref, w1_ref, w3_ref, w2_ref, y_ref):
    del be_ref

    @pl.when(pl.program_id(0) < nu_ref[0])
    def _():
        xb = x_ref[...].astype(BF16)
        h1 = _dot(xb, w1_ref[...].astype(BF16))
        h3 = _dot(xb, w3_ref[...].astype(BF16))
        act = (h1 * jax.nn.sigmoid(h1)) * h3
        y_ref[...] = _dot(act.astype(BF16), w2_ref[...].astype(BF16))

    @pl.when(pl.program_id(0) >= nu_ref[0])
    def _():
        y_ref[...] = jnp.zeros_like(y_ref)


def experts(buf, block_e, n_used, w1, w3, w2):
    nblk = buf.shape[0] // MOE_BM
    clamp = lambda i, nu: jnp.minimum(i, nu[0] - 1)
    grid_spec = pltpu.PrefetchScalarGridSpec(
        num_scalar_prefetch=2,
        grid=(nblk,),
        in_specs=[pl.BlockSpec((MOE_BM, D_MODEL), lambda i, be, nu: (clamp(i, nu), 0)),
                  pl.BlockSpec((None, D_MODEL, D_EXPERT), lambda i, be, nu: (be[i], 0, 0)),
                  pl.BlockSpec((None, D_MODEL, D_EXPERT), lambda i, be, nu: (be[i], 0, 0)),
                  pl.BlockSpec((None, D_EXPERT, D_MODEL), lambda i, be, nu: (be[i], 0, 0))],
        out_specs=pl.BlockSpec((MOE_BM, D_MODEL), lambda i, be, nu: (i, 0)),
    )
    return pl.pallas_call(
        _experts_kernel, grid_spec=grid_spec,
        out_shape=jax.ShapeDtypeStruct(buf.shape, F32),
        compiler_params=_params(("arbitrary",)),
        name="experts",
    )(block_e, n_used, buf, w1, w3, w2)


def _combine_kernel(tm, dest_ref, x1_ref, route_ref, ybuf_ref, o_ref, rows_ref, sem):
    def issue(t, carry):
        _row_copy(ybuf_ref, dest_ref[0, 0, 2 * t], rows_ref.at[0], t, sem).start()
        _row_copy(ybuf_ref, dest_ref[0, 0, 2 * t + 1], rows_ref.at[1], t, sem).start()
        return carry

    lax.fori_loop(0, tm, issue, 0, unroll=ROW_DMA_UNROLL)
    for k in range(2):
        pltpu.make_async_copy(ybuf_ref.at[pl.ds(0, tm)], rows_ref.at[k], sem).wait()
    route = route_ref[...]
    o_ref[...] = (x1_ref[...] + route[:, _R_G1:_R_G1 + 1] * rows_ref[0]
                  + route[:, _R_G2:_R_G2 + 1] * rows_ref[1])


def combine(x1, route, dest, ybuf, tm):
    n = x1.shape[0]
    nt = n // tm
    return pl.pallas_call(
        functools.partial(_combine_kernel, tm), grid=(nt,),
        in_specs=[pl.BlockSpec((1, 1, 2 * tm), lambda i: (i, 0, 0), memory_space=pltpu.SMEM),
                  pl.BlockSpec((tm, D_MODEL), lambda i: (i, 0)),
                  pl.BlockSpec((tm, LANES), lambda i: (i, 0)),
                  pl.BlockSpec(memory_space=pl.ANY)],
        out_specs=pl.BlockSpec((tm, D_MODEL), lambda i: (i, 0)),
        out_shape=jax.ShapeDtypeStruct((n, D_MODEL), F32),
        scratch_shapes=[pltpu.VMEM((2, tm, D_MODEL), F32), pltpu.SemaphoreType.DMA(())],
        compiler_params=_params(("arbitrary",)),
        name="combine",
    )(dest.reshape(nt, 1, 2 * tm), x1, route, ybuf)


def kernel(x_prompt, x_sample, mem_prompt, cache_k, cache_v, cache_mem_k, cache_mem_v, state_conv, page_table, norm1, w_in, conv_w, sb_bias, mem_norm, w_mem_kv, q_norm, k_norm, w_branch, w_o, norm2, w_group, b_group, w_router, b_router, w1, w3, w2):
    nb, seq, _ = x_prompt.shape
    ns, dec_seq, _ = x_sample.shape
    assert dec_seq == 1 and seq % SB_TQ == 0 and SB_TQ == SB_TK
    n_mem = mem_prompt.shape[1]
    sbw = SB_HEADS * SB_HEAD_DIM
    xw = X_HEADS * X_HEAD_DIM
    past = page_table.shape[1] * PAGE_SIZE

    assert w_in.shape[1] == _W_MAIN and _C_QX == _C_K + 2 * sbw and _C_GL == _C_QX + xw
    w_main = w_in.astype(BF16)
    w_kvt = w_in[:, _C_K:_C_QX].T.astype(BF16)
    wb = w_branch.astype(BF16)
    wo_bf = w_o.astype(BF16)
    w_route = jnp.zeros((D_MODEL, LANES), F32)
    w_route = w_route.at[:, :N_EXPERTS].set(w_router).at[:, N_EXPERTS:N_EXPERTS + N_GROUPS].set(w_group)
    b_route = jnp.zeros((1, LANES), F32)
    b_route = b_route.at[0, :N_EXPERTS].set(b_router).at[0, N_EXPERTS:N_EXPERTS + N_GROUPS].set(b_group)
    w_route = w_route.astype(BF16)

    xp = x_prompt.reshape(nb * seq, D_MODEL)
    mk_p, mv_p = memkv(mem_prompt.reshape(nb * n_mem, D_MODEL), mem_norm, w_mem_kv.astype(BF16), k_norm, n_mem)
    (kt_p, vt_p, q_p, qx_p, g1_p, g2_p, mp_p, tail_p, ktb_p, vtb_p) = inproj(
        xp, norm1, w_main, w_kvt, wb[0], conv_w, q_norm, seq, 512)
    ysb_p = sb_prompt(q_p, ktb_p, vtb_p, sb_bias, seq)
    yx_p = xattn_prompt(qx_p, mk_p.astype(BF16), mv_p.astype(BF16), seq, n_mem, 512)

    xs = x_sample.reshape(ns, D_MODEL)
    (kt_s, vt_s, q_s, qx_s, g1_s, g2_s, mp_s, u_s) = inproj(
        xs, norm1, w_main, w_kvt, wb[0], conv_w, q_norm, ns, ns,
        conv_state=(state_conv[:, 0], state_conv[:, 1]))
    ck_t = cache_k.transpose(0, 2, 3, 1).reshape(cache_k.shape[0], sbw, PAGE_SIZE)
    cv_t = cache_v.transpose(0, 2, 3, 1).reshape(cache_v.shape[0], sbw, PAGE_SIZE)
    ysb_s = sb_sample(q_s.astype(F32), kt_s[0], vt_s[0], ck_t, cv_t, page_table, sb_bias,
                      new_key_visible=(past < past))
    yx_s = xattn_sample(qx_s, cache_mem_k, cache_mem_v)

    cnt0 = jnp.zeros((1, LANES), F32)
    x1_p, h_p, route_p, cnt1 = merge(xp, mp_p, ysb_p, yx_p, g1_p, g2_p, wb[1], wb[2], wo_bf, norm2,
                                     w_route, b_route, cnt0, 512)
    x1_s, h_s, route_s, cnt2 = merge(xs, mp_s, ysb_s, yx_s, g1_s, g2_s, wb[1], wb[2], wo_bf, norm2,
                                     w_route, b_route, cnt1, ns)

    n_slots = 2 * (nb * seq + ns)
    nblk = -(-(n_slots + N_EXPERTS * (MOE_BM - 1)) // MOE_BM)
    counts = cnt2[0, :N_EXPERTS].astype(I32)
    padded = (counts + MOE_BM - 1) // MOE_BM * MOE_BM
    ends = jnp.cumsum(padded)
    starts = ends - padded
    n_used = (ends[-1] // MOE_BM).astype(I32).reshape(1)
    blk_start = jnp.arange(nblk, dtype=I32) * MOE_BM
    block_e = jnp.sum(blk_start[:, None] >= ends[None, :], axis=1).astype(I32)
    block_e = jnp.minimum(block_e, block_e[jnp.maximum(n_used[0] - 1, 0)])

    def dests(route):
        e = route[:, _R_E1:_R_E2 + 1].astype(I32)
        rk = route[:, _R_R1:_R_R2 + 1].astype(I32)
        return (starts[e] + rk).reshape(-1)

    dest_p, dest_s = dests(route_p), dests(route_s)
    buf = jnp.zeros((nblk * MOE_BM, D_MODEL), F32)
    buf = dispatch(h_p, dest_p, buf, 512)
    buf = dispatch(h_s, dest_s, buf, ns)
    ybuf = experts(buf, block_e, n_used, w1, w3, w2)
    y_p = combine(x1_p, route_p, dest_p, ybuf, 256)
    y_s = combine(x1_s, route_s, dest_s, ybuf, ns)

    def heads_last(t, b, s):
        return t.reshape(b, SB_HEADS, SB_HEAD_DIM, s).transpose(0, 3, 1, 2)

    conv_prompt = tail_p[:, SUBLANES - 2:, :]
    conv_sample = jnp.stack([state_conv[:, 1], u_s[0]], axis=1)
    return (y_p.reshape(nb, seq, D_MODEL), y_s.reshape(ns, 1, D_MODEL),
            heads_last(kt_p, nb, seq), heads_last(vt_p, nb, seq), conv_prompt,
            mk_p.reshape(nb, n_mem, X_HEADS, X_HEAD_DIM), mv_p.reshape(nb, n_mem, X_HEADS, X_HEAD_DIM),
            heads_last(kt_s, 1, ns).reshape(ns, 1, SB_HEADS, SB_HEAD_DIM),
            heads_last(vt_s, 1, ns).reshape(ns, 1, SB_HEADS, SB_HEAD_DIM),
            conv_sample)
```

```python
import functools

import jax
import jax.numpy as jnp
from jax import lax
from jax.experimental import pallas as pl
from jax.experimental.pallas import tpu as pltpu

F32 = jnp.float32
BF16 = jnp.bfloat16
I32 = jnp.int32

EPS = 1e-6
D_MODEL = 1024
BRANCH_W = 512
SB_HEADS = 8
SB_HEAD_DIM = 64
X_HEADS = 4
X_HEAD_DIM = 128
N_MEM = 256
N_GROUPS = 4
EXPERTS_PER_GROUP = 8
N_EXPERTS = N_GROUPS * EXPERTS_PER_GROUP
D_EXPERT = 512
PAGE_SIZE = 128

LANES = 128
SUBLANES = 8
VMEM_LIMIT = 56 * 1024 * 1024

SB_TK = 256
SB_TQ = 4 * SB_TK
MOE_BM = 256
XS_TOKENS = 4
ROW_DMA_UNROLL = 8


def _dot(a, b):
    return jnp.dot(a, b, preferred_element_type=F32)


def _dot_nt(a, b):
    return lax.dot_general(a, b, (((1,), (1,)), ((), ())), preferred_element_type=F32)


def _rms(x, g):
    return x * lax.rsqrt(jnp.mean(x * x, axis=-1, keepdims=True) + EPS) * g


def _softplus(z):
    return jnp.maximum(z, 0.0) + jnp.log(1.0 + jnp.exp(-jnp.abs(z)))


def _params(sem, vmem=VMEM_LIMIT):
    return pltpu.CompilerParams(dimension_semantics=sem, vmem_limit_bytes=vmem)


def _const_spec(shape):
    nd = len(shape)
    return pl.BlockSpec(shape, lambda *_: (0,) * nd, pipeline_mode=pl.Buffered(1))


def _memkv_kernel(mem_ref, g_ref, w_ref, kn_ref, mk_ref, mv_ref):
    xn = _rms(mem_ref[...], g_ref[...]).astype(BF16)
    kv = _dot(xn, w_ref[...])
    xw = X_HEADS * X_HEAD_DIM
    for h in range(X_HEADS):
        sl = slice(h * X_HEAD_DIM, (h + 1) * X_HEAD_DIM)
        mk_ref[:, sl] = _rms(kv[:, sl], kn_ref[...])
    mv_ref[...] = kv[:, xw:]


def memkv(mem2d, mem_norm, w_mem_kv_bf, k_norm, n_mem):
    n = mem2d.shape[0]
    xw = X_HEADS * X_HEAD_DIM
    return pl.pallas_call(
        _memkv_kernel,
        grid=(n // n_mem,),
        in_specs=[pl.BlockSpec((n_mem, D_MODEL), lambda i: (i, 0)),
                  _const_spec((1, D_MODEL)),
                  _const_spec((D_MODEL, 2 * xw)),
                  _const_spec((1, X_HEAD_DIM))],
        out_specs=[pl.BlockSpec((n_mem, xw), lambda i: (i, 0)),
                   pl.BlockSpec((n_mem, xw), lambda i: (i, 0))],
        out_shape=[jax.ShapeDtypeStruct((n, xw), F32), jax.ShapeDtypeStruct((n, xw), F32)],
        compiler_params=_params(("arbitrary",)),
        name="memkv",
    )(mem2d, mem_norm.reshape(1, -1), w_mem_kv_bf, k_norm.reshape(1, -1))


_C_UIN, _C_CG, _C_BG, _C_Q, _C_K, _C_QX, _C_GL = 0, 512, 1024, 1536, 2048, 3072, 3584
_W_MAIN = _C_GL + 3 * D_MODEL


def _inproj_kernel(carry_conv, tiles_per_seq, tm, *refs):
    if carry_conv:
        (x_ref, n1_ref, wm_ref, wkvt_ref, wb0_ref, cw_ref, qn_ref,
         kt_ref, vt_ref, q_ref, qx_ref, g1_ref, g2_ref, mp_ref, tail_ref, ktb_ref, vtb_ref,
         carry_ref) = refs
    else:
        (x_ref, n1_ref, wm_ref, wkvt_ref, wb0_ref, cw_ref, qn_ref, um2_ref, um1_ref,
         kt_ref, vt_ref, q_ref, qx_ref, g1_ref, g2_ref, mp_ref, tail_ref) = refs

    xn = _rms(x_ref[...], n1_ref[...]).astype(BF16)

    def proj(c0, width):
        return _dot(xn, wm_ref[:, c0:c0 + width])

    kvt = _dot_nt(wkvt_ref[...], xn)
    sbw = SB_HEADS * SB_HEAD_DIM
    kt_ref[...] = kvt[:sbw]
    vt_ref[...] = kvt[sbw:]
    if carry_conv:
        for c in range(tm // SB_TK):
            ktb_ref[c] = kvt[:sbw, c * SB_TK:(c + 1) * SB_TK].astype(BF16)
            vtb_ref[c] = kvt[sbw:, c * SB_TK:(c + 1) * SB_TK].astype(BF16)

    q_ref[...] = (proj(_C_Q, sbw) * (SB_HEAD_DIM ** -0.5)).astype(BF16)

    qx = proj(_C_QX, X_HEADS * X_HEAD_DIM)
    for h in range(X_HEADS):
        sl = slice(h * X_HEAD_DIM, (h + 1) * X_HEAD_DIM)
        qx_ref[:, sl] = _rms(qx[:, sl], qn_ref[...]).astype(BF16)

    u = proj(_C_CG, BRANCH_W) * proj(_C_UIN, BRANCH_W)
    if carry_conv:
        @pl.when(pl.program_id(0) % tiles_per_seq == 0)
        def _():
            carry_ref[...] = jnp.zeros_like(carry_ref)
        ext = jnp.concatenate([carry_ref[...], u], axis=0)
        um1 = pltpu.roll(ext, 1, 0)[SUBLANES:]
        um2 = pltpu.roll(ext, 2, 0)[SUBLANES:]
        carry_ref[...] = u[tm - SUBLANES:]
        tail_ref[...] = u[tm - SUBLANES:]
    else:
        um1 = um1_ref[...]
        um2 = um2_ref[...]
        tail_ref[...] = u
    cw = cw_ref[...]
    y_conv = proj(_C_BG, BRANCH_W) * (um2 * cw[0:1] + um1 * cw[1:2] + u * cw[2:3])

    g0 = jax.nn.sigmoid(proj(_C_GL, D_MODEL))
    mp_ref[...] = g0 * _dot(y_conv.astype(BF16), wb0_ref[...])
    g1_ref[...] = jax.nn.sigmoid(proj(_C_GL + D_MODEL, D_MODEL))
    g2_ref[...] = jax.nn.sigmoid(proj(_C_GL + 2 * D_MODEL, D_MODEL))


def inproj(x2d, norm1, w_main_bf, w_kvt_bf, wb0_bf, conv_w, q_norm, seq_len, tm, conv_state=None):
    n = x2d.shape[0]
    carry = conv_state is None
    nseq = n // seq_len if carry else 1
    slen = seq_len if carry else n
    tps = slen // tm
    sbw = SB_HEADS * SB_HEAD_DIM
    xw = X_HEADS * X_HEAD_DIM
    nkb = max(tm // SB_TK, 1)
    tail_rows = SUBLANES if carry else tm

    row = lambda w: pl.BlockSpec((tm, w), lambda i: (i, 0))
    kt_spec = pl.BlockSpec((None, sbw, tm), lambda i: (i // tps, 0, i % tps))
    ktb_spec = pl.BlockSpec((None, nkb, sbw, SB_TK), lambda i: (i // tps, i % tps, 0, 0))
    in_specs = [row(D_MODEL), _const_spec((1, D_MODEL)), _const_spec((D_MODEL, _W_MAIN)),
                _const_spec((2 * sbw, D_MODEL)), _const_spec((BRANCH_W, D_MODEL)),
                _const_spec((3, BRANCH_W)), _const_spec((1, X_HEAD_DIM))]
    args = [x2d, norm1.reshape(1, -1), w_main_bf, w_kvt_bf, wb0_bf, conv_w, q_norm.reshape(1, -1)]
    scratch = []
    if carry:
        scratch = [pltpu.VMEM((SUBLANES, BRANCH_W), F32)]
    else:
        in_specs += [row(BRANCH_W), row(BRANCH_W)]
        args += list(conv_state)
    out_specs = [kt_spec, kt_spec, row(sbw), row(xw),
                 row(D_MODEL), row(D_MODEL), row(D_MODEL),
                 pl.BlockSpec((None, tail_rows, BRANCH_W), lambda i: (i // tps, 0, 0))]
    out_shape = [jax.ShapeDtypeStruct((nseq, sbw, slen), F32), jax.ShapeDtypeStruct((nseq, sbw, slen), F32),
                 jax.ShapeDtypeStruct((n, sbw), BF16), jax.ShapeDtypeStruct((n, xw), BF16),
                 jax.ShapeDtypeStruct((n, D_MODEL), F32), jax.ShapeDtypeStruct((n, D_MODEL), F32),
                 jax.ShapeDtypeStruct((n, D_MODEL), F32),
                 jax.ShapeDtypeStruct((nseq, tail_rows, BRANCH_W), F32)]
    if carry:
        out_specs += [ktb_spec, ktb_spec]
        out_shape += [jax.ShapeDtypeStruct((nseq, slen // SB_TK, sbw, SB_TK), BF16)] * 2
    return pl.pallas_call(
        functools.partial(_inproj_kernel, carry, tps, tm),
        grid=(n // tm,),
        in_specs=in_specs, out_specs=out_specs, out_shape=out_shape,
        scratch_shapes=scratch,
        compiler_params=_params(("arbitrary",)),
        name="inproj_prompt" if carry else "inproj_sample",
    )(*args)


def _sbp_kernel(bias_ref, q_ref, kt_ref, vt_ref, o_ref, acc_ref, car_ref):
    pair = pl.program_id(1)
    qi = pl.program_id(2)
    tq, tk = SB_TQ, SB_TK
    lane = lax.broadcasted_iota(I32, (tq, LANES), 1)
    q = q_ref[...]
    zero = jnp.zeros_like(q)
    qh = (jnp.where(lane < SB_HEAD_DIM, q, zero), jnp.where(lane >= SB_HEAD_DIM, q, zero))
    r = lax.broadcasted_iota(I32, (tk, tk), 0)
    c = lax.broadcasted_iota(I32, (tk, tk), 1)
    tri = (r >= c).astype(BF16)
    qrow = lax.broadcasted_iota(I32, (tq, tk), 0)
    kcol = lax.broadcasted_iota(I32, (tq, tk), 1)
    acc_ref[...] = jnp.zeros_like(acc_ref)
    car_ref[...] = jnp.zeros_like(car_ref)

    def block(j, diag_sub):
        kt = kt_ref[j]
        vt = vt_ref[j]
        rs = slice(0, tq)
        if diag_sub is not None:
            rs = slice(diag_sub * tk, tq)
            causal = (kcol + diag_sub * tk < qrow)[rs]
        for h in range(2):
            z = _dot(qh[h][rs], kt) + bias_ref[2 * pair + h]
            sp = _softplus(z)
            if diag_sub is not None:
                sp = jnp.where(causal, sp, 0.0)
            car = car_ref[h, rs]
            cum = _dot(sp.astype(BF16), tri)
            a = jnp.exp(z - (cum + jnp.concatenate([car] * (tk // LANES), axis=1)))
            if diag_sub is not None:
                a = jnp.where(causal, a, 0.0)
            acc_ref[h, rs] += _dot_nt(a.astype(BF16), vt)
            car_ref[h, rs] = car + jnp.broadcast_to(cum[:, 0:1], car.shape)

    nsub = tq // tk
    for s in reversed(range(nsub)):
        block(qi * nsub + s, s)

    def body(t, carry):
        block(qi * nsub - 1 - 2 * t, None)
        block(qi * nsub - 2 - 2 * t, None)
        return carry

    lax.fori_loop(0, qi * (nsub // 2), body, 0)
    o_ref[...] = jnp.where(lane < SB_HEAD_DIM, acc_ref[0], acc_ref[1]).astype(o_ref.dtype)


def sb_prompt(q_bf, ktb, vtb, sb_bias, seq_len):
    n = q_bf.shape[0]
    nseq = n // seq_len
    nq = seq_len // SB_TQ
    nk = seq_len // SB_TK
    npair = SB_HEADS // 2
    grid_spec = pltpu.PrefetchScalarGridSpec(
        num_scalar_prefetch=0,
        grid=(nseq, npair, nq),
        in_specs=[pl.BlockSpec(memory_space=pltpu.SMEM),
                  pl.BlockSpec((SB_TQ, LANES), lambda b, p, i: (b * nq + i, p)),
                  pl.BlockSpec((None, nk, LANES, SB_TK), lambda b, p, i: (b, 0, p, 0)),
                  pl.BlockSpec((None, nk, LANES, SB_TK), lambda b, p, i: (b, 0, p, 0))],
        out_specs=pl.BlockSpec((SB_TQ, LANES), lambda b, p, i: (b * nq + i, p)),
        scratch_shapes=[pltpu.VMEM((2, SB_TQ, LANES), F32), pltpu.VMEM((2, SB_TQ, LANES), F32)],
    )
    return pl.pallas_call(
        _sbp_kernel, grid_spec=grid_spec,
        out_shape=jax.ShapeDtypeStruct((n, SB_HEADS * SB_HEAD_DIM), BF16),
        compiler_params=_params(("arbitrary", "arbitrary", "arbitrary")),
        name="sb_prompt",
    )(sb_bias.astype(F32), q_bf, ktb, vtb)


def _xp_kernel(qx_ref, mk_ref, mv_ref, o_ref):
    for h in range(X_HEADS):
        sl = slice(h * X_HEAD_DIM, (h + 1) * X_HEAD_DIM)
        s = _dot_nt(qx_ref[:, sl], mk_ref[:, sl]) * (X_HEAD_DIM ** -0.5)
        e = jnp.exp(s - jnp.max(s, axis=-1, keepdims=True))
        p = e / jnp.sum(e, axis=-1, keepdims=True)
        o_ref[:, sl] = _dot(p.astype(BF16), mv_ref[:, sl]).astype(o_ref.dtype)


def xattn_prompt(qx_bf, mk_bf, mv_bf, seq_len, n_mem, tm):
    n = qx_bf.shape[0]
    tps = seq_len // tm
    xw = X_HEADS * X_HEAD_DIM
    return pl.pallas_call(
        _xp_kernel, grid=(n // tm,),
        in_specs=[pl.BlockSpec((tm, xw), lambda i: (i, 0)),
                  pl.BlockSpec((n_mem, xw), lambda i: (i // tps, 0)),
                  pl.BlockSpec((n_mem, xw), lambda i: (i // tps, 0))],
        out_specs=pl.BlockSpec((tm, xw), lambda i: (i, 0)),
        out_shape=jax.ShapeDtypeStruct((n, xw), BF16),
        compiler_params=_params(("arbitrary",)),
        name="xattn_prompt",
    )(qx_bf, mk_bf, mv_bf)


def _split_bf16(x):
    hi = x.astype(BF16)
    return hi, (x - hi.astype(F32)).astype(BF16)


def _sbs_kernel(new_key_visible, npages, pt_ref, bias_ref, qbd_ref, kn_ref, vn_ref, *refs):
    del pt_ref
    k_refs, v_refs, o_ref = refs[:npages], refs[npages:2 * npages], refs[2 * npages]
    b = pl.program_id(0)
    psz = PAGE_SIZE
    sbw = SB_HEADS * SB_HEAD_DIM
    rows = npages * SB_HEADS
    head_bits = SB_HEADS.bit_length() - 1
    qbd = qbd_ref[0]
    bias = bias_ref[...]

    mine = lax.broadcasted_iota(I32, (SB_HEADS, kn_ref.shape[1]), 1) == b
    zs = _dot(qbd, kn_ref[...].astype(BF16)) + bias[:, 0:1]
    seen = jnp.logical_and(mine, new_key_visible)
    sp0 = jnp.where(seen, _softplus(zs), 0.0)
    a0 = jnp.where(seen, jnp.exp(zs - sp0), 0.0)
    sp0_tot = jnp.sum(sp0, axis=-1, keepdims=True)
    a0_rows = jnp.concatenate(
        [jnp.broadcast_to(a0[h:h + 1], (SB_HEAD_DIM, a0.shape[1])) for h in range(SB_HEADS)], axis=0)
    new = jnp.sum(a0_rows * vn_ref[...], axis=-1, keepdims=True)

    z = jnp.concatenate([_dot(qbd, k_refs[j][...].astype(BF16)) + bias for j in range(npages)], axis=0)
    sp = _softplus(z)
    r = lax.broadcasted_iota(I32, (psz, psz), 0)
    c = lax.broadcasted_iota(I32, (psz, psz), 1)
    tri = (r >= c).astype(BF16)
    hi, lo = _split_bf16(sp)
    cum = _dot(hi, tri) + _dot(lo, tri)
    tot = jnp.broadcast_to(cum[:, 0:1], (rows, psz))
    rr = lax.broadcasted_iota(I32, (rows, rows), 0)
    cc = lax.broadcasted_iota(I32, (rows, rows), 1)
    later = jnp.logical_and((cc >> head_bits) > (rr >> head_bits),
                            (cc & (SB_HEADS - 1)) == (rr & (SB_HEADS - 1))).astype(BF16)
    th, tl = _split_bf16(tot)
    carry = _dot(later, th) + _dot(later, tl) + jnp.concatenate([sp0_tot] * npages, axis=0)
    a = jnp.exp(z - cum - carry)

    accs = []
    for h in range(SB_HEADS):
        hs = slice(h * SB_HEAD_DIM, (h + 1) * SB_HEAD_DIM)
        acc = jnp.zeros((SB_HEAD_DIM, psz), F32)
        for j in range(npages):
            rj = j * SB_HEADS + h
            acc = acc + jnp.broadcast_to(a[rj:rj + 1], (SB_HEAD_DIM, psz)) * v_refs[j][hs, :]
        accs.append(acc)
    acc = jnp.concatenate(accs, axis=0)
    lane0 = lax.broadcasted_iota(I32, (sbw, psz), 1) == 0
    acc = acc + jnp.where(lane0, new, 0.0)
    hi2, lo2 = _split_bf16(acc)
    ones = jnp.ones((SUBLANES, psz), BF16)
    y = _dot_nt(ones, hi2) + _dot_nt(ones, lo2)
    o_ref[pl.ds(b, 1), :] = y[0:1]


def sb_sample(q_s, kn_t, vn_t, cache_k_t, cache_v_t, page_table, sb_bias, new_key_visible):
    nb, npages = page_table.shape
    sbw = SB_HEADS * SB_HEAD_DIM
    head_of_col = jnp.arange(sbw) // SB_HEAD_DIM
    sel = head_of_col[None, :] == jnp.arange(SB_HEADS)[:, None]
    qbd = jnp.where(sel[None], q_s[:, None, :], 0.0).astype(BF16)
    assert PAGE_SIZE == LANES
    bias = jnp.broadcast_to(sb_bias.astype(F32)[:, None], (SB_HEADS, LANES))
    page = lambda j: pl.BlockSpec((None, sbw, PAGE_SIZE), lambda b, pt: (pt[b, j], 0, 0))
    grid_spec = pltpu.PrefetchScalarGridSpec(
        num_scalar_prefetch=1,
        grid=(nb,),
        in_specs=[pl.BlockSpec((SB_HEADS, LANES), lambda b, pt: (0, 0)),
                  pl.BlockSpec((1, SB_HEADS, sbw), lambda b, pt: (b, 0, 0)),
                  pl.BlockSpec((sbw, nb), lambda b, pt: (0, 0)),
                  pl.BlockSpec((sbw, nb), lambda b, pt: (0, 0))]
                 + [page(j) for j in range(npages)] * 2,
        out_specs=pl.BlockSpec((nb, sbw), lambda b, pt: (0, 0)),
    )
    return pl.pallas_call(
        functools.partial(_sbs_kernel, new_key_visible, npages), grid_spec=grid_spec,
        out_shape=jax.ShapeDtypeStruct((nb, sbw), F32),
        compiler_params=_params(("arbitrary",)),
        name="sb_sample",
    )(page_table, bias, qbd, kn_t, vn_t, *([cache_k_t] * npages), *([cache_v_t] * npages))


def _xs_kernel(qx_ref, mk_ref, mv_ref, o_ref):
    step = pl.program_id(0)
    rows = N_MEM * X_HEADS
    own = ((lax.broadcasted_iota(I32, (SUBLANES, rows), 1) & (X_HEADS - 1))
           == lax.broadcasted_iota(I32, (SUBLANES, rows), 0))
    for t in range(XS_TOKENS):
        mk = mk_ref[t].astype(BF16)
        mv = mv_ref[t].astype(BF16)
        s = _dot_nt(qx_ref[t], mk) * (X_HEAD_DIM ** -0.5)
        s = jnp.where(own, s, -jnp.inf)
        m = jnp.max(s, axis=-1, keepdims=True)
        m = jnp.where(m == -jnp.inf, 0.0, m)
        e = jnp.exp(s - m)
        den = jnp.sum(e, axis=-1, keepdims=True)
        p = e / jnp.where(den == 0.0, 1.0, den)
        y = _dot(p.astype(BF16), mv)
        o_ref[pl.ds(step * XS_TOKENS + t, 1)] = y[None, 0:X_HEADS]


def xattn_sample(qx_bf, cache_mem_k, cache_mem_v):
    nb = qx_bf.shape[0]
    rows = N_MEM * X_HEADS
    mk = cache_mem_k.reshape(nb, rows, X_HEAD_DIM)
    mv = cache_mem_v.reshape(nb, rows, X_HEAD_DIM)
    qx = qx_bf.reshape(nb, X_HEADS, X_HEAD_DIM)
    qx = jnp.concatenate([qx, jnp.zeros((nb, SUBLANES - X_HEADS, X_HEAD_DIM), BF16)], axis=1)
    assert nb % XS_TOKENS == 0
    out = pl.pallas_call(
        _xs_kernel, grid=(nb // XS_TOKENS,),
        in_specs=[pl.BlockSpec((XS_TOKENS, SUBLANES, X_HEAD_DIM), lambda b: (b, 0, 0)),
                  pl.BlockSpec((XS_TOKENS, rows, X_HEAD_DIM), lambda b: (b, 0, 0)),
                  pl.BlockSpec((XS_TOKENS, rows, X_HEAD_DIM), lambda b: (b, 0, 0))],
        out_specs=pl.BlockSpec((nb, X_HEADS, X_HEAD_DIM), lambda b: (0, 0, 0)),
        out_shape=jax.ShapeDtypeStruct((nb, X_HEADS, X_HEAD_DIM), F32),
        compiler_params=_params(("arbitrary",)),
        name="xattn_sample",
    )(qx, mk, mv)
    return out.reshape(nb, X_HEADS * X_HEAD_DIM)


_R_E1, _R_E2, _R_R1, _R_R2, _R_G1, _R_G2 = 0, 1, 2, 3, 4, 5
_GROUP_LANE0 = N_EXPERTS


def _merge_kernel(tm, x_ref, mp_ref, ysb_ref, yx_ref, g1_ref, g2_ref, wb1_ref, wb2_ref, wo_ref,
                  n2_ref, wr_ref, br_ref, cin_ref, x1_ref, h_ref, route_ref, route_t_ref, cout_ref, cnt_ref):
    @pl.when(pl.program_id(0) == 0)
    def _():
        cnt_ref[...] = cin_ref[...]

    m = (mp_ref[...] + g1_ref[...] * _dot(ysb_ref[...].astype(BF16), wb1_ref[...])
         + g2_ref[...] * _dot(yx_ref[...].astype(BF16), wb2_ref[...]))
    x1 = x_ref[...] + _dot(m.astype(BF16), wo_ref[...])
    x1_ref[...] = x1
    hn = _rms(x1, n2_ref[...])
    h_ref[...] = hn
    logits = _dot(hn.astype(BF16), wr_ref[...]) + br_ref[...]

    lane = lax.broadcasted_iota(I32, (tm, LANES), 1).astype(F32)
    big = float(LANES)
    ninf = -jnp.inf

    def first_argmax(v):
        mx = jnp.max(v, axis=-1, keepdims=True)
        idx = jnp.min(jnp.where(v == mx, lane, big), axis=-1, keepdims=True)
        return mx, idx

    gl = jnp.where((lane >= _GROUP_LANE0) & (lane < _GROUP_LANE0 + N_GROUPS), logits, ninf)
    gmax, glane = first_argmax(gl)
    gprob = 1.0 / jnp.sum(jnp.exp(gl - gmax), axis=-1, keepdims=True)
    e0 = (glane - _GROUP_LANE0) * EXPERTS_PER_GROUP
    el = jnp.where((lane >= e0) & (lane < e0 + EXPERTS_PER_GROUP), logits, ninf)
    m1, i1 = first_argmax(el)
    m2, i2 = first_argmax(jnp.where(lane == i1, ninf, el))
    e2 = jnp.exp(m2 - m1)
    gate1 = gprob / (1.0 + e2)
    gate2 = gprob * e2 / (1.0 + e2)

    oh1 = (lane == i1).astype(F32)
    oh2 = (lane == i2).astype(F32)
    cnt = oh1 + oh2
    r = lax.broadcasted_iota(I32, (tm, tm), 0)
    c = lax.broadcasted_iota(I32, (tm, tm), 1)
    before = _dot((c < r).astype(BF16), cnt.astype(BF16)) + cnt_ref[...]
    rank1 = jnp.sum(oh1 * before, axis=-1, keepdims=True)
    rank2 = jnp.sum(oh2 * before, axis=-1, keepdims=True)
    cnt_ref[...] = cnt_ref[...] + jnp.sum(cnt, axis=0, keepdims=True)
    cout_ref[...] = cnt_ref[...]

    rec = jnp.zeros((tm, LANES), F32)
    for ln, val in ((_R_E1, i1.astype(F32)), (_R_E2, i2.astype(F32)), (_R_R1, rank1), (_R_R2, rank2),
                    (_R_G1, gate1), (_R_G2, gate2)):
        rec = jnp.where(lane == ln, val, rec)
    route_ref[...] = rec
    route_t_ref[...] = rec.T[0:SUBLANES]


def merge(x2d, mp, ysb, yx, g1, g2, wb1_bf, wb2_bf, wo_bf, norm2, w_route_bf, b_route, cnt_in, tm):
    n = x2d.shape[0]
    row = lambda w: pl.BlockSpec((tm, w), lambda i: (i, 0))
    return pl.pallas_call(
        functools.partial(_merge_kernel, tm), grid=(n // tm,),
        in_specs=[row(D_MODEL), row(D_MODEL), row(BRANCH_W), row(BRANCH_W), row(D_MODEL), row(D_MODEL),
                  _const_spec((BRANCH_W, D_MODEL)), _const_spec((BRANCH_W, D_MODEL)),
                  _const_spec((D_MODEL, D_MODEL)), _const_spec((1, D_MODEL)),
                  _const_spec((D_MODEL, LANES)), _const_spec((1, LANES)), _const_spec((1, LANES))],
        out_specs=[row(D_MODEL), row(D_MODEL), row(LANES), pl.BlockSpec((SUBLANES, tm), lambda i: (0, i)),
                   pl.BlockSpec((1, LANES), lambda i: (0, 0))],
        out_shape=[jax.ShapeDtypeStruct((n, D_MODEL), F32), jax.ShapeDtypeStruct((n, D_MODEL), F32),
                   jax.ShapeDtypeStruct((n, LANES), F32), jax.ShapeDtypeStruct((SUBLANES, n), F32),
                   jax.ShapeDtypeStruct((1, LANES), F32)],
        scratch_shapes=[pltpu.VMEM((1, LANES), F32)],
        compiler_params=_params(("arbitrary",)),
        name="merge",
    )(x2d, mp, ysb, yx, g1, g2, wb1_bf, wb2_bf, wo_bf, norm2.reshape(1, -1), w_route_bf, b_route, cnt_in)


def _row_copy(src, src_row, dst, dst_row, sem):
    return pltpu.make_async_copy(src.at[pl.ds(src_row, 1)], dst.at[pl.ds(dst_row, 1)], sem)


def _dispatch_kernel(tm, dest_ref, h_ref, buf_in_ref, buf_ref, sem):
    del buf_in_ref

    def issue(t, carry):
        _row_copy(h_ref, t, buf_ref, dest_ref[0, 0, t], sem).start()
        _row_copy(h_ref, t, buf_ref, dest_ref[0, 1, t], sem).start()
        return carry

    lax.fori_loop(0, tm, issue, 0, unroll=ROW_DMA_UNROLL)
    for _ in range(2):
        pltpu.make_async_copy(h_ref, buf_ref.at[pl.ds(0, tm)], sem).wait()


def _slot_tiles(dest, tm):
    return dest.reshape(2, dest.shape[1] // tm, tm).transpose(1, 0, 2)


def dispatch(h, dest, buf_zero, tm):
    n = h.shape[0]
    nt = n // tm
    return pl.pallas_call(
        functools.partial(_dispatch_kernel, tm), grid=(nt,),
        in_specs=[pl.BlockSpec((1, 2, tm), lambda i: (i, 0, 0), memory_space=pltpu.SMEM),
                  pl.BlockSpec((tm, D_MODEL), lambda i: (i, 0)),
                  pl.BlockSpec(memory_space=pl.ANY)],
        out_specs=pl.BlockSpec(memory_space=pl.ANY),
        out_shape=jax.ShapeDtypeStruct(buf_zero.shape, buf_zero.dtype),
        scratch_shapes=[pltpu.SemaphoreType.DMA(())],
        input_output_aliases={2: 0},
        compiler_params=_params(("arbitrary",)),
        name="dispatch",
    )(_slot_tiles(dest, tm), h, buf_zero)


def _experts_kernel(be_ref, nu_ref, x_ref, w1_ref, w3_ref, w2_ref, y_ref):
    del be_ref

    @pl.when(pl.program_id(0) < nu_ref[0])
    def _():
        xb = x_ref[...].astype(BF16)
        h1 = _dot(xb, w1_ref[...].astype(BF16))
        h3 = _dot(xb, w3_ref[...].astype(BF16))
        act = (h1 * jax.nn.sigmoid(h1)) * h3
        y_ref[...] = _dot(act.astype(BF16), w2_ref[...].astype(BF16))

    @pl.when(pl.program_id(0) >= nu_ref[0])
    def _():
        y_ref[...] = jnp.zeros_like(y_ref)


def experts(buf, block_e, n_used, w1, w3, w2):
    nblk = buf.shape[0] // MOE_BM
    clamp = lambda i, nu: jnp.minimum(i, nu[0] - 1)
    grid_spec = pltpu.PrefetchScalarGridSpec(
        num_scalar_prefetch=2,
        grid=(nblk,),
        in_specs=[pl.BlockSpec((MOE_BM, D_MODEL), lambda i, be, nu: (clamp(i, nu), 0)),
                  pl.BlockSpec((None, D_MODEL, D_EXPERT), lambda i, be, nu: (be[i], 0, 0)),
                  pl.BlockSpec((None, D_MODEL, D_EXPERT), lambda i, be, nu: (be[i], 0, 0)),
                  pl.BlockSpec((None, D_EXPERT, D_MODEL), lambda i, be, nu: (be[i], 0, 0))],
        out_specs=pl.BlockSpec((MOE_BM, D_MODEL), lambda i, be, nu: (i, 0)),
    )
    return pl.pallas_call(
        _experts_kernel, grid_spec=grid_spec,
        out_shape=jax.ShapeDtypeStruct(buf.shape, F32),
        compiler_params=_params(("arbitrary",)),
        name="experts",
    )(block_e, n_used, buf, w1, w3, w2)


def _combine_kernel(tm, dest_ref, x1_ref, route_ref, ybuf_ref, o_ref, rows_ref, sem):
    def issue(t, carry):
        _row_copy(ybuf_ref, dest_ref[0, 0, t], rows_ref.at[0], t, sem).start()
        _row_copy(ybuf_ref, dest_ref[0, 1, t], rows_ref.at[1], t, sem).start()
        return carry

    lax.fori_loop(0, tm, issue, 0, unroll=ROW_DMA_UNROLL)
    for k in range(2):
        pltpu.make_async_copy(ybuf_ref.at[pl.ds(0, tm)], rows_ref.at[k], sem).wait()
    route = route_ref[...]
    o_ref[...] = (x1_ref[...] + route[:, _R_G1:_R_G1 + 1] * rows_ref[0]
                  + route[:, _R_G2:_R_G2 + 1] * rows_ref[1])


def combine(x1, route, dest, ybuf, tm):
    n = x1.shape[0]
    nt = n // tm
    return pl.pallas_call(
        functools.partial(_combine_kernel, tm), grid=(nt,),
        in_specs=[pl.BlockSpec((1, 2, tm), lambda i: (i, 0, 0), memory_space=pltpu.SMEM),
                  pl.BlockSpec((tm, D_MODEL), lambda i: (i, 0)),
                  pl.BlockSpec((tm, LANES), lambda i: (i, 0)),
                  pl.BlockSpec(memory_space=pl.ANY)],
        out_specs=pl.BlockSpec((tm, D_MODEL), lambda i: (i, 0)),
        out_shape=jax.ShapeDtypeStruct((n, D_MODEL), F32),
        scratch_shapes=[pltpu.VMEM((2, tm, D_MODEL), F32), pltpu.SemaphoreType.DMA(())],
        compiler_params=_params(("arbitrary",)),
        name="combine",
    )(_slot_tiles(dest, tm), x1, route, ybuf)


def kernel(x_prompt, x_sample, mem_prompt, cache_k, cache_v, cache_mem_k, cache_mem_v, state_conv, page_table, norm1, w_in, conv_w, sb_bias, mem_norm, w_mem_kv, q_norm, k_norm, w_branch, w_o, norm2, w_group, b_group, w_router, b_router, w1, w3, w2):
    nb, seq, _ = x_prompt.shape
    ns, dec_seq, _ = x_sample.shape
    assert dec_seq == 1 and seq % SB_TQ == 0 and SB_TQ % (2 * SB_TK) == 0
    n_mem = mem_prompt.shape[1]
    sbw = SB_HEADS * SB_HEAD_DIM
    xw = X_HEADS * X_HEAD_DIM
    past = page_table.shape[1] * PAGE_SIZE

    assert w_in.shape[1] == _W_MAIN and _C_QX == _C_K + 2 * sbw and _C_GL == _C_QX + xw
    w_main = w_in.astype(BF16)
    w_kvt = w_in[:, _C_K:_C_QX].T.astype(BF16)
    wb = w_branch.astype(BF16)
    wo_bf = w_o.astype(BF16)
    w_route = jnp.zeros((D_MODEL, LANES), F32)
    w_route = w_route.at[:, :N_EXPERTS].set(w_router).at[:, N_EXPERTS:N_EXPERTS + N_GROUPS].set(w_group)
    b_route = jnp.zeros((1, LANES), F32)
    b_route = b_route.at[0, :N_EXPERTS].set(b_router).at[0, N_EXPERTS:N_EXPERTS + N_GROUPS].set(b_group)
    w_route = w_route.astype(BF16)

    xp = x_prompt.reshape(nb * seq, D_MODEL)
    mk_p, mv_p = memkv(mem_prompt.reshape(nb * n_mem, D_MODEL), mem_norm, w_mem_kv.astype(BF16), k_norm, n_mem)
    (kt_p, vt_p, q_p, qx_p, g1_p, g2_p, mp_p, tail_p, ktb_p, vtb_p) = inproj(
        xp, norm1, w_main, w_kvt, wb[0], conv_w, q_norm, seq, 512)
    ysb_p = sb_prompt(q_p, ktb_p, vtb_p, sb_bias, seq)
    yx_p = xattn_prompt(qx_p, mk_p.astype(BF16), mv_p.astype(BF16), seq, n_mem, 512)

    xs = x_sample.reshape(ns, D_MODEL)
    (kt_s, vt_s, q_s, qx_s, g1_s, g2_s, mp_s, u_s) = inproj(
        xs, norm1, w_main, w_kvt, wb[0], conv_w, q_norm, ns, ns,
        conv_state=(state_conv[:, 0], state_conv[:, 1]))
    ck_t = cache_k.transpose(0, 2, 3, 1).reshape(cache_k.shape[0], sbw, PAGE_SIZE)
    cv_t = cache_v.transpose(0, 2, 3, 1).reshape(cache_v.shape[0], sbw, PAGE_SIZE)
    ysb_s = sb_sample(q_s.astype(F32), kt_s[0], vt_s[0], ck_t, cv_t, page_table, sb_bias,
                      new_key_visible=(past < past))
    yx_s = xattn_sample(qx_s, cache_mem_k, cache_mem_v)

    cnt0 = jnp.zeros((1, LANES), F32)
    x1_p, h_p, route_p, rt_p, cnt1 = merge(xp, mp_p, ysb_p, yx_p, g1_p, g2_p, wb[1], wb[2], wo_bf, norm2,
                                           w_route, b_route, cnt0, 512)
    x1_s, h_s, route_s, rt_s, cnt2 = merge(xs, mp_s, ysb_s, yx_s, g1_s, g2_s, wb[1], wb[2], wo_bf, norm2,
                                           w_route, b_route, cnt1, ns)

    n_slots = 2 * (nb * seq + ns)
    nblk = -(-(n_slots + N_EXPERTS * (MOE_BM - 1)) // MOE_BM)
    counts = cnt2[0, :N_EXPERTS].astype(I32)
    padded = (counts + MOE_BM - 1) // MOE_BM * MOE_BM
    ends = jnp.cumsum(padded)
    starts = ends - padded
    n_used = (ends[-1] // MOE_BM).astype(I32).reshape(1)
    blk_start = jnp.arange(nblk, dtype=I32) * MOE_BM
    block_e = jnp.sum(blk_start[:, None] >= ends[None, :], axis=1).astype(I32)
    block_e = jnp.minimum(block_e, block_e[jnp.maximum(n_used[0] - 1, 0)])

    def dests(route_t):
        e = route_t[_R_E1:_R_E2 + 1].astype(I32)
        rk = route_t[_R_R1:_R_R2 + 1].astype(I32)
        return starts[e] + rk

    dest_p, dest_s = dests(rt_p), dests(rt_s)
    buf = jnp.zeros((nblk * MOE_BM, D_MODEL), F32)
    buf = dispatch(h_p, dest_p, buf, 512)
    buf = dispatch(h_s, dest_s, buf, ns)
    ybuf = experts(buf, block_e, n_used, w1, w3, w2)
    y_p = combine(x1_p, route_p, dest_p, ybuf, 256)
    y_s = combine(x1_s, route_s, dest_s, ybuf, ns)

    def heads_last(t, b, s):
        return t.reshape(b, SB_HEADS, SB_HEAD_DIM, s).transpose(0, 3, 1, 2)

    conv_prompt = tail_p[:, SUBLANES - 2:, :]
    conv_sample = jnp.stack([state_conv[:, 1], u_s[0]], axis=1)
    return (y_p.reshape(nb, seq, D_MODEL), y_s.reshape(ns, 1, D_MODEL),
            heads_last(kt_p, nb, seq), heads_last(vt_p, nb, seq), conv_prompt,
            mk_p.reshape(nb, n_mem, X_HEADS, X_HEAD_DIM), mv_p.reshape(nb, n_mem, X_HEADS, X_HEAD_DIM),
            heads_last(kt_s, 1, ns).reshape(ns, 1, SB_HEADS, SB_HEAD_DIM),
            heads_last(vt_s, 1, ns).reshape(ns, 1, SB_HEADS, SB_HEAD_DIM),
            conv_sample)
```

```python
import functools

import jax
import jax.numpy as jnp
from jax import lax
from jax.experimental import pallas as pl
from jax.experimental.pallas import tpu as pltpu

F32 = jnp.float32
BF16 = jnp.bfloat16
I32 = jnp.int32

EPS = 1e-6
D_MODEL = 1024
BRANCH_W = 512
SB_HEADS = 8
SB_HEAD_DIM = 64
X_HEADS = 4
X_HEAD_DIM = 128
N_MEM = 256
N_GROUPS = 4
EXPERTS_PER_GROUP = 8
N_EXPERTS = N_GROUPS * EXPERTS_PER_GROUP
D_EXPERT = 512
PAGE_SIZE = 128

LANES = 128
SUBLANES = 8
VMEM_LIMIT = 56 * 1024 * 1024

SB_TK = 256
SB_TQ = 4 * SB_TK
MOE_BM = 256
XS_TOKENS = 4
ROW_DMA_UNROLL = 8


def _dot(a, b):
    return jnp.dot(a, b, preferred_element_type=F32)


def _dot_nt(a, b):
    return lax.dot_general(a, b, (((1,), (1,)), ((), ())), preferred_element_type=F32)


def _rms(x, g):
    return x * lax.rsqrt(jnp.mean(x * x, axis=-1, keepdims=True) + EPS) * g


def _softplus(z):
    return jnp.maximum(z, 0.0) + jnp.log(1.0 + jnp.exp(-jnp.abs(z)))


def _params(sem, vmem=VMEM_LIMIT):
    return pltpu.CompilerParams(dimension_semantics=sem, vmem_limit_bytes=vmem)


def _const_spec(shape):
    nd = len(shape)
    return pl.BlockSpec(shape, lambda *_: (0,) * nd, pipeline_mode=pl.Buffered(1))


def _memkv_kernel(mem_ref, g_ref, w_ref, kn_ref, mk_ref, mv_ref):
    xn = _rms(mem_ref[...], g_ref[...]).astype(BF16)
    kv = _dot(xn, w_ref[...])
    xw = X_HEADS * X_HEAD_DIM
    for h in range(X_HEADS):
        sl = slice(h * X_HEAD_DIM, (h + 1) * X_HEAD_DIM)
        mk_ref[:, sl] = _rms(kv[:, sl], kn_ref[...])
    mv_ref[...] = kv[:, xw:]


def memkv(mem2d, mem_norm, w_mem_kv_bf, k_norm, n_mem):
    n = mem2d.shape[0]
    xw = X_HEADS * X_HEAD_DIM
    return pl.pallas_call(
        _memkv_kernel,
        grid=(n // n_mem,),
        in_specs=[pl.BlockSpec((n_mem, D_MODEL), lambda i: (i, 0)),
                  _const_spec((1, D_MODEL)),
                  _const_spec((D_MODEL, 2 * xw)),
                  _const_spec((1, X_HEAD_DIM))],
        out_specs=[pl.BlockSpec((n_mem, xw), lambda i: (i, 0)),
                   pl.BlockSpec((n_mem, xw), lambda i: (i, 0))],
        out_shape=[jax.ShapeDtypeStruct((n, xw), F32), jax.ShapeDtypeStruct((n, xw), F32)],
        compiler_params=_params(("arbitrary",)),
        name="memkv",
    )(mem2d, mem_norm.reshape(1, -1), w_mem_kv_bf, k_norm.reshape(1, -1))


_C_UIN, _C_CG, _C_BG, _C_Q, _C_K, _C_QX, _C_GL = 0, 512, 1024, 1536, 2048, 3072, 3584
_W_MAIN = _C_GL + 3 * D_MODEL


def _kvt_kernel(w_ref, o_ref):
    o_ref[...] = w_ref[...].T.astype(BF16)


def kv_weight_t(w_in):
    kvw = 2 * SB_HEADS * SB_HEAD_DIM
    assert _C_K % kvw == 0
    return pl.pallas_call(
        _kvt_kernel, grid=(1,),
        in_specs=[pl.BlockSpec((D_MODEL, kvw), lambda i: (0, _C_K // kvw))],
        out_specs=pl.BlockSpec((kvw, D_MODEL), lambda i: (0, 0)),
        out_shape=jax.ShapeDtypeStruct((kvw, D_MODEL), BF16),
        compiler_params=_params(("arbitrary",)),
        name="kv_weight_t",
    )(w_in)


def _inproj_kernel(carry_conv, tiles_per_seq, tm, *refs):
    if carry_conv:
        (x_ref, n1_ref, wm_ref, wkvt_ref, wb0_ref, cw_ref, qn_ref,
         kt_ref, vt_ref, q_ref, qx_ref, g1_ref, g2_ref, mp_ref, tail_ref, ktb_ref, vtb_ref,
         carry_ref) = refs
    else:
        (x_ref, n1_ref, wm_ref, wkvt_ref, wb0_ref, cw_ref, qn_ref, um2_ref, um1_ref,
         kt_ref, vt_ref, q_ref, qx_ref, g1_ref, g2_ref, mp_ref, tail_ref) = refs

    xn = _rms(x_ref[...], n1_ref[...]).astype(BF16)

    def proj(c0, width):
        return _dot(xn, wm_ref[:, c0:c0 + width])

    kvt = _dot_nt(wkvt_ref[...], xn)
    sbw = SB_HEADS * SB_HEAD_DIM
    kt_ref[...] = kvt[:sbw]
    vt_ref[...] = kvt[sbw:]
    if carry_conv:
        for c in range(tm // SB_TK):
            ktb_ref[c] = kvt[:sbw, c * SB_TK:(c + 1) * SB_TK].astype(BF16)
            vtb_ref[c] = kvt[sbw:, c * SB_TK:(c + 1) * SB_TK].astype(BF16)

    q_ref[...] = (proj(_C_Q, sbw) * (SB_HEAD_DIM ** -0.5)).astype(BF16)

    qx = proj(_C_QX, X_HEADS * X_HEAD_DIM)
    for h in range(X_HEADS):
        sl = slice(h * X_HEAD_DIM, (h + 1) * X_HEAD_DIM)
        qx_ref[:, sl] = _rms(qx[:, sl], qn_ref[...]).astype(BF16)

    u = proj(_C_CG, BRANCH_W) * proj(_C_UIN, BRANCH_W)
    if carry_conv:
        @pl.when(pl.program_id(0) % tiles_per_seq == 0)
        def _():
            carry_ref[...] = jnp.zeros_like(carry_ref)
        ext = jnp.concatenate([carry_ref[...], u], axis=0)
        um1 = pltpu.roll(ext, 1, 0)[SUBLANES:]
        um2 = pltpu.roll(ext, 2, 0)[SUBLANES:]
        carry_ref[...] = u[tm - SUBLANES:]
        tail_ref[...] = u[tm - SUBLANES:]
    else:
        um1 = um1_ref[...]
        um2 = um2_ref[...]
        tail_ref[...] = u
    cw = cw_ref[...]
    y_conv = proj(_C_BG, BRANCH_W) * (um2 * cw[0:1] + um1 * cw[1:2] + u * cw[2:3])

    g0 = jax.nn.sigmoid(proj(_C_GL, D_MODEL))
    mp_ref[...] = g0 * _dot(y_conv.astype(BF16), wb0_ref[...])
    g1_ref[...] = jax.nn.sigmoid(proj(_C_GL + D_MODEL, D_MODEL))
    g2_ref[...] = jax.nn.sigmoid(proj(_C_GL + 2 * D_MODEL, D_MODEL))


def inproj(x2d, norm1, w_main_bf, w_kvt_bf, wb0_bf, conv_w, q_norm, seq_len, tm, conv_state=None):
    n = x2d.shape[0]
    carry = conv_state is None
    nseq = n // seq_len if carry else 1
    slen = seq_len if carry else n
    tps = slen // tm
    sbw = SB_HEADS * SB_HEAD_DIM
    xw = X_HEADS * X_HEAD_DIM
    nkb = max(tm // SB_TK, 1)
    tail_rows = SUBLANES if carry else tm

    row = lambda w: pl.BlockSpec((tm, w), lambda i: (i, 0))
    kt_spec = pl.BlockSpec((None, sbw, tm), lambda i: (i // tps, 0, i % tps))
    ktb_spec = pl.BlockSpec((None, nkb, sbw, SB_TK), lambda i: (i // tps, i % tps, 0, 0))
    in_specs = [row(D_MODEL), _const_spec((1, D_MODEL)), _const_spec((D_MODEL, _W_MAIN)),
                _const_spec((2 * sbw, D_MODEL)), _const_spec((BRANCH_W, D_MODEL)),
                _const_spec((3, BRANCH_W)), _const_spec((1, X_HEAD_DIM))]
    args = [x2d, norm1.reshape(1, -1), w_main_bf, w_kvt_bf, wb0_bf, conv_w, q_norm.reshape(1, -1)]
    scratch = []
    if carry:
        scratch = [pltpu.VMEM((SUBLANES, BRANCH_W), F32)]
    else:
        in_specs += [row(BRANCH_W), row(BRANCH_W)]
        args += list(conv_state)
    out_specs = [kt_spec, kt_spec, row(sbw), row(xw),
                 row(D_MODEL), row(D_MODEL), row(D_MODEL),
                 pl.BlockSpec((None, tail_rows, BRANCH_W), lambda i: (i // tps, 0, 0))]
    out_shape = [jax.ShapeDtypeStruct((nseq, sbw, slen), F32), jax.ShapeDtypeStruct((nseq, sbw, slen), F32),
                 jax.ShapeDtypeStruct((n, sbw), BF16), jax.ShapeDtypeStruct((n, xw), BF16),
                 jax.ShapeDtypeStruct((n, D_MODEL), F32), jax.ShapeDtypeStruct((n, D_MODEL), F32),
                 jax.ShapeDtypeStruct((n, D_MODEL), F32),
                 jax.ShapeDtypeStruct((nseq, tail_rows, BRANCH_W), F32)]
    if carry:
        out_specs += [ktb_spec, ktb_spec]
        out_shape += [jax.ShapeDtypeStruct((nseq, slen // SB_TK, sbw, SB_TK), BF16)] * 2
    return pl.pallas_call(
        functools.partial(_inproj_kernel, carry, tps, tm),
        grid=(n // tm,),
        in_specs=in_specs, out_specs=out_specs, out_shape=out_shape,
        scratch_shapes=scratch,
        compiler_params=_params(("arbitrary",)),
        name="inproj_prompt" if carry else "inproj_sample",
    )(*args)


def _sbp_kernel(bias_ref, q_ref, kt_ref, vt_ref, o_ref, acc_ref, car_ref):
    pair = pl.program_id(1)
    qi = pl.program_id(2)
    tq, tk = SB_TQ, SB_TK
    lane = lax.broadcasted_iota(I32, (tq, LANES), 1)
    q = q_ref[...]
    zero = jnp.zeros_like(q)
    qh = (jnp.where(lane < SB_HEAD_DIM, q, zero), jnp.where(lane >= SB_HEAD_DIM, q, zero))
    r = lax.broadcasted_iota(I32, (tk, tk), 0)
    c = lax.broadcasted_iota(I32, (tk, tk), 1)
    tri = (r >= c).astype(BF16)
    qrow = lax.broadcasted_iota(I32, (tq, tk), 0)
    kcol = lax.broadcasted_iota(I32, (tq, tk), 1)
    acc_ref[...] = jnp.zeros_like(acc_ref)
    car_ref[...] = jnp.zeros_like(car_ref)

    def block(j, diag_sub):
        kt = kt_ref[j]
        vt = vt_ref[j]
        rs = slice(0, tq)
        if diag_sub is not None:
            rs = slice(diag_sub * tk, tq)
            causal = (kcol + diag_sub * tk < qrow)[rs]
        for h in range(2):
            z = _dot(qh[h][rs], kt) + bias_ref[2 * pair + h]
            sp = _softplus(z)
            if diag_sub is not None:
                sp = jnp.where(causal, sp, 0.0)
            car = car_ref[h, rs]
            cum = _dot(sp.astype(BF16), tri)
            a = jnp.exp(z - (cum + jnp.concatenate([car] * (tk // LANES), axis=1)))
            if diag_sub is not None:
                a = jnp.where(causal, a, 0.0)
            acc_ref[h, rs] += _dot_nt(a.astype(BF16), vt)
            car_ref[h, rs] = car + jnp.broadcast_to(cum[:, 0:1], car.shape)

    nsub = tq // tk
    for s in reversed(range(nsub)):
        block(qi * nsub + s, s)

    def body(t, carry):
        block(qi * nsub - 1 - 2 * t, None)
        block(qi * nsub - 2 - 2 * t, None)
        return carry

    lax.fori_loop(0, qi * (nsub // 2), body, 0)
    o_ref[...] = jnp.where(lane < SB_HEAD_DIM, acc_ref[0], acc_ref[1]).astype(o_ref.dtype)


def sb_prompt(q_bf, ktb, vtb, sb_bias, seq_len):
    n = q_bf.shape[0]
    nseq = n // seq_len
    nq = seq_len // SB_TQ
    nk = seq_len // SB_TK
    npair = SB_HEADS // 2
    grid_spec = pltpu.PrefetchScalarGridSpec(
        num_scalar_prefetch=0,
        grid=(nseq, npair, nq),
        in_specs=[pl.BlockSpec(memory_space=pltpu.SMEM),
                  pl.BlockSpec((SB_TQ, LANES), lambda b, p, i: (b * nq + i, p)),
                  pl.BlockSpec((None, nk, LANES, SB_TK), lambda b, p, i: (b, 0, p, 0)),
                  pl.BlockSpec((None, nk, LANES, SB_TK), lambda b, p, i: (b, 0, p, 0))],
        out_specs=pl.BlockSpec((SB_TQ, LANES), lambda b, p, i: (b * nq + i, p)),
        scratch_shapes=[pltpu.VMEM((2, SB_TQ, LANES), F32), pltpu.VMEM((2, SB_TQ, LANES), F32)],
    )
    return pl.pallas_call(
        _sbp_kernel, grid_spec=grid_spec,
        out_shape=jax.ShapeDtypeStruct((n, SB_HEADS * SB_HEAD_DIM), BF16),
        compiler_params=_params(("arbitrary", "arbitrary", "arbitrary")),
        name="sb_prompt",
    )(sb_bias.astype(F32), q_bf, ktb, vtb)


def _xp_kernel(qx_ref, mk_ref, mv_ref, o_ref):
    for h in range(X_HEADS):
        sl = slice(h * X_HEAD_DIM, (h + 1) * X_HEAD_DIM)
        s = _dot_nt(qx_ref[:, sl], mk_ref[:, sl]) * (X_HEAD_DIM ** -0.5)
        e = jnp.exp(s - jnp.max(s, axis=-1, keepdims=True))
        p = e / jnp.sum(e, axis=-1, keepdims=True)
        o_ref[:, sl] = _dot(p.astype(BF16), mv_ref[:, sl]).astype(o_ref.dtype)


def xattn_prompt(qx_bf, mk_bf, mv_bf, seq_len, n_mem, tm):
    n = qx_bf.shape[0]
    tps = seq_len // tm
    xw = X_HEADS * X_HEAD_DIM
    return pl.pallas_call(
        _xp_kernel, grid=(n // tm,),
        in_specs=[pl.BlockSpec((tm, xw), lambda i: (i, 0)),
                  pl.BlockSpec((n_mem, xw), lambda i: (i // tps, 0)),
                  pl.BlockSpec((n_mem, xw), lambda i: (i // tps, 0))],
        out_specs=pl.BlockSpec((tm, xw), lambda i: (i, 0)),
        out_shape=jax.ShapeDtypeStruct((n, xw), BF16),
        compiler_params=_params(("arbitrary",)),
        name="xattn_prompt",
    )(qx_bf, mk_bf, mv_bf)


def _split_bf16(x):
    hi = x.astype(BF16)
    return hi, (x - hi.astype(F32)).astype(BF16)


def _sbs_kernel(new_key_visible, npages, pt_ref, bias_ref, qbd_ref, kn_ref, vn_ref, *refs):
    del pt_ref
    k_refs, v_refs, o_ref = refs[:npages], refs[npages:2 * npages], refs[2 * npages]
    b = pl.program_id(0)
    psz = PAGE_SIZE
    sbw = SB_HEADS * SB_HEAD_DIM
    rows = npages * SB_HEADS
    head_bits = SB_HEADS.bit_length() - 1
    qbd = qbd_ref[0]
    bias = bias_ref[...]

    mine = lax.broadcasted_iota(I32, (SB_HEADS, kn_ref.shape[1]), 1) == b
    zs = _dot(qbd, kn_ref[...].astype(BF16)) + bias[:, 0:1]
    seen = jnp.logical_and(mine, new_key_visible)
    sp0 = jnp.where(seen, _softplus(zs), 0.0)
    a0 = jnp.where(seen, jnp.exp(zs - sp0), 0.0)
    sp0_tot = jnp.sum(sp0, axis=-1, keepdims=True)
    a0_rows = jnp.concatenate(
        [jnp.broadcast_to(a0[h:h + 1], (SB_HEAD_DIM, a0.shape[1])) for h in range(SB_HEADS)], axis=0)
    new = jnp.sum(a0_rows * vn_ref[...], axis=-1, keepdims=True)

    z = jnp.concatenate([_dot(qbd, k_refs[j][...].astype(BF16)) + bias for j in range(npages)], axis=0)
    sp = _softplus(z)
    r = lax.broadcasted_iota(I32, (psz, psz), 0)
    c = lax.broadcasted_iota(I32, (psz, psz), 1)
    tri = (r >= c).astype(BF16)
    hi, lo = _split_bf16(sp)
    cum = _dot(hi, tri) + _dot(lo, tri)
    tot = jnp.broadcast_to(cum[:, 0:1], (rows, psz))
    rr = lax.broadcasted_iota(I32, (rows, rows), 0)
    cc = lax.broadcasted_iota(I32, (rows, rows), 1)
    later = jnp.logical_and((cc >> head_bits) > (rr >> head_bits),
                            (cc & (SB_HEADS - 1)) == (rr & (SB_HEADS - 1))).astype(BF16)
    th, tl = _split_bf16(tot)
    carry = _dot(later, th) + _dot(later, tl) + jnp.concatenate([sp0_tot] * npages, axis=0)
    a = jnp.exp(z - cum - carry)

    accs = []
    for h in range(SB_HEADS):
        hs = slice(h * SB_HEAD_DIM, (h + 1) * SB_HEAD_DIM)
        acc = jnp.zeros((SB_HEAD_DIM, psz), F32)
        for j in range(npages):
            rj = j * SB_HEADS + h
            acc = acc + jnp.broadcast_to(a[rj:rj + 1], (SB_HEAD_DIM, psz)) * v_refs[j][hs, :]
        accs.append(acc)
    acc = jnp.concatenate(accs, axis=0)
    lane0 = lax.broadcasted_iota(I32, (sbw, psz), 1) == 0
    acc = acc + jnp.where(lane0, new, 0.0)
    hi2, lo2 = _split_bf16(acc)
    ones = jnp.ones((SUBLANES, psz), BF16)
    y = _dot_nt(ones, hi2) + _dot_nt(ones, lo2)
    o_ref[pl.ds(b, 1), :] = y[0:1]


def sb_sample(q_s, kn_t, vn_t, cache_k_t, cache_v_t, page_table, sb_bias, new_key_visible):
    nb, npages = page_table.shape
    sbw = SB_HEADS * SB_HEAD_DIM
    head_of_col = jnp.arange(sbw) // SB_HEAD_DIM
    sel = head_of_col[None, :] == jnp.arange(SB_HEADS)[:, None]
    qbd = jnp.where(sel[None], q_s[:, None, :], 0.0).astype(BF16)
    assert PAGE_SIZE == LANES
    bias = jnp.broadcast_to(sb_bias.astype(F32)[:, None], (SB_HEADS, LANES))
    page = lambda j: pl.BlockSpec((None, sbw, PAGE_SIZE), lambda b, pt: (pt[b, j], 0, 0))
    grid_spec = pltpu.PrefetchScalarGridSpec(
        num_scalar_prefetch=1,
        grid=(nb,),
        in_specs=[pl.BlockSpec((SB_HEADS, LANES), lambda b, pt: (0, 0)),
                  pl.BlockSpec((1, SB_HEADS, sbw), lambda b, pt: (b, 0, 0)),
                  pl.BlockSpec((sbw, nb), lambda b, pt: (0, 0)),
                  pl.BlockSpec((sbw, nb), lambda b, pt: (0, 0))]
                 + [page(j) for j in range(npages)] * 2,
        out_specs=pl.BlockSpec((nb, sbw), lambda b, pt: (0, 0)),
    )
    return pl.pallas_call(
        functools.partial(_sbs_kernel, new_key_visible, npages), grid_spec=grid_spec,
        out_shape=jax.ShapeDtypeStruct((nb, sbw), F32),
        compiler_params=_params(("arbitrary",)),
        name="sb_sample",
    )(page_table, bias, qbd, kn_t, vn_t, *([cache_k_t] * npages), *([cache_v_t] * npages))


def _xs_kernel(qx_ref, mk_ref, mv_ref, o_ref):
    step = pl.program_id(0)
    rows = N_MEM * X_HEADS
    own = ((lax.broadcasted_iota(I32, (SUBLANES, rows), 1) & (X_HEADS - 1))
           == lax.broadcasted_iota(I32, (SUBLANES, rows), 0))
    for t in range(XS_TOKENS):
        mk = mk_ref[t].astype(BF16)
        mv = mv_ref[t].astype(BF16)
        s = _dot_nt(qx_ref[t], mk) * (X_HEAD_DIM ** -0.5)
        s = jnp.where(own, s, -jnp.inf)
        m = jnp.max(s, axis=-1, keepdims=True)
        m = jnp.where(m == -jnp.inf, 0.0, m)
        e = jnp.exp(s - m)
        den = jnp.sum(e, axis=-1, keepdims=True)
        p = e / jnp.where(den == 0.0, 1.0, den)
        y = _dot(p.astype(BF16), mv)
        o_ref[pl.ds(step * XS_TOKENS + t, 1)] = y[None, 0:X_HEADS]


def xattn_sample(qx_bf, cache_mem_k, cache_mem_v):
    nb = qx_bf.shape[0]
    rows = N_MEM * X_HEADS
    mk = cache_mem_k.reshape(nb, rows, X_HEAD_DIM)
    mv = cache_mem_v.reshape(nb, rows, X_HEAD_DIM)
    qx = qx_bf.reshape(nb, X_HEADS, X_HEAD_DIM)
    qx = jnp.concatenate([qx, jnp.zeros((nb, SUBLANES - X_HEADS, X_HEAD_DIM), BF16)], axis=1)
    assert nb % XS_TOKENS == 0
    out = pl.pallas_call(
        _xs_kernel, grid=(nb // XS_TOKENS,),
        in_specs=[pl.BlockSpec((XS_TOKENS, SUBLANES, X_HEAD_DIM), lambda b: (b, 0, 0)),
                  pl.BlockSpec((XS_TOKENS, rows, X_HEAD_DIM), lambda b: (b, 0, 0)),
                  pl.BlockSpec((XS_TOKENS, rows, X_HEAD_DIM), lambda b: (b, 0, 0))],
        out_specs=pl.BlockSpec((nb, X_HEADS, X_HEAD_DIM), lambda b: (0, 0, 0)),
        out_shape=jax.ShapeDtypeStruct((nb, X_HEADS, X_HEAD_DIM), F32),
        compiler_params=_params(("arbitrary",)),
        name="xattn_sample",
    )(qx, mk, mv)
    return out.reshape(nb, X_HEADS * X_HEAD_DIM)


_R_E1, _R_E2, _R_R1, _R_R2, _R_G1, _R_G2 = 0, 1, 2, 3, 4, 5
_GROUP_LANE0 = N_EXPERTS


def _merge_kernel(tm, x_ref, mp_ref, ysb_ref, yx_ref, g1_ref, g2_ref, wb1_ref, wb2_ref, wo_ref,
                  n2_ref, wr_ref, br_ref, cin_ref, x1_ref, h_ref, route_ref, route_t_ref, cout_ref, cnt_ref):
    @pl.when(pl.program_id(0) == 0)
    def _():
        cnt_ref[...] = cin_ref[...]

    m = (mp_ref[...] + g1_ref[...] * _dot(ysb_ref[...].astype(BF16), wb1_ref[...])
         + g2_ref[...] * _dot(yx_ref[...].astype(BF16), wb2_ref[...]))
    x1 = x_ref[...] + _dot(m.astype(BF16), wo_ref[...])
    x1_ref[...] = x1
    hn = _rms(x1, n2_ref[...])
    h_ref[...] = hn
    logits = _dot(hn.astype(BF16), wr_ref[...]) + br_ref[...]

    lane = lax.broadcasted_iota(I32, (tm, LANES), 1).astype(F32)
    big = float(LANES)
    ninf = -jnp.inf

    def first_argmax(v):
        mx = jnp.max(v, axis=-1, keepdims=True)
        idx = jnp.min(jnp.where(v == mx, lane, big), axis=-1, keepdims=True)
        return mx, idx

    gl = jnp.where((lane >= _GROUP_LANE0) & (lane < _GROUP_LANE0 + N_GROUPS), logits, ninf)
    gmax, glane = first_argmax(gl)
    gprob = 1.0 / jnp.sum(jnp.exp(gl - gmax), axis=-1, keepdims=True)
    e0 = (glane - _GROUP_LANE0) * EXPERTS_PER_GROUP
    el = jnp.where((lane >= e0) & (lane < e0 + EXPERTS_PER_GROUP), logits, ninf)
    m1, i1 = first_argmax(el)
    m2, i2 = first_argmax(jnp.where(lane == i1, ninf, el))
    e2 = jnp.exp(m2 - m1)
    gate1 = gprob / (1.0 + e2)
    gate2 = gprob * e2 / (1.0 + e2)

    oh1 = (lane == i1).astype(F32)
    oh2 = (lane == i2).astype(F32)
    cnt = oh1 + oh2
    r = lax.broadcasted_iota(I32, (tm, tm), 0)
    c = lax.broadcasted_iota(I32, (tm, tm), 1)
    before = _dot((c < r).astype(BF16), cnt.astype(BF16)) + cnt_ref[...]
    rank1 = jnp.sum(oh1 * before, axis=-1, keepdims=True)
    rank2 = jnp.sum(oh2 * before, axis=-1, keepdims=True)
    cnt_ref[...] = cnt_ref[...] + jnp.sum(cnt, axis=0, keepdims=True)
    cout_ref[...] = cnt_ref[...]

    rec = jnp.zeros((tm, LANES), F32)
    for ln, val in ((_R_E1, i1.astype(F32)), (_R_E2, i2.astype(F32)), (_R_R1, rank1), (_R_R2, rank2),
                    (_R_G1, gate1), (_R_G2, gate2)):
        rec = jnp.where(lane == ln, val, rec)
    route_ref[...] = rec
    route_t_ref[...] = rec.T[0:SUBLANES]


def merge(x2d, mp, ysb, yx, g1, g2, wb1_bf, wb2_bf, wo_bf, norm2, w_route_bf, b_route, cnt_in, tm):
    n = x2d.shape[0]
    row = lambda w: pl.BlockSpec((tm, w), lambda i: (i, 0))
    return pl.pallas_call(
        functools.partial(_merge_kernel, tm), grid=(n // tm,),
        in_specs=[row(D_MODEL), row(D_MODEL), row(BRANCH_W), row(BRANCH_W), row(D_MODEL), row(D_MODEL),
                  _const_spec((BRANCH_W, D_MODEL)), _const_spec((BRANCH_W, D_MODEL)),
                  _const_spec((D_MODEL, D_MODEL)), _const_spec((1, D_MODEL)),
                  _const_spec((D_MODEL, LANES)), _const_spec((1, LANES)), _const_spec((1, LANES))],
        out_specs=[row(D_MODEL), row(D_MODEL), row(LANES), pl.BlockSpec((SUBLANES, tm), lambda i: (0, i)),
                   pl.BlockSpec((1, LANES), lambda i: (0, 0))],
        out_shape=[jax.ShapeDtypeStruct((n, D_MODEL), F32), jax.ShapeDtypeStruct((n, D_MODEL), F32),
                   jax.ShapeDtypeStruct((n, LANES), F32), jax.ShapeDtypeStruct((SUBLANES, n), F32),
                   jax.ShapeDtypeStruct((1, LANES), F32)],
        scratch_shapes=[pltpu.VMEM((1, LANES), F32)],
        compiler_params=_params(("arbitrary",)),
        name="merge",
    )(x2d, mp, ysb, yx, g1, g2, wb1_bf, wb2_bf, wo_bf, norm2.reshape(1, -1), w_route_bf, b_route, cnt_in)


def _row_copy(src, src_row, dst, dst_row, sem):
    return pltpu.make_async_copy(src.at[pl.ds(src_row, 1)], dst.at[pl.ds(dst_row, 1)], sem)


def _dispatch_kernel(tm, dest_ref, h_ref, buf_in_ref, buf_ref, sem):
    del buf_in_ref

    def issue(t, carry):
        _row_copy(h_ref, t, buf_ref, dest_ref[0, 0, t], sem).start()
        _row_copy(h_ref, t, buf_ref, dest_ref[0, 0, tm + t], sem).start()
        return carry

    lax.fori_loop(0, tm, issue, 0, unroll=ROW_DMA_UNROLL)
    for _ in range(2):
        pltpu.make_async_copy(h_ref, buf_ref.at[pl.ds(0, tm)], sem).wait()


def _slot_tiles(dest, tm):
    nt = dest.shape[1] // tm
    return dest.reshape(2, nt, tm).transpose(1, 0, 2).reshape(nt, 1, 2 * tm)


def dispatch(h, dest, buf_zero, tm):
    n = h.shape[0]
    nt = n // tm
    return pl.pallas_call(
        functools.partial(_dispatch_kernel, tm), grid=(nt,),
        in_specs=[pl.BlockSpec((1, 1, 2 * tm), lambda i: (i, 0, 0), memory_space=pltpu.SMEM),
                  pl.BlockSpec((tm, D_MODEL), lambda i: (i, 0)),
                  pl.BlockSpec(memory_space=pl.ANY)],
        out_specs=pl.BlockSpec(memory_space=pl.ANY),
        out_shape=jax.ShapeDtypeStruct(buf_zero.shape, buf_zero.dtype),
        scratch_shapes=[pltpu.SemaphoreType.DMA(())],
        input_output_aliases={2: 0},
        compiler_params=_params(("arbitrary",)),
        name="dispatch",
    )(_slot_tiles(dest, tm), h, buf_zero)


def _experts_kernel(be_ref, nu_ref, x_ref, w1_ref, w3_ref, w2_ref, y_ref):
    del be_ref

    @pl.when(pl.program_id(0) < nu_ref[0])
    def _():
        xb = x_ref[...].astype(BF16)
        h1 = _dot(xb, w1_ref[...].astype(BF16))
        h3 = _dot(xb, w3_ref[...].astype(BF16))
        act = (h1 * jax.nn.sigmoid(h1)) * h3
        y_ref[...] = _dot(act.astype(BF16), w2_ref[...].astype(BF16))

    @pl.when(pl.program_id(0) >= nu_ref[0])
    def _():
        y_ref[...] = jnp.zeros_like(y_ref)


def experts(buf, block_e, n_used, w1, w3, w2):
    nblk = buf.shape[0] // MOE_BM
    clamp = lambda i, nu: jnp.minimum(i, nu[0] - 1)
    grid_spec = pltpu.PrefetchScalarGridSpec(
        num_scalar_prefetch=2,
        grid=(nblk,),
        in_specs=[pl.BlockSpec((MOE_BM, D_MODEL), lambda i, be, nu: (clamp(i, nu), 0)),
                  pl.BlockSpec((None, D_MODEL, D_EXPERT), lambda i, be, nu: (be[i], 0, 0)),
                  pl.BlockSpec((None, D_MODEL, D_EXPERT), lambda i, be, nu: (be[i], 0, 0)),
                  pl.BlockSpec((None, D_EXPERT, D_MODEL), lambda i, be, nu: (be[i], 0, 0))],
        out_specs=pl.BlockSpec((MOE_BM, D_MODEL), lambda i, be, nu: (i, 0)),
    )
    return pl.pallas_call(
        _experts_kernel, grid_spec=grid_spec,
        out_shape=jax.ShapeDtypeStruct(buf.shape, F32),
        compiler_params=_params(("arbitrary",)),
        name="experts",
    )(block_e, n_used, buf, w1, w3, w2)


def _combine_kernel(tm, dest_ref, x1_ref, route_ref, ybuf_ref, o_ref, rows_ref, sem):
    def issue(t, carry):
        _row_copy(ybuf_ref, dest_ref[0, 0, t], rows_ref.at[0], t, sem).start()
        _row_copy(ybuf_ref, dest_ref[0, 0, tm + t], rows_ref.at[1], t, sem).start()
        return carry

    lax.fori_loop(0, tm, issue, 0, unroll=ROW_DMA_UNROLL)
    for k in range(2):
        pltpu.make_async_copy(ybuf_ref.at[pl.ds(0, tm)], rows_ref.at[k], sem).wait()
    route = route_ref[...]
    o_ref[...] = (x1_ref[...] + route[:, _R_G1:_R_G1 + 1] * rows_ref[0]
                  + route[:, _R_G2:_R_G2 + 1] * rows_ref[1])


def combine(x1, route, dest, ybuf, tm):
    n = x1.shape[0]
    nt = n // tm
    return pl.pallas_call(
        functools.partial(_combine_kernel, tm), grid=(nt,),
        in_specs=[pl.BlockSpec((1, 1, 2 * tm), lambda i: (i, 0, 0), memory_space=pltpu.SMEM),
                  pl.BlockSpec((tm, D_MODEL), lambda i: (i, 0)),
                  pl.BlockSpec((tm, LANES), lambda i: (i, 0)),
                  pl.BlockSpec(memory_space=pl.ANY)],
        out_specs=pl.BlockSpec((tm, D_MODEL), lambda i: (i, 0)),
        out_shape=jax.ShapeDtypeStruct((n, D_MODEL), F32),
        scratch_shapes=[pltpu.VMEM((2, tm, D_MODEL), F32), pltpu.SemaphoreType.DMA(())],
        compiler_params=_params(("arbitrary",)),
        name="combine",
    )(_slot_tiles(dest, tm), x1, route, ybuf)


def kernel(x_prompt, x_sample, mem_prompt, cache_k, cache_v, cache_mem_k, cache_mem_v, state_conv, page_table, norm1, w_in, conv_w, sb_bias, mem_norm, w_mem_kv, q_norm, k_norm, w_branch, w_o, norm2, w_group, b_group, w_router, b_router, w1, w3, w2):
    nb, seq, _ = x_prompt.shape
    ns, dec_seq, _ = x_sample.shape
    assert dec_seq == 1 and seq % SB_TQ == 0 and SB_TQ % (2 * SB_TK) == 0
    n_mem = mem_prompt.shape[1]
    sbw = SB_HEADS * SB_HEAD_DIM
    xw = X_HEADS * X_HEAD_DIM
    past = page_table.shape[1] * PAGE_SIZE

    assert w_in.shape[1] == _W_MAIN and _C_QX == _C_K + 2 * sbw and _C_GL == _C_QX + xw
    w_main = w_in.astype(BF16)
    w_kvt = kv_weight_t(w_in)
    wb = w_branch.astype(BF16)
    wo_bf = w_o.astype(BF16)
    w_route = jnp.zeros((D_MODEL, LANES), F32)
    w_route = w_route.at[:, :N_EXPERTS].set(w_router).at[:, N_EXPERTS:N_EXPERTS + N_GROUPS].set(w_group)
    b_route = jnp.zeros((1, LANES), F32)
    b_route = b_route.at[0, :N_EXPERTS].set(b_router).at[0, N_EXPERTS:N_EXPERTS + N_GROUPS].set(b_group)
    w_route = w_route.astype(BF16)

    xp = x_prompt.reshape(nb * seq, D_MODEL)
    mk_p, mv_p = memkv(mem_prompt.reshape(nb * n_mem, D_MODEL), mem_norm, w_mem_kv.astype(BF16), k_norm, n_mem)
    (kt_p, vt_p, q_p, qx_p, g1_p, g2_p, mp_p, tail_p, ktb_p, vtb_p) = inproj(
        xp, norm1, w_main, w_kvt, wb[0], conv_w, q_norm, seq, 512)
    ysb_p = sb_prompt(q_p, ktb_p, vtb_p, sb_bias, seq)
    yx_p = xattn_prompt(qx_p, mk_p.astype(BF16), mv_p.astype(BF16), seq, n_mem, 512)

    xs = x_sample.reshape(ns, D_MODEL)
    (kt_s, vt_s, q_s, qx_s, g1_s, g2_s, mp_s, u_s) = inproj(
        xs, norm1, w_main, w_kvt, wb[0], conv_w, q_norm, ns, ns,
        conv_state=(state_conv[:, 0], state_conv[:, 1]))
    ck_t = cache_k.transpose(0, 2, 3, 1).reshape(cache_k.shape[0], sbw, PAGE_SIZE)
    cv_t = cache_v.transpose(0, 2, 3, 1).reshape(cache_v.shape[0], sbw, PAGE_SIZE)
    ysb_s = sb_sample(q_s.astype(F32), kt_s[0], vt_s[0], ck_t, cv_t, page_table, sb_bias,
                      new_key_visible=(past < past))
    yx_s = xattn_sample(qx_s, cache_mem_k, cache_mem_v)

    cnt0 = jnp.zeros((1, LANES), F32)
    x1_p, h_p, route_p, rt_p, cnt1 = merge(xp, mp_p, ysb_p, yx_p, g1_p, g2_p, wb[1], wb[2], wo_bf, norm2,
                                           w_route, b_route, cnt0, 512)
    x1_s, h_s, route_s, rt_s, cnt2 = merge(xs, mp_s, ysb_s, yx_s, g1_s, g2_s, wb[1], wb[2], wo_bf, norm2,
                                           w_route, b_route, cnt1, ns)

    n_slots = 2 * (nb * seq + ns)
    nblk = -(-(n_slots + N_EXPERTS * (MOE_BM - 1)) // MOE_BM)
    counts = cnt2[0, :N_EXPERTS].astype(I32)
    padded = (counts + MOE_BM - 1) // MOE_BM * MOE_BM
    ends = jnp.cumsum(padded)
    starts = ends - padded
    n_used = (ends[-1] // MOE_BM).astype(I32).reshape(1)
    blk_start = jnp.arange(nblk, dtype=I32) * MOE_BM
    block_e = jnp.sum(blk_start[:, None] >= ends[None, :], axis=1).astype(I32)
    block_e = jnp.minimum(block_e, block_e[jnp.maximum(n_used[0] - 1, 0)])

    def dests(route_t):
        e = route_t[_R_E1:_R_E2 + 1].astype(I32)
        rk = route_t[_R_R1:_R_R2 + 1].astype(I32)
        first = jnp.zeros_like(e)
        for x in range(N_EXPERTS):
            first = jnp.where(e == x, starts[x], first)
        return first + rk

    dest_p, dest_s = dests(rt_p), dests(rt_s)
    buf = jnp.zeros((nblk * MOE_BM, D_MODEL), F32)
    buf = dispatch(h_p, dest_p, buf, 512)
    buf = dispatch(h_s, dest_s, buf, ns)
    ybuf = experts(buf, block_e, n_used, w1, w3, w2)
    y_p = combine(x1_p, route_p, dest_p, ybuf, 256)
    y_s = combine(x1_s, route_s, dest_s, ybuf, ns)

    def heads_last(t, b, s):
        return t.reshape(b, SB_HEADS, SB_HEAD_DIM, s).transpose(0, 3, 1, 2)

    conv_prompt = tail_p[:, SUBLANES - 2:, :]
    conv_sample = jnp.stack([state_conv[:, 1], u_s[0]], axis=1)
    return (y_p.reshape(nb, seq, D_MODEL), y_s.reshape(ns, 1, D_MODEL),
            heads_last(kt_p, nb, seq), heads_last(vt_p, nb, seq), conv_prompt,
            mk_p.reshape(nb, n_mem, X_HEADS, X_HEAD_DIM), mv_p.reshape(nb, n_mem, X_HEADS, X_HEAD_DIM),
            heads_last(kt_s, 1, ns).reshape(ns, 1, SB_HEADS, SB_HEAD_DIM),
            heads_last(vt_s, 1, ns).reshape(ns, 1, SB_HEADS, SB_HEAD_DIM),
            conv_sample)
```

```python
import functools

import jax
import jax.numpy as jnp
from jax import lax
from jax.experimental import pallas as pl
from jax.experimental.pallas import tpu as pltpu

F32 = jnp.float32
BF16 = jnp.bfloat16
I32 = jnp.int32

EPS = 1e-6
D_MODEL = 1024
BRANCH_W = 512
SB_HEADS = 8
SB_HEAD_DIM = 64
X_HEADS = 4
X_HEAD_DIM = 128
N_MEM = 256
N_GROUPS = 4
EXPERTS_PER_GROUP = 8
N_EXPERTS = N_GROUPS * EXPERTS_PER_GROUP
D_EXPERT = 512
PAGE_SIZE = 128

LANES = 128
SUBLANES = 8
VMEM_LIMIT = 56 * 1024 * 1024

SB_TK = 256
SB_TQ = 4 * SB_TK
MOE_BM = 256
XS_TOKENS = 4
ROW_DMA_UNROLL = 8


def _dot(a, b):
    return jnp.dot(a, b, preferred_element_type=F32)


def _dot_nt(a, b):
    return lax.dot_general(a, b, (((1,), (1,)), ((), ())), preferred_element_type=F32)


def _rms(x, g):
    return x * lax.rsqrt(jnp.mean(x * x, axis=-1, keepdims=True) + EPS) * g


def _softplus(z):
    return jnp.maximum(z, 0.0) + jnp.log(1.0 + jnp.exp(-jnp.abs(z)))


ROW_CHUNKS = D_MODEL // LANES
ROW_REC = (ROW_CHUNKS, 1, LANES)


def _to_row_records(ref, x):
    for c in range(ROW_CHUNKS):
        ref[:, c, 0, :] = x[:, c * LANES:(c + 1) * LANES]


def _from_row_records(ref, *mid):
    return jnp.concatenate([ref[(slice(None),) + mid + (c, 0, slice(None))] for c in range(ROW_CHUNKS)], axis=1)


def _params(sem, vmem=VMEM_LIMIT):
    return pltpu.CompilerParams(dimension_semantics=sem, vmem_limit_bytes=vmem)


def _const_spec(shape):
    nd = len(shape)
    return pl.BlockSpec(shape, lambda *_: (0,) * nd, pipeline_mode=pl.Buffered(1))


def _memkv_kernel(mem_ref, g_ref, w_ref, kn_ref, mk_ref, mv_ref):
    xn = _rms(mem_ref[...], g_ref[...]).astype(BF16)
    kv = _dot(xn, w_ref[...])
    xw = X_HEADS * X_HEAD_DIM
    for h in range(X_HEADS):
        sl = slice(h * X_HEAD_DIM, (h + 1) * X_HEAD_DIM)
        mk_ref[:, sl] = _rms(kv[:, sl], kn_ref[...])
    mv_ref[...] = kv[:, xw:]


def memkv(mem2d, mem_norm, w_mem_kv_bf, k_norm, n_mem):
    n = mem2d.shape[0]
    xw = X_HEADS * X_HEAD_DIM
    return pl.pallas_call(
        _memkv_kernel,
        grid=(n // n_mem,),
        in_specs=[pl.BlockSpec((n_mem, D_MODEL), lambda i: (i, 0)),
                  _const_spec((1, D_MODEL)),
                  _const_spec((D_MODEL, 2 * xw)),
                  _const_spec((1, X_HEAD_DIM))],
        out_specs=[pl.BlockSpec((n_mem, xw), lambda i: (i, 0)),
                   pl.BlockSpec((n_mem, xw), lambda i: (i, 0))],
        out_shape=[jax.ShapeDtypeStruct((n, xw), F32), jax.ShapeDtypeStruct((n, xw), F32)],
        compiler_params=_params(("arbitrary",)),
        name="memkv",
    )(mem2d, mem_norm.reshape(1, -1), w_mem_kv_bf, k_norm.reshape(1, -1))


_C_UIN, _C_CG, _C_BG, _C_Q, _C_K, _C_QX, _C_GL = 0, 512, 1024, 1536, 2048, 3072, 3584
_W_MAIN = _C_GL + 3 * D_MODEL


def _kvt_kernel(w_ref, o_ref):
    o_ref[...] = w_ref[...].T.astype(BF16)


def kv_weight_t(w_in):
    kvw = 2 * SB_HEADS * SB_HEAD_DIM
    assert _C_K % kvw == 0
    return pl.pallas_call(
        _kvt_kernel, grid=(1,),
        in_specs=[pl.BlockSpec((D_MODEL, kvw), lambda i: (0, _C_K // kvw))],
        out_specs=pl.BlockSpec((kvw, D_MODEL), lambda i: (0, 0)),
        out_shape=jax.ShapeDtypeStruct((kvw, D_MODEL), BF16),
        compiler_params=_params(("arbitrary",)),
        name="kv_weight_t",
    )(w_in)


def _inproj_kernel(carry_conv, tiles_per_seq, tm, *refs):
    if carry_conv:
        (x_ref, n1_ref, wm_ref, wkvt_ref, wb0_ref, cw_ref, qn_ref,
         kt_ref, vt_ref, q_ref, qx_ref, g1_ref, g2_ref, mp_ref, tail_ref, ktb_ref, vtb_ref,
         carry_ref) = refs
    else:
        (x_ref, n1_ref, wm_ref, wkvt_ref, wb0_ref, cw_ref, qn_ref, um2_ref, um1_ref,
         kt_ref, vt_ref, q_ref, qx_ref, g1_ref, g2_ref, mp_ref, tail_ref) = refs

    xn = _rms(x_ref[...], n1_ref[...]).astype(BF16)

    def proj(c0, width):
        return _dot(xn, wm_ref[:, c0:c0 + width])

    kvt = _dot_nt(wkvt_ref[...], xn)
    sbw = SB_HEADS * SB_HEAD_DIM
    kt_ref[...] = kvt[:sbw]
    vt_ref[...] = kvt[sbw:]
    if carry_conv:
        for c in range(tm // SB_TK):
            ktb_ref[c] = kvt[:sbw, c * SB_TK:(c + 1) * SB_TK].astype(BF16)
            vtb_ref[c] = kvt[sbw:, c * SB_TK:(c + 1) * SB_TK].astype(BF16)

    q_ref[...] = (proj(_C_Q, sbw) * (SB_HEAD_DIM ** -0.5)).astype(BF16)

    qx = proj(_C_QX, X_HEADS * X_HEAD_DIM)
    for h in range(X_HEADS):
        sl = slice(h * X_HEAD_DIM, (h + 1) * X_HEAD_DIM)
        qx_ref[:, sl] = _rms(qx[:, sl], qn_ref[...]).astype(BF16)

    u = proj(_C_CG, BRANCH_W) * proj(_C_UIN, BRANCH_W)
    if carry_conv:
        @pl.when(pl.program_id(0) % tiles_per_seq == 0)
        def _():
            carry_ref[...] = jnp.zeros_like(carry_ref)
        ext = jnp.concatenate([carry_ref[...], u], axis=0)
        um1 = pltpu.roll(ext, 1, 0)[SUBLANES:]
        um2 = pltpu.roll(ext, 2, 0)[SUBLANES:]
        carry_ref[...] = u[tm - SUBLANES:]
        tail_ref[...] = u[tm - SUBLANES:]
    else:
        um1 = um1_ref[...]
        um2 = um2_ref[...]
        tail_ref[...] = u
    cw = cw_ref[...]
    y_conv = proj(_C_BG, BRANCH_W) * (um2 * cw[0:1] + um1 * cw[1:2] + u * cw[2:3])

    g0 = jax.nn.sigmoid(proj(_C_GL, D_MODEL))
    mp_ref[...] = g0 * _dot(y_conv.astype(BF16), wb0_ref[...])
    g1_ref[...] = jax.nn.sigmoid(proj(_C_GL + D_MODEL, D_MODEL))
    g2_ref[...] = jax.nn.sigmoid(proj(_C_GL + 2 * D_MODEL, D_MODEL))


def inproj(x2d, norm1, w_main_bf, w_kvt_bf, wb0_bf, conv_w, q_norm, seq_len, tm, conv_state=None):
    n = x2d.shape[0]
    carry = conv_state is None
    nseq = n // seq_len if carry else 1
    slen = seq_len if carry else n
    tps = slen // tm
    sbw = SB_HEADS * SB_HEAD_DIM
    xw = X_HEADS * X_HEAD_DIM
    nkb = max(tm // SB_TK, 1)
    tail_rows = SUBLANES if carry else tm

    row = lambda w: pl.BlockSpec((tm, w), lambda i: (i, 0))
    kt_spec = pl.BlockSpec((None, sbw, tm), lambda i: (i // tps, 0, i % tps))
    ktb_spec = pl.BlockSpec((None, nkb, sbw, SB_TK), lambda i: (i // tps, i % tps, 0, 0))
    in_specs = [row(D_MODEL), _const_spec((1, D_MODEL)), _const_spec((D_MODEL, _W_MAIN)),
                _const_spec((2 * sbw, D_MODEL)), _const_spec((BRANCH_W, D_MODEL)),
                _const_spec((3, BRANCH_W)), _const_spec((1, X_HEAD_DIM))]
    args = [x2d, norm1.reshape(1, -1), w_main_bf, w_kvt_bf, wb0_bf, conv_w, q_norm.reshape(1, -1)]
    scratch = []
    if carry:
        scratch = [pltpu.VMEM((SUBLANES, BRANCH_W), F32)]
    else:
        in_specs += [row(BRANCH_W), row(BRANCH_W)]
        args += list(conv_state)
    out_specs = [kt_spec, kt_spec, row(sbw), row(xw),
                 row(D_MODEL), row(D_MODEL), row(D_MODEL),
                 pl.BlockSpec((None, tail_rows, BRANCH_W), lambda i: (i // tps, 0, 0))]
    out_shape = [jax.ShapeDtypeStruct((nseq, sbw, slen), F32), jax.ShapeDtypeStruct((nseq, sbw, slen), F32),
                 jax.ShapeDtypeStruct((n, sbw), BF16), jax.ShapeDtypeStruct((n, xw), BF16),
                 jax.ShapeDtypeStruct((n, D_MODEL), F32), jax.ShapeDtypeStruct((n, D_MODEL), F32),
                 jax.ShapeDtypeStruct((n, D_MODEL), F32),
                 jax.ShapeDtypeStruct((nseq, tail_rows, BRANCH_W), F32)]
    if carry:
        out_specs += [ktb_spec, ktb_spec]
        out_shape += [jax.ShapeDtypeStruct((nseq, slen // SB_TK, sbw, SB_TK), BF16)] * 2
    return pl.pallas_call(
        functools.partial(_inproj_kernel, carry, tps, tm),
        grid=(n // tm,),
        in_specs=in_specs, out_specs=out_specs, out_shape=out_shape,
        scratch_shapes=scratch,
        compiler_params=_params(("arbitrary",)),
        name="inproj_prompt" if carry else "inproj_sample",
    )(*args)


def _sbp_kernel(bias_ref, q_ref, kt_ref, vt_ref, o_ref, acc_ref, car_ref):
    pair = pl.program_id(1)
    qi = pl.program_id(2)
    tq, tk = SB_TQ, SB_TK
    lane = lax.broadcasted_iota(I32, (tq, LANES), 1)
    q = q_ref[...]
    zero = jnp.zeros_like(q)
    qh = (jnp.where(lane < SB_HEAD_DIM, q, zero), jnp.where(lane >= SB_HEAD_DIM, q, zero))
    r = lax.broadcasted_iota(I32, (tk, tk), 0)
    c = lax.broadcasted_iota(I32, (tk, tk), 1)
    tri = (r >= c).astype(BF16)
    qrow = lax.broadcasted_iota(I32, (tq, tk), 0)
    kcol = lax.broadcasted_iota(I32, (tq, tk), 1)
    acc_ref[...] = jnp.zeros_like(acc_ref)
    car_ref[...] = jnp.zeros_like(car_ref)

    def block(j, diag_sub):
        kt = kt_ref[j]
        vt = vt_ref[j]
        rs = slice(0, tq)
        if diag_sub is not None:
            rs = slice(diag_sub * tk, tq)
            causal = (kcol + diag_sub * tk < qrow)[rs]
        for h in range(2):
            z = _dot(qh[h][rs], kt) + bias_ref[2 * pair + h]
            sp = _softplus(z)
            if diag_sub is not None:
                sp = jnp.where(causal, sp, 0.0)
            car = car_ref[h, rs]
            cum = _dot(sp.astype(BF16), tri)
            a = jnp.exp(z - (cum + jnp.concatenate([car] * (tk // LANES), axis=1)))
            if diag_sub is not None:
                a = jnp.where(causal, a, 0.0)
            acc_ref[h, rs] += _dot_nt(a.astype(BF16), vt)
            car_ref[h, rs] = car + jnp.broadcast_to(cum[:, 0:1], car.shape)

    nsub = tq // tk
    for s in reversed(range(nsub)):
        block(qi * nsub + s, s)

    def body(t, carry):
        block(qi * nsub - 1 - 2 * t, None)
        block(qi * nsub - 2 - 2 * t, None)
        return carry

    lax.fori_loop(0, qi * (nsub // 2), body, 0)
    o_ref[...] = jnp.where(lane < SB_HEAD_DIM, acc_ref[0], acc_ref[1]).astype(o_ref.dtype)


def sb_prompt(q_bf, ktb, vtb, sb_bias, seq_len):
    n = q_bf.shape[0]
    nseq = n // seq_len
    nq = seq_len // SB_TQ
    nk = seq_len // SB_TK
    npair = SB_HEADS // 2
    grid_spec = pltpu.PrefetchScalarGridSpec(
        num_scalar_prefetch=0,
        grid=(nseq, npair, nq),
        in_specs=[pl.BlockSpec(memory_space=pltpu.SMEM),
                  pl.BlockSpec((SB_TQ, LANES), lambda b, p, i: (b * nq + i, p)),
                  pl.BlockSpec((None, nk, LANES, SB_TK), lambda b, p, i: (b, 0, p, 0)),
                  pl.BlockSpec((None, nk, LANES, SB_TK), lambda b, p, i: (b, 0, p, 0))],
        out_specs=pl.BlockSpec((SB_TQ, LANES), lambda b, p, i: (b * nq + i, p)),
        scratch_shapes=[pltpu.VMEM((2, SB_TQ, LANES), F32), pltpu.VMEM((2, SB_TQ, LANES), F32)],
    )
    return pl.pallas_call(
        _sbp_kernel, grid_spec=grid_spec,
        out_shape=jax.ShapeDtypeStruct((n, SB_HEADS * SB_HEAD_DIM), BF16),
        compiler_params=_params(("arbitrary", "arbitrary", "arbitrary")),
        name="sb_prompt",
    )(sb_bias.astype(F32), q_bf, ktb, vtb)


def _xp_kernel(qx_ref, mk_ref, mv_ref, o_ref):
    for h in range(X_HEADS):
        sl = slice(h * X_HEAD_DIM, (h + 1) * X_HEAD_DIM)
        s = _dot_nt(qx_ref[:, sl], mk_ref[:, sl]) * (X_HEAD_DIM ** -0.5)
        e = jnp.exp(s - jnp.max(s, axis=-1, keepdims=True))
        p = e / jnp.sum(e, axis=-1, keepdims=True)
        o_ref[:, sl] = _dot(p.astype(BF16), mv_ref[:, sl]).astype(o_ref.dtype)


def xattn_prompt(qx_bf, mk_bf, mv_bf, seq_len, n_mem, tm):
    n = qx_bf.shape[0]
    tps = seq_len // tm
    xw = X_HEADS * X_HEAD_DIM
    return pl.pallas_call(
        _xp_kernel, grid=(n // tm,),
        in_specs=[pl.BlockSpec((tm, xw), lambda i: (i, 0)),
                  pl.BlockSpec((n_mem, xw), lambda i: (i // tps, 0)),
                  pl.BlockSpec((n_mem, xw), lambda i: (i // tps, 0))],
        out_specs=pl.BlockSpec((tm, xw), lambda i: (i, 0)),
        out_shape=jax.ShapeDtypeStruct((n, xw), BF16),
        compiler_params=_params(("arbitrary",)),
        name="xattn_prompt",
    )(qx_bf, mk_bf, mv_bf)


def _split_bf16(x):
    hi = x.astype(BF16)
    return hi, (x - hi.astype(F32)).astype(BF16)


def _sbs_kernel(new_key_visible, npages, pt_ref, bias_ref, qbd_ref, kn_ref, vn_ref, *refs):
    del pt_ref
    k_refs, v_refs, o_ref = refs[:npages], refs[npages:2 * npages], refs[2 * npages]
    b = pl.program_id(0)
    psz = PAGE_SIZE
    sbw = SB_HEADS * SB_HEAD_DIM
    rows = npages * SB_HEADS
    head_bits = SB_HEADS.bit_length() - 1
    qbd = qbd_ref[0]
    bias = bias_ref[...]

    mine = lax.broadcasted_iota(I32, (SB_HEADS, kn_ref.shape[1]), 1) == b
    zs = _dot(qbd, kn_ref[...].astype(BF16)) + bias[:, 0:1]
    seen = jnp.logical_and(mine, new_key_visible)
    sp0 = jnp.where(seen, _softplus(zs), 0.0)
    a0 = jnp.where(seen, jnp.exp(zs - sp0), 0.0)
    sp0_tot = jnp.sum(sp0, axis=-1, keepdims=True)
    a0_rows = jnp.concatenate(
        [jnp.broadcast_to(a0[h:h + 1], (SB_HEAD_DIM, a0.shape[1])) for h in range(SB_HEADS)], axis=0)
    new = jnp.sum(a0_rows * vn_ref[...], axis=-1, keepdims=True)

    z = jnp.concatenate([_dot(qbd, k_refs[j][...].astype(BF16)) + bias for j in range(npages)], axis=0)
    sp = _softplus(z)
    r = lax.broadcasted_iota(I32, (psz, psz), 0)
    c = lax.broadcasted_iota(I32, (psz, psz), 1)
    tri = (r >= c).astype(BF16)
    hi, lo = _split_bf16(sp)
    cum = _dot(hi, tri) + _dot(lo, tri)
    tot = jnp.broadcast_to(cum[:, 0:1], (rows, psz))
    rr = lax.broadcasted_iota(I32, (rows, rows), 0)
    cc = lax.broadcasted_iota(I32, (rows, rows), 1)
    later = jnp.logical_and((cc >> head_bits) > (rr >> head_bits),
                            (cc & (SB_HEADS - 1)) == (rr & (SB_HEADS - 1))).astype(BF16)
    th, tl = _split_bf16(tot)
    carry = _dot(later, th) + _dot(later, tl) + jnp.concatenate([sp0_tot] * npages, axis=0)
    a = jnp.exp(z - cum - carry)

    accs = []
    for h in range(SB_HEADS):
        hs = slice(h * SB_HEAD_DIM, (h + 1) * SB_HEAD_DIM)
        acc = jnp.zeros((SB_HEAD_DIM, psz), F32)
        for j in range(npages):
            rj = j * SB_HEADS + h
            acc = acc + jnp.broadcast_to(a[rj:rj + 1], (SB_HEAD_DIM, psz)) * v_refs[j][hs, :]
        accs.append(acc)
    acc = jnp.concatenate(accs, axis=0)
    lane0 = lax.broadcasted_iota(I32, (sbw, psz), 1) == 0
    acc = acc + jnp.where(lane0, new, 0.0)
    hi2, lo2 = _split_bf16(acc)
    ones = jnp.ones((SUBLANES, psz), BF16)
    y = _dot_nt(ones, hi2) + _dot_nt(ones, lo2)
    o_ref[pl.ds(b, 1), :] = y[0:1]


def sb_sample(q_s, kn_t, vn_t, cache_k_t, cache_v_t, page_table, sb_bias, new_key_visible):
    nb, npages = page_table.shape
    sbw = SB_HEADS * SB_HEAD_DIM
    head_of_col = jnp.arange(sbw) // SB_HEAD_DIM
    sel = head_of_col[None, :] == jnp.arange(SB_HEADS)[:, None]
    qbd = jnp.where(sel[None], q_s[:, None, :], 0.0).astype(BF16)
    assert PAGE_SIZE == LANES
    bias = jnp.broadcast_to(sb_bias.astype(F32)[:, None], (SB_HEADS, LANES))
    page = lambda j: pl.BlockSpec((None, sbw, PAGE_SIZE), lambda b, pt: (pt[b, j], 0, 0))
    grid_spec = pltpu.PrefetchScalarGridSpec(
        num_scalar_prefetch=1,
        grid=(nb,),
        in_specs=[pl.BlockSpec((SB_HEADS, LANES), lambda b, pt: (0, 0)),
                  pl.BlockSpec((1, SB_HEADS, sbw), lambda b, pt: (b, 0, 0)),
                  pl.BlockSpec((sbw, nb), lambda b, pt: (0, 0)),
                  pl.BlockSpec((sbw, nb), lambda b, pt: (0, 0))]
                 + [page(j) for j in range(npages)] * 2,
        out_specs=pl.BlockSpec((nb, sbw), lambda b, pt: (0, 0)),
    )
    return pl.pallas_call(
        functools.partial(_sbs_kernel, new_key_visible, npages), grid_spec=grid_spec,
        out_shape=jax.ShapeDtypeStruct((nb, sbw), F32),
        compiler_params=_params(("arbitrary",)),
        name="sb_sample",
    )(page_table, bias, qbd, kn_t, vn_t, *([cache_k_t] * npages), *([cache_v_t] * npages))


def _xs_kernel(qx_ref, mk_ref, mv_ref, o_ref):
    step = pl.program_id(0)
    rows = N_MEM * X_HEADS
    own = ((lax.broadcasted_iota(I32, (SUBLANES, rows), 1) & (X_HEADS - 1))
           == lax.broadcasted_iota(I32, (SUBLANES, rows), 0))
    for t in range(XS_TOKENS):
        mk = mk_ref[t].astype(BF16)
        mv = mv_ref[t].astype(BF16)
        s = _dot_nt(qx_ref[t], mk) * (X_HEAD_DIM ** -0.5)
        s = jnp.where(own, s, -jnp.inf)
        m = jnp.max(s, axis=-1, keepdims=True)
        m = jnp.where(m == -jnp.inf, 0.0, m)
        e = jnp.exp(s - m)
        den = jnp.sum(e, axis=-1, keepdims=True)
        p = e / jnp.where(den == 0.0, 1.0, den)
        y = _dot(p.astype(BF16), mv)
        o_ref[pl.ds(step * XS_TOKENS + t, 1)] = y[None, 0:X_HEADS]


def xattn_sample(qx_bf, cache_mem_k, cache_mem_v):
    nb = qx_bf.shape[0]
    rows = N_MEM * X_HEADS
    mk = cache_mem_k.reshape(nb, rows, X_HEAD_DIM)
    mv = cache_mem_v.reshape(nb, rows, X_HEAD_DIM)
    qx = qx_bf.reshape(nb, X_HEADS, X_HEAD_DIM)
    qx = jnp.concatenate([qx, jnp.zeros((nb, SUBLANES - X_HEADS, X_HEAD_DIM), BF16)], axis=1)
    assert nb % XS_TOKENS == 0
    out = pl.pallas_call(
        _xs_kernel, grid=(nb // XS_TOKENS,),
        in_specs=[pl.BlockSpec((XS_TOKENS, SUBLANES, X_HEAD_DIM), lambda b: (b, 0, 0)),
                  pl.BlockSpec((XS_TOKENS, rows, X_HEAD_DIM), lambda b: (b, 0, 0)),
                  pl.BlockSpec((XS_TOKENS, rows, X_HEAD_DIM), lambda b: (b, 0, 0))],
        out_specs=pl.BlockSpec((nb, X_HEADS, X_HEAD_DIM), lambda b: (0, 0, 0)),
        out_shape=jax.ShapeDtypeStruct((nb, X_HEADS, X_HEAD_DIM), F32),
        compiler_params=_params(("arbitrary",)),
        name="xattn_sample",
    )(qx, mk, mv)
    return out.reshape(nb, X_HEADS * X_HEAD_DIM)


_R_E1, _R_E2, _R_R1, _R_R2, _R_G1, _R_G2 = 0, 1, 2, 3, 4, 5
_GROUP_LANE0 = N_EXPERTS


def _merge_kernel(tm, x_ref, mp_ref, ysb_ref, yx_ref, g1_ref, g2_ref, wb1_ref, wb2_ref, wo_ref,
                  n2_ref, wr_ref, br_ref, cin_ref, x1_ref, h_ref, route_ref, route_t_ref, cout_ref, cnt_ref):
    @pl.when(pl.program_id(0) == 0)
    def _():
        cnt_ref[...] = cin_ref[...]

    m = (mp_ref[...] + g1_ref[...] * _dot(ysb_ref[...].astype(BF16), wb1_ref[...])
         + g2_ref[...] * _dot(yx_ref[...].astype(BF16), wb2_ref[...]))
    x1 = x_ref[...] + _dot(m.astype(BF16), wo_ref[...])
    x1_ref[...] = x1
    hn = _rms(x1, n2_ref[...])
    _to_row_records(h_ref, hn)
    logits = _dot(hn.astype(BF16), wr_ref[...]) + br_ref[...]

    lane = lax.broadcasted_iota(I32, (tm, LANES), 1).astype(F32)
    big = float(LANES)
    ninf = -jnp.inf

    def first_argmax(v):
        mx = jnp.max(v, axis=-1, keepdims=True)
        idx = jnp.min(jnp.where(v == mx, lane, big), axis=-1, keepdims=True)
        return mx, idx

    gl = jnp.where((lane >= _GROUP_LANE0) & (lane < _GROUP_LANE0 + N_GROUPS), logits, ninf)
    gmax, glane = first_argmax(gl)
    gprob = 1.0 / jnp.sum(jnp.exp(gl - gmax), axis=-1, keepdims=True)
    e0 = (glane - _GROUP_LANE0) * EXPERTS_PER_GROUP
    el = jnp.where((lane >= e0) & (lane < e0 + EXPERTS_PER_GROUP), logits, ninf)
    m1, i1 = first_argmax(el)
    m2, i2 = first_argmax(jnp.where(lane == i1, ninf, el))
    e2 = jnp.exp(m2 - m1)
    gate1 = gprob / (1.0 + e2)
    gate2 = gprob * e2 / (1.0 + e2)

    oh1 = (lane == i1).astype(F32)
    oh2 = (lane == i2).astype(F32)
    cnt = oh1 + oh2
    r = lax.broadcasted_iota(I32, (tm, tm), 0)
    c = lax.broadcasted_iota(I32, (tm, tm), 1)
    before = _dot((c < r).astype(BF16), cnt.astype(BF16)) + cnt_ref[...]
    rank1 = jnp.sum(oh1 * before, axis=-1, keepdims=True)
    rank2 = jnp.sum(oh2 * before, axis=-1, keepdims=True)
    cnt_ref[...] = cnt_ref[...] + jnp.sum(cnt, axis=0, keepdims=True)
    cout_ref[...] = cnt_ref[...]

    rec = jnp.zeros((tm, LANES), F32)
    for ln, val in ((_R_E1, i1.astype(F32)), (_R_E2, i2.astype(F32)), (_R_R1, rank1), (_R_R2, rank2),
                    (_R_G1, gate1), (_R_G2, gate2)):
        rec = jnp.where(lane == ln, val, rec)
    route_ref[...] = rec
    route_t_ref[...] = rec.T[0:SUBLANES]


def merge(x2d, mp, ysb, yx, g1, g2, wb1_bf, wb2_bf, wo_bf, norm2, w_route_bf, b_route, cnt_in, tm):
    n = x2d.shape[0]
    row = lambda w: pl.BlockSpec((tm, w), lambda i: (i, 0))
    return pl.pallas_call(
        functools.partial(_merge_kernel, tm), grid=(n // tm,),
        in_specs=[row(D_MODEL), row(D_MODEL), row(BRANCH_W), row(BRANCH_W), row(D_MODEL), row(D_MODEL),
                  _const_spec((BRANCH_W, D_MODEL)), _const_spec((BRANCH_W, D_MODEL)),
                  _const_spec((D_MODEL, D_MODEL)), _const_spec((1, D_MODEL)),
                  _const_spec((D_MODEL, LANES)), _const_spec((1, LANES)), _const_spec((1, LANES))],
        out_specs=[row(D_MODEL), pl.BlockSpec((tm,) + ROW_REC, lambda i: (i, 0, 0, 0)), row(LANES),
                   pl.BlockSpec((SUBLANES, tm), lambda i: (0, i)),
                   pl.BlockSpec((1, LANES), lambda i: (0, 0))],
        out_shape=[jax.ShapeDtypeStruct((n, D_MODEL), F32), jax.ShapeDtypeStruct((n,) + ROW_REC, F32),
                   jax.ShapeDtypeStruct((n, LANES), F32), jax.ShapeDtypeStruct((SUBLANES, n), F32),
                   jax.ShapeDtypeStruct((1, LANES), F32)],
        scratch_shapes=[pltpu.VMEM((1, LANES), F32)],
        compiler_params=_params(("arbitrary",)),
        name="merge",
    )(x2d, mp, ysb, yx, g1, g2, wb1_bf, wb2_bf, wo_bf, norm2.reshape(1, -1), w_route_bf, b_route, cnt_in)


def _experts_kernel(n_tok, nblk, be_ref, nu_ref, sl_cur_ref, sl_nxt_ref, h_ref, w1_ref, w3_ref, w2_ref,
                    y2_ref, xbuf, ybuf, zrec, sem_g, sem_s):
    del be_ref
    i = pl.program_id(0)
    nu = nu_ref[0]
    par = i & 1
    bm = MOE_BM

    def gather(sl_ref, p):
        def issue(r, carry):
            tok = jnp.minimum(lax.shift_right_logical(sl_ref[0, 0, r], 1), n_tok - 1)
            pltpu.make_async_copy(h_ref.at[tok], xbuf.at[p, :, pl.ds(r, 1), :], sem_g.at[p]).start()
            return carry
        lax.fori_loop(0, bm, issue, 0, unroll=ROW_DMA_UNROLL)

    def scatter(p):
        def issue(r, carry):
            pltpu.make_async_copy(ybuf.at[p, :, pl.ds(r, 1), :], y2_ref.at[sl_cur_ref[0, 0, r]], sem_s.at[p]).start()
            return carry
        lax.fori_loop(0, bm, issue, 0, unroll=ROW_DMA_UNROLL)

    def wait_gather(p):
        pltpu.make_async_copy(xbuf.at[p], xbuf.at[p], sem_g.at[p]).wait()

    def wait_scatter(p):
        pltpu.make_async_copy(ybuf.at[p], ybuf.at[p], sem_s.at[p]).wait()

    @pl.when(i == 0)
    def _():
        zrec[...] = jnp.zeros(zrec.shape, F32)
        for d in range(2):
            fill = pltpu.make_async_copy(zrec, y2_ref.at[pl.ds(2 * n_tok + d * bm, bm)], sem_s.at[1])
            fill.start()
            fill.wait()
        gather(sl_cur_ref, 0)

    @pl.when(i + 1 < nu)
    def _():
        gather(sl_nxt_ref, 1 - par)

    @pl.when(i < nu)
    def _():
        wait_gather(par)
        xb = jnp.concatenate([xbuf[par, c] for c in range(ROW_CHUNKS)], axis=1).astype(BF16)
        h1 = _dot(xb, w1_ref[...].astype(BF16))
        h3 = _dot(xb, w3_ref[...].astype(BF16))
        act = (h1 * jax.nn.sigmoid(h1)) * h3
        y = _dot(act.astype(BF16), w2_ref[...].astype(BF16))
        for c in range(ROW_CHUNKS):
            ybuf[par, c] = y[:, c * LANES:(c + 1) * LANES]
        scatter(par)

    @pl.when(jnp.logical_and(i >= 1, i - 1 < nu))
    def _():
        wait_scatter(1 - par)

    @pl.when(jnp.logical_and(i == nblk - 1, i < nu))
    def _():
        wait_scatter(par)


def experts(h_all, slots, block_e, n_used, w1, w3, w2):
    n_tok = h_all.shape[0]
    nblk = slots.shape[0]
    last = lambda nu: jnp.maximum(nu[0] - 1, 0)
    grid_spec = pltpu.PrefetchScalarGridSpec(
        num_scalar_prefetch=2,
        grid=(nblk,),
        in_specs=[pl.BlockSpec((1, 1, MOE_BM), lambda i, be, nu: (jnp.minimum(i, last(nu)), 0, 0),
                               memory_space=pltpu.SMEM),
                  pl.BlockSpec((1, 1, MOE_BM), lambda i, be, nu: (jnp.minimum(i + 1, last(nu)), 0, 0),
                               memory_space=pltpu.SMEM),
                  pl.BlockSpec(memory_space=pl.ANY),
                  pl.BlockSpec((None, D_MODEL, D_EXPERT), lambda i, be, nu: (be[i], 0, 0)),
                  pl.BlockSpec((None, D_MODEL, D_EXPERT), lambda i, be, nu: (be[i], 0, 0)),
                  pl.BlockSpec((None, D_EXPERT, D_MODEL), lambda i, be, nu: (be[i], 0, 0))],
        out_specs=pl.BlockSpec(memory_space=pl.ANY),
        scratch_shapes=[pltpu.VMEM((2, ROW_CHUNKS, MOE_BM, LANES), F32),
                        pltpu.VMEM((2, ROW_CHUNKS, MOE_BM, LANES), F32),
                        pltpu.VMEM((MOE_BM,) + ROW_REC, F32),
                        pltpu.SemaphoreType.DMA((2,)), pltpu.SemaphoreType.DMA((2,))],
    )
    return pl.pallas_call(
        functools.partial(_experts_kernel, n_tok, nblk), grid_spec=grid_spec,
        out_shape=jax.ShapeDtypeStruct((2 * n_tok + 2 * MOE_BM,) + ROW_REC, F32),
        compiler_params=_params(("arbitrary",)),
        name="experts",
    )(block_e, n_used, slots, slots, h_all, w1, w3, w2)


def _moe_add_kernel(x1_ref, route_ref, y2_ref, o_ref):
    route = route_ref[...]
    o_ref[...] = (x1_ref[...] + route[:, _R_G1:_R_G1 + 1] * _from_row_records(y2_ref, 0)
                  + route[:, _R_G2:_R_G2 + 1] * _from_row_records(y2_ref, 1))


def moe_add(x1, route, y2_pairs, row_off, tm):
    n = x1.shape[0]
    off = row_off // tm
    return pl.pallas_call(
        _moe_add_kernel, grid=(n // tm,),
        in_specs=[pl.BlockSpec((tm, D_MODEL), lambda i: (i, 0)),
                  pl.BlockSpec((tm, LANES), lambda i: (i, 0)),
                  pl.BlockSpec((tm, 2) + ROW_REC, lambda i: (i + off, 0, 0, 0, 0))],
        out_specs=pl.BlockSpec((tm, D_MODEL), lambda i: (i, 0)),
        out_shape=jax.ShapeDtypeStruct((n, D_MODEL), F32),
        compiler_params=_params(("arbitrary",)),
        name="moe_add",
    )(x1, route, y2_pairs)


def kernel(x_prompt, x_sample, mem_prompt, cache_k, cache_v, cache_mem_k, cache_mem_v, state_conv, page_table, norm1, w_in, conv_w, sb_bias, mem_norm, w_mem_kv, q_norm, k_norm, w_branch, w_o, norm2, w_group, b_group, w_router, b_router, w1, w3, w2):
    nb, seq, _ = x_prompt.shape
    ns, dec_seq, _ = x_sample.shape
    assert dec_seq == 1 and seq % SB_TQ == 0 and SB_TQ % (2 * SB_TK) == 0
    n_mem = mem_prompt.shape[1]
    sbw = SB_HEADS * SB_HEAD_DIM
    xw = X_HEADS * X_HEAD_DIM
    past = page_table.shape[1] * PAGE_SIZE

    assert w_in.shape[1] == _W_MAIN and _C_QX == _C_K + 2 * sbw and _C_GL == _C_QX + xw
    w_main = w_in.astype(BF16)
    w_kvt = kv_weight_t(w_in)
    wb = w_branch.astype(BF16)
    wo_bf = w_o.astype(BF16)
    w_route = jnp.zeros((D_MODEL, LANES), F32)
    w_route = w_route.at[:, :N_EXPERTS].set(w_router).at[:, N_EXPERTS:N_EXPERTS + N_GROUPS].set(w_group)
    b_route = jnp.zeros((1, LANES), F32)
    b_route = b_route.at[0, :N_EXPERTS].set(b_router).at[0, N_EXPERTS:N_EXPERTS + N_GROUPS].set(b_group)
    w_route = w_route.astype(BF16)

    xp = x_prompt.reshape(nb * seq, D_MODEL)
    mk_p, mv_p = memkv(mem_prompt.reshape(nb * n_mem, D_MODEL), mem_norm, w_mem_kv.astype(BF16), k_norm, n_mem)
    (kt_p, vt_p, q_p, qx_p, g1_p, g2_p, mp_p, tail_p, ktb_p, vtb_p) = inproj(
        xp, norm1, w_main, w_kvt, wb[0], conv_w, q_norm, seq, 512)
    ysb_p = sb_prompt(q_p, ktb_p, vtb_p, sb_bias, seq)
    yx_p = xattn_prompt(qx_p, mk_p.astype(BF16), mv_p.astype(BF16), seq, n_mem, 512)

    xs = x_sample.reshape(ns, D_MODEL)
    (kt_s, vt_s, q_s, qx_s, g1_s, g2_s, mp_s, u_s) = inproj(
        xs, norm1, w_main, w_kvt, wb[0], conv_w, q_norm, ns, ns,
        conv_state=(state_conv[:, 0], state_conv[:, 1]))
    ck_t = cache_k.transpose(0, 2, 3, 1).reshape(cache_k.shape[0], sbw, PAGE_SIZE)
    cv_t = cache_v.transpose(0, 2, 3, 1).reshape(cache_v.shape[0], sbw, PAGE_SIZE)
    ysb_s = sb_sample(q_s.astype(F32), kt_s[0], vt_s[0], ck_t, cv_t, page_table, sb_bias,
                      new_key_visible=(past < past))
    yx_s = xattn_sample(qx_s, cache_mem_k, cache_mem_v)

    cnt0 = jnp.zeros((1, LANES), F32)
    x1_p, h_p, route_p, rt_p, cnt1 = merge(xp, mp_p, ysb_p, yx_p, g1_p, g2_p, wb[1], wb[2], wo_bf, norm2,
                                           w_route, b_route, cnt0, 512)
    x1_s, h_s, route_s, rt_s, cnt2 = merge(xs, mp_s, ysb_s, yx_s, g1_s, g2_s, wb[1], wb[2], wo_bf, norm2,
                                           w_route, b_route, cnt1, ns)

    n_tok = nb * seq + ns
    n_slots = 2 * n_tok
    assert n_slots < (1 << 16)
    nblk = -(-(n_slots + N_EXPERTS * (MOE_BM - 1)) // MOE_BM)
    n_pad = nblk * MOE_BM - n_slots
    assert n_pad < (1 << 16)
    counts = cnt2[0, :N_EXPERTS].astype(I32)
    padded = (counts + MOE_BM - 1) // MOE_BM * MOE_BM
    ends = jnp.cumsum(padded)
    n_used = (ends[-1] // MOE_BM).astype(I32).reshape(1)
    blk_start = jnp.arange(nblk, dtype=I32) * MOE_BM
    block_e = jnp.sum(blk_start[:, None] >= ends[None, :], axis=1).astype(I32)
    block_e = jnp.minimum(block_e, block_e[jnp.maximum(n_used[0] - 1, 0)])

    e_all = jnp.concatenate([rt_p[_R_E1:_R_E2 + 1], rt_s[_R_E1:_R_E2 + 1]], axis=1).astype(I32)
    slot_id = 2 * jnp.arange(n_tok, dtype=I32)[None, :] + jnp.arange(2, dtype=I32)[:, None]
    key_real = (e_all << 17) | slot_id
    pad_id = jnp.arange(n_pad, dtype=I32)
    pad_e = jnp.sum(pad_id[:, None] >= jnp.cumsum(padded - counts)[None, :], axis=1).astype(I32)
    key_pad = (pad_e << 17) | (1 << 16) | pad_id
    keys = jnp.sort(jnp.concatenate([key_real.reshape(-1), key_pad]))
    pos = jnp.arange(nblk * MOE_BM, dtype=I32)
    pad_row = n_slots + ((pos // MOE_BM) & 1) * MOE_BM + pos % MOE_BM
    slots = jnp.where((keys >> 16) & 1 == 1, pad_row, keys & 0xFFFF).reshape(nblk, 1, MOE_BM)

    h_all = jnp.concatenate([h_p, h_s], axis=0)
    y2 = experts(h_all, slots, block_e, n_used, w1, w3, w2)
    y2_pairs = y2.reshape((n_tok + MOE_BM, 2) + ROW_REC)
    y_p = moe_add(x1_p, route_p, y2_pairs, 0, 512)
    y_s = moe_add(x1_s, route_s, y2_pairs, nb * seq, ns)

    def heads_last(t, b, s):
        return t.reshape(b, SB_HEADS, SB_HEAD_DIM, s).transpose(0, 3, 1, 2)

    conv_prompt = tail_p[:, SUBLANES - 2:, :]
    conv_sample = jnp.stack([state_conv[:, 1], u_s[0]], axis=1)
    return (y_p.reshape(nb, seq, D_MODEL), y_s.reshape(ns, 1, D_MODEL),
            heads_last(kt_p, nb, seq), heads_last(vt_p, nb, seq), conv_prompt,
            mk_p.reshape(nb, n_mem, X_HEADS, X_HEAD_DIM), mv_p.reshape(nb, n_mem, X_HEADS, X_HEAD_DIM),
            heads_last(kt_s, 1, ns).reshape(ns, 1, SB_HEADS, SB_HEAD_DIM),
            heads_last(vt_s, 1, ns).reshape(ns, 1, SB_HEADS, SB_HEAD_DIM),
            conv_sample)
```

```python
import functools

import jax
import jax.numpy as jnp
from jax import lax
from jax.experimental import pallas as pl
from jax.experimental.pallas import tpu as pltpu

F32 = jnp.float32
BF16 = jnp.bfloat16
I32 = jnp.int32

EPS = 1e-6
D_MODEL = 1024
BRANCH_W = 512
SB_HEADS = 8
SB_HEAD_DIM = 64
X_HEADS = 4
X_HEAD_DIM = 128
N_MEM = 256
N_GROUPS = 4
EXPERTS_PER_GROUP = 8
N_EXPERTS = N_GROUPS * EXPERTS_PER_GROUP
D_EXPERT = 512
PAGE_SIZE = 128

LANES = 128
SUBLANES = 8
VMEM_LIMIT = 56 * 1024 * 1024

SB_TK = 256
SB_TQ = 4 * SB_TK
MOE_BM = 256
XS_TOKENS = 4
ROW_DMA_UNROLL = 8


def _dot(a, b):
    return jnp.dot(a, b, preferred_element_type=F32)


def _dot_nt(a, b):
    return lax.dot_general(a, b, (((1,), (1,)), ((), ())), preferred_element_type=F32)


def _rms(x, g):
    return x * lax.rsqrt(jnp.mean(x * x, axis=-1, keepdims=True) + EPS) * g


def _softplus(z):
    return jnp.maximum(z, 0.0) + jnp.log(1.0 + jnp.exp(-jnp.abs(z)))


ROW_CHUNKS = D_MODEL // LANES
ROW_REC = (ROW_CHUNKS, LANES)


def _to_row_records(ref, x):
    for c in range(ROW_CHUNKS):
        ref[:, c, :] = x[:, c * LANES:(c + 1) * LANES]


def _from_row_records(ref, *mid):
    return jnp.concatenate([ref[(slice(None),) + mid + (c, slice(None))] for c in range(ROW_CHUNKS)], axis=1)


def _params(sem, vmem=VMEM_LIMIT):
    return pltpu.CompilerParams(dimension_semantics=sem, vmem_limit_bytes=vmem)


def _const_spec(shape):
    nd = len(shape)
    return pl.BlockSpec(shape, lambda *_: (0,) * nd, pipeline_mode=pl.Buffered(1))


def _memkv_kernel(mem_ref, g_ref, w_ref, kn_ref, mk_ref, mv_ref):
    xn = _rms(mem_ref[...], g_ref[...]).astype(BF16)
    kv = _dot(xn, w_ref[...])
    xw = X_HEADS * X_HEAD_DIM
    for h in range(X_HEADS):
        sl = slice(h * X_HEAD_DIM, (h + 1) * X_HEAD_DIM)
        mk_ref[:, sl] = _rms(kv[:, sl], kn_ref[...])
    mv_ref[...] = kv[:, xw:]


def memkv(mem2d, mem_norm, w_mem_kv_bf, k_norm, n_mem):
    n = mem2d.shape[0]
    xw = X_HEADS * X_HEAD_DIM
    return pl.pallas_call(
        _memkv_kernel,
        grid=(n // n_mem,),
        in_specs=[pl.BlockSpec((n_mem, D_MODEL), lambda i: (i, 0)),
                  _const_spec((1, D_MODEL)),
                  _const_spec((D_MODEL, 2 * xw)),
                  _const_spec((1, X_HEAD_DIM))],
        out_specs=[pl.BlockSpec((n_mem, xw), lambda i: (i, 0)),
                   pl.BlockSpec((n_mem, xw), lambda i: (i, 0))],
        out_shape=[jax.ShapeDtypeStruct((n, xw), F32), jax.ShapeDtypeStruct((n, xw), F32)],
        compiler_params=_params(("arbitrary",)),
        name="memkv",
    )(mem2d, mem_norm.reshape(1, -1), w_mem_kv_bf, k_norm.reshape(1, -1))


_C_UIN, _C_CG, _C_BG, _C_Q, _C_K, _C_QX, _C_GL = 0, 512, 1024, 1536, 2048, 3072, 3584
_W_MAIN = _C_GL + 3 * D_MODEL


def _kvt_kernel(w_ref, o_ref):
    o_ref[...] = w_ref[...].T.astype(BF16)


def kv_weight_t(w_in):
    kvw = 2 * SB_HEADS * SB_HEAD_DIM
    assert _C_K % kvw == 0
    return pl.pallas_call(
        _kvt_kernel, grid=(1,),
        in_specs=[pl.BlockSpec((D_MODEL, kvw), lambda i: (0, _C_K // kvw))],
        out_specs=pl.BlockSpec((kvw, D_MODEL), lambda i: (0, 0)),
        out_shape=jax.ShapeDtypeStruct((kvw, D_MODEL), BF16),
        compiler_params=_params(("arbitrary",)),
        name="kv_weight_t",
    )(w_in)


def _inproj_kernel(carry_conv, tiles_per_seq, tm, *refs):
    if carry_conv:
        (x_ref, n1_ref, wm_ref, wkvt_ref, wb0_ref, cw_ref, qn_ref,
         kt_ref, vt_ref, q_ref, qx_ref, g1_ref, g2_ref, mp_ref, tail_ref, ktb_ref, vtb_ref,
         carry_ref) = refs
    else:
        (x_ref, n1_ref, wm_ref, wkvt_ref, wb0_ref, cw_ref, qn_ref, um2_ref, um1_ref,
         kt_ref, vt_ref, q_ref, qx_ref, g1_ref, g2_ref, mp_ref, tail_ref) = refs

    xn = _rms(x_ref[...], n1_ref[...]).astype(BF16)

    def proj(c0, width):
        return _dot(xn, wm_ref[:, c0:c0 + width])

    kvt = _dot_nt(wkvt_ref[...], xn)
    sbw = SB_HEADS * SB_HEAD_DIM
    kt_ref[...] = kvt[:sbw]
    vt_ref[...] = kvt[sbw:]
    if carry_conv:
        for c in range(tm // SB_TK):
            ktb_ref[c] = kvt[:sbw, c * SB_TK:(c + 1) * SB_TK].astype(BF16)
            vtb_ref[c] = kvt[sbw:, c * SB_TK:(c + 1) * SB_TK].astype(BF16)

    q_ref[...] = (proj(_C_Q, sbw) * (SB_HEAD_DIM ** -0.5)).astype(BF16)

    qx = proj(_C_QX, X_HEADS * X_HEAD_DIM)
    for h in range(X_HEADS):
        sl = slice(h * X_HEAD_DIM, (h + 1) * X_HEAD_DIM)
        qx_ref[:, sl] = _rms(qx[:, sl], qn_ref[...]).astype(BF16)

    u = proj(_C_CG, BRANCH_W) * proj(_C_UIN, BRANCH_W)
    if carry_conv:
        @pl.when(pl.program_id(0) % tiles_per_seq == 0)
        def _():
            carry_ref[...] = jnp.zeros_like(carry_ref)
        ext = jnp.concatenate([carry_ref[...], u], axis=0)
        um1 = pltpu.roll(ext, 1, 0)[SUBLANES:]
        um2 = pltpu.roll(ext, 2, 0)[SUBLANES:]
        carry_ref[...] = u[tm - SUBLANES:]
        tail_ref[...] = u[tm - SUBLANES:]
    else:
        um1 = um1_ref[...]
        um2 = um2_ref[...]
        tail_ref[...] = u
    cw = cw_ref[...]
    y_conv = proj(_C_BG, BRANCH_W) * (um2 * cw[0:1] + um1 * cw[1:2] + u * cw[2:3])

    g0 = jax.nn.sigmoid(proj(_C_GL, D_MODEL))
    mp_ref[...] = g0 * _dot(y_conv.astype(BF16), wb0_ref[...])
    g1_ref[...] = jax.nn.sigmoid(proj(_C_GL + D_MODEL, D_MODEL))
    g2_ref[...] = jax.nn.sigmoid(proj(_C_GL + 2 * D_MODEL, D_MODEL))


def inproj(x2d, norm1, w_main_bf, w_kvt_bf, wb0_bf, conv_w, q_norm, seq_len, tm, conv_state=None):
    n = x2d.shape[0]
    carry = conv_state is None
    nseq = n // seq_len if carry else 1
    slen = seq_len if carry else n
    tps = slen // tm
    sbw = SB_HEADS * SB_HEAD_DIM
    xw = X_HEADS * X_HEAD_DIM
    nkb = max(tm // SB_TK, 1)
    tail_rows = SUBLANES if carry else tm

    row = lambda w: pl.BlockSpec((tm, w), lambda i: (i, 0))
    kt_spec = pl.BlockSpec((None, sbw, tm), lambda i: (i // tps, 0, i % tps))
    ktb_spec = pl.BlockSpec((None, nkb, sbw, SB_TK), lambda i: (i // tps, i % tps, 0, 0))
    in_specs = [row(D_MODEL), _const_spec((1, D_MODEL)), _const_spec((D_MODEL, _W_MAIN)),
                _const_spec((2 * sbw, D_MODEL)), _const_spec((BRANCH_W, D_MODEL)),
                _const_spec((3, BRANCH_W)), _const_spec((1, X_HEAD_DIM))]
    args = [x2d, norm1.reshape(1, -1), w_main_bf, w_kvt_bf, wb0_bf, conv_w, q_norm.reshape(1, -1)]
    scratch = []
    if carry:
        scratch = [pltpu.VMEM((SUBLANES, BRANCH_W), F32)]
    else:
        in_specs += [row(BRANCH_W), row(BRANCH_W)]
        args += list(conv_state)
    out_specs = [kt_spec, kt_spec, row(sbw), row(xw),
                 row(D_MODEL), row(D_MODEL), row(D_MODEL),
                 pl.BlockSpec((None, tail_rows, BRANCH_W), lambda i: (i // tps, 0, 0))]
    out_shape = [jax.ShapeDtypeStruct((nseq, sbw, slen), F32), jax.ShapeDtypeStruct((nseq, sbw, slen), F32),
                 jax.ShapeDtypeStruct((n, sbw), BF16), jax.ShapeDtypeStruct((n, xw), BF16),
                 jax.ShapeDtypeStruct((n, D_MODEL), F32), jax.ShapeDtypeStruct((n, D_MODEL), F32),
                 jax.ShapeDtypeStruct((n, D_MODEL), F32),
                 jax.ShapeDtypeStruct((nseq, tail_rows, BRANCH_W), F32)]
    if carry:
        out_specs += [ktb_spec, ktb_spec]
        out_shape += [jax.ShapeDtypeStruct((nseq, slen // SB_TK, sbw, SB_TK), BF16)] * 2
    return pl.pallas_call(
        functools.partial(_inproj_kernel, carry, tps, tm),
        grid=(n // tm,),
        in_specs=in_specs, out_specs=out_specs, out_shape=out_shape,
        scratch_shapes=scratch,
        compiler_params=_params(("arbitrary",)),
        name="inproj_prompt" if carry else "inproj_sample",
    )(*args)


def _sbp_kernel(bias_ref, q_ref, kt_ref, vt_ref, o_ref, acc_ref, car_ref):
    pair = pl.program_id(1)
    qi = pl.program_id(2)
    tq, tk = SB_TQ, SB_TK
    lane = lax.broadcasted_iota(I32, (tq, LANES), 1)
    q = q_ref[...]
    zero = jnp.zeros_like(q)
    qh = (jnp.where(lane < SB_HEAD_DIM, q, zero), jnp.where(lane >= SB_HEAD_DIM, q, zero))
    r = lax.broadcasted_iota(I32, (tk, tk), 0)
    c = lax.broadcasted_iota(I32, (tk, tk), 1)
    tri = (r >= c).astype(BF16)
    qrow = lax.broadcasted_iota(I32, (tq, tk), 0)
    kcol = lax.broadcasted_iota(I32, (tq, tk), 1)
    acc_ref[...] = jnp.zeros_like(acc_ref)
    car_ref[...] = jnp.zeros_like(car_ref)

    def block(j, diag_sub):
        kt = kt_ref[j]
        vt = vt_ref[j]
        rs = slice(0, tq)
        if diag_sub is not None:
            rs = slice(diag_sub * tk, tq)
            causal = (kcol + diag_sub * tk < qrow)[rs]
        for h in range(2):
            z = _dot(qh[h][rs], kt) + bias_ref[2 * pair + h]
            sp = _softplus(z)
            if diag_sub is not None:
                sp = jnp.where(causal, sp, 0.0)
            car = car_ref[h, rs]
            cum = _dot(sp.astype(BF16), tri)
            a = jnp.exp(z - (cum + jnp.concatenate([car] * (tk // LANES), axis=1)))
            if diag_sub is not None:
                a = jnp.where(causal, a, 0.0)
            acc_ref[h, rs] += _dot_nt(a.astype(BF16), vt)
            car_ref[h, rs] = car + jnp.broadcast_to(cum[:, 0:1], car.shape)

    nsub = tq // tk
    for s in reversed(range(nsub)):
        block(qi * nsub + s, s)

    def body(t, carry):
        block(qi * nsub - 1 - 2 * t, None)
        block(qi * nsub - 2 - 2 * t, None)
        return carry

    lax.fori_loop(0, qi * (nsub // 2), body, 0)
    o_ref[...] = jnp.where(lane < SB_HEAD_DIM, acc_ref[0], acc_ref[1]).astype(o_ref.dtype)


def sb_prompt(q_bf, ktb, vtb, sb_bias, seq_len):
    n = q_bf.shape[0]
    nseq = n // seq_len
    nq = seq_len // SB_TQ
    nk = seq_len // SB_TK
    npair = SB_HEADS // 2
    grid_spec = pltpu.PrefetchScalarGridSpec(
        num_scalar_prefetch=0,
        grid=(nseq, npair, nq),
        in_specs=[pl.BlockSpec(memory_space=pltpu.SMEM),
                  pl.BlockSpec((SB_TQ, LANES), lambda b, p, i: (b * nq + i, p)),
                  pl.BlockSpec((None, nk, LANES, SB_TK), lambda b, p, i: (b, 0, p, 0)),
                  pl.BlockSpec((None, nk, LANES, SB_TK), lambda b, p, i: (b, 0, p, 0))],
        out_specs=pl.BlockSpec((SB_TQ, LANES), lambda b, p, i: (b * nq + i, p)),
        scratch_shapes=[pltpu.VMEM((2, SB_TQ, LANES), F32), pltpu.VMEM((2, SB_TQ, LANES), F32)],
    )
    return pl.pallas_call(
        _sbp_kernel, grid_spec=grid_spec,
        out_shape=jax.ShapeDtypeStruct((n, SB_HEADS * SB_HEAD_DIM), BF16),
        compiler_params=_params(("arbitrary", "arbitrary", "arbitrary")),
        name="sb_prompt",
    )(sb_bias.astype(F32), q_bf, ktb, vtb)


def _xp_kernel(qx_ref, mk_ref, mv_ref, o_ref):
    for h in range(X_HEADS):
        sl = slice(h * X_HEAD_DIM, (h + 1) * X_HEAD_DIM)
        s = _dot_nt(qx_ref[:, sl], mk_ref[:, sl]) * (X_HEAD_DIM ** -0.5)
        e = jnp.exp(s - jnp.max(s, axis=-1, keepdims=True))
        p = e / jnp.sum(e, axis=-1, keepdims=True)
        o_ref[:, sl] = _dot(p.astype(BF16), mv_ref[:, sl]).astype(o_ref.dtype)


def xattn_prompt(qx_bf, mk_bf, mv_bf, seq_len, n_mem, tm):
    n = qx_bf.shape[0]
    tps = seq_len // tm
    xw = X_HEADS * X_HEAD_DIM
    return pl.pallas_call(
        _xp_kernel, grid=(n // tm,),
        in_specs=[pl.BlockSpec((tm, xw), lambda i: (i, 0)),
                  pl.BlockSpec((n_mem, xw), lambda i: (i // tps, 0)),
                  pl.BlockSpec((n_mem, xw), lambda i: (i // tps, 0))],
        out_specs=pl.BlockSpec((tm, xw), lambda i: (i, 0)),
        out_shape=jax.ShapeDtypeStruct((n, xw), BF16),
        compiler_params=_params(("arbitrary",)),
        name="xattn_prompt",
    )(qx_bf, mk_bf, mv_bf)


def _split_bf16(x):
    hi = x.astype(BF16)
    return hi, (x - hi.astype(F32)).astype(BF16)


def _sbs_kernel(new_key_visible, npages, pt_ref, bias_ref, qbd_ref, kn_ref, vn_ref, *refs):
    del pt_ref
    k_refs, v_refs, o_ref = refs[:npages], refs[npages:2 * npages], refs[2 * npages]
    b = pl.program_id(0)
    psz = PAGE_SIZE
    sbw = SB_HEADS * SB_HEAD_DIM
    rows = npages * SB_HEADS
    head_bits = SB_HEADS.bit_length() - 1
    qbd = qbd_ref[0]
    bias = bias_ref[...]

    mine = lax.broadcasted_iota(I32, (SB_HEADS, kn_ref.shape[1]), 1) == b
    zs = _dot(qbd, kn_ref[...].astype(BF16)) + bias[:, 0:1]
    seen = jnp.logical_and(mine, new_key_visible)
    sp0 = jnp.where(seen, _softplus(zs), 0.0)
    a0 = jnp.where(seen, jnp.exp(zs - sp0), 0.0)
    sp0_tot = jnp.sum(sp0, axis=-1, keepdims=True)
    a0_rows = jnp.concatenate(
        [jnp.broadcast_to(a0[h:h + 1], (SB_HEAD_DIM, a0.shape[1])) for h in range(SB_HEADS)], axis=0)
    new = jnp.sum(a0_rows * vn_ref[...], axis=-1, keepdims=True)

    z = jnp.concatenate([_dot(qbd, k_refs[j][...].astype(BF16)) + bias for j in range(npages)], axis=0)
    sp = _softplus(z)
    r = lax.broadcasted_iota(I32, (psz, psz), 0)
    c = lax.broadcasted_iota(I32, (psz, psz), 1)
    tri = (r >= c).astype(BF16)
    hi, lo = _split_bf16(sp)
    cum = _dot(hi, tri) + _dot(lo, tri)
    tot = jnp.broadcast_to(cum[:, 0:1], (rows, psz))
    rr = lax.broadcasted_iota(I32, (rows, rows), 0)
    cc = lax.broadcasted_iota(I32, (rows, rows), 1)
    later = jnp.logical_and((cc >> head_bits) > (rr >> head_bits),
                            (cc & (SB_HEADS - 1)) == (rr & (SB_HEADS - 1))).astype(BF16)
    th, tl = _split_bf16(tot)
    carry = _dot(later, th) + _dot(later, tl) + jnp.concatenate([sp0_tot] * npages, axis=0)
    a = jnp.exp(z - cum - carry)

    accs = []
    for h in range(SB_HEADS):
        hs = slice(h * SB_HEAD_DIM, (h + 1) * SB_HEAD_DIM)
        acc = jnp.zeros((SB_HEAD_DIM, psz), F32)
        for j in range(npages):
            rj = j * SB_HEADS + h
            acc = acc + jnp.broadcast_to(a[rj:rj + 1], (SB_HEAD_DIM, psz)) * v_refs[j][hs, :]
        accs.append(acc)
    acc = jnp.concatenate(accs, axis=0)
    lane0 = lax.broadcasted_iota(I32, (sbw, psz), 1) == 0
    acc = acc + jnp.where(lane0, new, 0.0)
    hi2, lo2 = _split_bf16(acc)
    ones = jnp.ones((SUBLANES, psz), BF16)
    y = _dot_nt(ones, hi2) + _dot_nt(ones, lo2)
    o_ref[pl.ds(b, 1), :] = y[0:1]


def sb_sample(q_s, kn_t, vn_t, cache_k_t, cache_v_t, page_table, sb_bias, new_key_visible):
    nb, npages = page_table.shape
    sbw = SB_HEADS * SB_HEAD_DIM
    head_of_col = jnp.arange(sbw) // SB_HEAD_DIM
    sel = head_of_col[None, :] == jnp.arange(SB_HEADS)[:, None]
    qbd = jnp.where(sel[None], q_s[:, None, :], 0.0).astype(BF16)
    assert PAGE_SIZE == LANES
    bias = jnp.broadcast_to(sb_bias.astype(F32)[:, None], (SB_HEADS, LANES))
    page = lambda j: pl.BlockSpec((None, sbw, PAGE_SIZE), lambda b, pt: (pt[b, j], 0, 0))
    grid_spec = pltpu.PrefetchScalarGridSpec(
        num_scalar_prefetch=1,
        grid=(nb,),
        in_specs=[pl.BlockSpec((SB_HEADS, LANES), lambda b, pt: (0, 0)),
                  pl.BlockSpec((1, SB_HEADS, sbw), lambda b, pt: (b, 0, 0)),
                  pl.BlockSpec((sbw, nb), lambda b, pt: (0, 0)),
                  pl.BlockSpec((sbw, nb), lambda b, pt: (0, 0))]
                 + [page(j) for j in range(npages)] * 2,
        out_specs=pl.BlockSpec((nb, sbw), lambda b, pt: (0, 0)),
    )
    return pl.pallas_call(
        functools.partial(_sbs_kernel, new_key_visible, npages), grid_spec=grid_spec,
        out_shape=jax.ShapeDtypeStruct((nb, sbw), F32),
        compiler_params=_params(("arbitrary",)),
        name="sb_sample",
    )(page_table, bias, qbd, kn_t, vn_t, *([cache_k_t] * npages), *([cache_v_t] * npages))


def _xs_kernel(qx_ref, mk_ref, mv_ref, o_ref):
    step = pl.program_id(0)
    rows = N_MEM * X_HEADS
    own = ((lax.broadcasted_iota(I32, (SUBLANES, rows), 1) & (X_HEADS - 1))
           == lax.broadcasted_iota(I32, (SUBLANES, rows), 0))
    for t in range(XS_TOKENS):
        mk = mk_ref[t].astype(BF16)
        mv = mv_ref[t].astype(BF16)
        s = _dot_nt(qx_ref[t], mk) * (X_HEAD_DIM ** -0.5)
        s = jnp.where(own, s, -jnp.inf)
        m = jnp.max(s, axis=-1, keepdims=True)
        m = jnp.where(m == -jnp.inf, 0.0, m)
        e = jnp.exp(s - m)
        den = jnp.sum(e, axis=-1, keepdims=True)
        p = e / jnp.where(den == 0.0, 1.0, den)
        y = _dot(p.astype(BF16), mv)
        o_ref[pl.ds(step * XS_TOKENS + t, 1)] = y[None, 0:X_HEADS]


def xattn_sample(qx_bf, cache_mem_k, cache_mem_v):
    nb = qx_bf.shape[0]
    rows = N_MEM * X_HEADS
    mk = cache_mem_k.reshape(nb, rows, X_HEAD_DIM)
    mv = cache_mem_v.reshape(nb, rows, X_HEAD_DIM)
    qx = qx_bf.reshape(nb, X_HEADS, X_HEAD_DIM)
    qx = jnp.concatenate([qx, jnp.zeros((nb, SUBLANES - X_HEADS, X_HEAD_DIM), BF16)], axis=1)
    assert nb % XS_TOKENS == 0
    out = pl.pallas_call(
        _xs_kernel, grid=(nb // XS_TOKENS,),
        in_specs=[pl.BlockSpec((XS_TOKENS, SUBLANES, X_HEAD_DIM), lambda b: (b, 0, 0)),
                  pl.BlockSpec((XS_TOKENS, rows, X_HEAD_DIM), lambda b: (b, 0, 0)),
                  pl.BlockSpec((XS_TOKENS, rows, X_HEAD_DIM), lambda b: (b, 0, 0))],
        out_specs=pl.BlockSpec((nb, X_HEADS, X_HEAD_DIM), lambda b: (0, 0, 0)),
        out_shape=jax.ShapeDtypeStruct((nb, X_HEADS, X_HEAD_DIM), F32),
        compiler_params=_params(("arbitrary",)),
        name="xattn_sample",
    )(qx, mk, mv)
    return out.reshape(nb, X_HEADS * X_HEAD_DIM)


_R_E1, _R_E2, _R_R1, _R_R2, _R_G1, _R_G2 = 0, 1, 2, 3, 4, 5
_GROUP_LANE0 = N_EXPERTS


def _merge_kernel(tm, x_ref, mp_ref, ysb_ref, yx_ref, g1_ref, g2_ref, wb1_ref, wb2_ref, wo_ref,
                  n2_ref, wr_ref, br_ref, cin_ref, x1_ref, h_ref, route_ref, route_t_ref, cout_ref, cnt_ref):
    @pl.when(pl.program_id(0) == 0)
    def _():
        cnt_ref[...] = cin_ref[...]

    m = (mp_ref[...] + g1_ref[...] * _dot(ysb_ref[...].astype(BF16), wb1_ref[...])
         + g2_ref[...] * _dot(yx_ref[...].astype(BF16), wb2_ref[...]))
    x1 = x_ref[...] + _dot(m.astype(BF16), wo_ref[...])
    x1_ref[...] = x1
    hn = _rms(x1, n2_ref[...])
    _to_row_records(h_ref, hn)
    logits = _dot(hn.astype(BF16), wr_ref[...]) + br_ref[...]

    lane = lax.broadcasted_iota(I32, (tm, LANES), 1).astype(F32)
    big = float(LANES)
    ninf = -jnp.inf

    def first_argmax(v):
        mx = jnp.max(v, axis=-1, keepdims=True)
        idx = jnp.min(jnp.where(v == mx, lane, big), axis=-1, keepdims=True)
        return mx, idx

    gl = jnp.where((lane >= _GROUP_LANE0) & (lane < _GROUP_LANE0 + N_GROUPS), logits, ninf)
    gmax, glane = first_argmax(gl)
    gprob = 1.0 / jnp.sum(jnp.exp(gl - gmax), axis=-1, keepdims=True)
    e0 = (glane - _GROUP_LANE0) * EXPERTS_PER_GROUP
    el = jnp.where((lane >= e0) & (lane < e0 + EXPERTS_PER_GROUP), logits, ninf)
    m1, i1 = first_argmax(el)
    m2, i2 = first_argmax(jnp.where(lane == i1, ninf, el))
    e2 = jnp.exp(m2 - m1)
    gate1 = gprob / (1.0 + e2)
    gate2 = gprob * e2 / (1.0 + e2)

    oh1 = (lane == i1).astype(F32)
    oh2 = (lane == i2).astype(F32)
    cnt = oh1 + oh2
    r = lax.broadcasted_iota(I32, (tm, tm), 0)
    c = lax.broadcasted_iota(I32, (tm, tm), 1)
    before = _dot((c < r).astype(BF16), cnt.astype(BF16)) + cnt_ref[...]
    rank1 = jnp.sum(oh1 * before, axis=-1, keepdims=True)
    rank2 = jnp.sum(oh2 * before, axis=-1, keepdims=True)
    cnt_ref[...] = cnt_ref[...] + jnp.sum(cnt, axis=0, keepdims=True)
    cout_ref[...] = cnt_ref[...]

    rec = jnp.zeros((tm, LANES), F32)
    for ln, val in ((_R_E1, i1.astype(F32)), (_R_E2, i2.astype(F32)), (_R_R1, rank1), (_R_R2, rank2),
                    (_R_G1, gate1), (_R_G2, gate2)):
        rec = jnp.where(lane == ln, val, rec)
    route_ref[...] = rec
    route_t_ref[...] = rec.T[0:SUBLANES]


def merge(x2d, mp, ysb, yx, g1, g2, wb1_bf, wb2_bf, wo_bf, norm2, w_route_bf, b_route, cnt_in, tm):
    n = x2d.shape[0]
    row = lambda w: pl.BlockSpec((tm, w), lambda i: (i, 0))
    return pl.pallas_call(
        functools.partial(_merge_kernel, tm), grid=(n // tm,),
        in_specs=[row(D_MODEL), row(D_MODEL), row(BRANCH_W), row(BRANCH_W), row(D_MODEL), row(D_MODEL),
                  _const_spec((BRANCH_W, D_MODEL)), _const_spec((BRANCH_W, D_MODEL)),
                  _const_spec((D_MODEL, D_MODEL)), _const_spec((1, D_MODEL)),
                  _const_spec((D_MODEL, LANES)), _const_spec((1, LANES)), _const_spec((1, LANES))],
        out_specs=[row(D_MODEL), pl.BlockSpec((tm,) + ROW_REC, lambda i: (i, 0, 0)), row(LANES),
                   pl.BlockSpec((SUBLANES, tm), lambda i: (0, i)),
                   pl.BlockSpec((1, LANES), lambda i: (0, 0))],
        out_shape=[jax.ShapeDtypeStruct((n, D_MODEL), F32), jax.ShapeDtypeStruct((n,) + ROW_REC, F32),
                   jax.ShapeDtypeStruct((n, LANES), F32), jax.ShapeDtypeStruct((SUBLANES, n), F32),
                   jax.ShapeDtypeStruct((1, LANES), F32)],
        scratch_shapes=[pltpu.VMEM((1, LANES), F32)],
        compiler_params=_params(("arbitrary",)),
        name="merge",
    )(x2d, mp, ysb, yx, g1, g2, wb1_bf, wb2_bf, wo_bf, norm2.reshape(1, -1), w_route_bf, b_route, cnt_in)


def _experts_kernel(n_tok, nblk, be_ref, nu_ref, sl_cur_ref, sl_nxt_ref, h_ref, w1_ref, w3_ref, w2_ref,
                    y2_ref, xbuf, ybuf, zrec, sem_g, sem_s):
    del be_ref
    i = pl.program_id(0)
    nu = nu_ref[0]
    par = i & 1
    bm = MOE_BM

    def gather(sl_ref, p):
        def issue(k, carry):
            for prio in range(2):
                r = 2 * k + prio
                tok = jnp.minimum(lax.shift_right_logical(sl_ref[0, 0, r], 1), n_tok - 1)
                pltpu.make_async_copy(h_ref.at[pl.ds(tok, 1)], xbuf.at[p, pl.ds(r, 1)],
                                      sem_g.at[p]).start(priority=prio)
            return carry
        lax.fori_loop(0, bm // 2, issue, 0, unroll=ROW_DMA_UNROLL // 2)

    def scatter(p):
        def issue(k, carry):
            for prio in range(2):
                r = 2 * k + prio
                pltpu.make_async_copy(ybuf.at[p, pl.ds(r, 1)], y2_ref.at[pl.ds(sl_cur_ref[0, 0, r], 1)],
                                      sem_s.at[p]).start(priority=prio)
            return carry
        lax.fori_loop(0, bm // 2, issue, 0, unroll=ROW_DMA_UNROLL // 2)

    def wait_gather(p):
        pltpu.make_async_copy(xbuf.at[p], xbuf.at[p], sem_g.at[p]).wait()

    def wait_scatter(p):
        pltpu.make_async_copy(ybuf.at[p], ybuf.at[p], sem_s.at[p]).wait()

    @pl.when(i == 0)
    def _():
        zrec[...] = jnp.zeros(zrec.shape, F32)
        for d in range(2):
            fill = pltpu.make_async_copy(zrec, y2_ref.at[pl.ds(2 * n_tok + d * bm, bm)], sem_s.at[1])
            fill.start()
            fill.wait()
        gather(sl_cur_ref, 0)

    @pl.when(i + 1 < nu)
    def _():
        gather(sl_nxt_ref, 1 - par)

    @pl.when(i < nu)
    def _():
        wait_gather(par)
        xb = _from_row_records(xbuf.at[par]).astype(BF16)
        h1 = _dot(xb, w1_ref[...].astype(BF16))
        h3 = _dot(xb, w3_ref[...].astype(BF16))
        act = (h1 * jax.nn.sigmoid(h1)) * h3
        _to_row_records(ybuf.at[par], _dot(act.astype(BF16), w2_ref[...].astype(BF16)))
        scatter(par)

    @pl.when(jnp.logical_and(i >= 1, i - 1 < nu))
    def _():
        wait_scatter(1 - par)

    @pl.when(jnp.logical_and(i == nblk - 1, i < nu))
    def _():
        wait_scatter(par)


def experts(h_all, slots, block_e, n_used, w1, w3, w2):
    n_tok = h_all.shape[0]
    nblk = slots.shape[0]
    last = lambda nu: jnp.maximum(nu[0] - 1, 0)
    grid_spec = pltpu.PrefetchScalarGridSpec(
        num_scalar_prefetch=2,
        grid=(nblk,),
        in_specs=[pl.BlockSpec((1, 1, MOE_BM), lambda i, be, nu: (jnp.minimum(i, last(nu)), 0, 0),
                               memory_space=pltpu.SMEM),
                  pl.BlockSpec((1, 1, MOE_BM), lambda i, be, nu: (jnp.minimum(i + 1, last(nu)), 0, 0),
                               memory_space=pltpu.SMEM),
                  pl.BlockSpec(memory_space=pl.ANY),
                  pl.BlockSpec((None, D_MODEL, D_EXPERT), lambda i, be, nu: (be[i], 0, 0)),
                  pl.BlockSpec((None, D_MODEL, D_EXPERT), lambda i, be, nu: (be[i], 0, 0)),
                  pl.BlockSpec((None, D_EXPERT, D_MODEL), lambda i, be, nu: (be[i], 0, 0))],
        out_specs=pl.BlockSpec(memory_space=pl.ANY),
        scratch_shapes=[pltpu.VMEM((2, MOE_BM) + ROW_REC, F32),
                        pltpu.VMEM((2, MOE_BM) + ROW_REC, F32),
                        pltpu.VMEM((MOE_BM,) + ROW_REC, F32),
                        pltpu.SemaphoreType.DMA((2,)), pltpu.SemaphoreType.DMA((2,))],
    )
    return pl.pallas_call(
        functools.partial(_experts_kernel, n_tok, nblk), grid_spec=grid_spec,
        out_shape=jax.ShapeDtypeStruct((2 * n_tok + 2 * MOE_BM,) + ROW_REC, F32),
        compiler_params=_params(("arbitrary",)),
        name="experts",
    )(block_e, n_used, slots, slots, h_all, w1, w3, w2)


def _moe_add_kernel(x1_ref, route_ref, y2_ref, o_ref):
    route = route_ref[...]
    o_ref[...] = (x1_ref[...] + route[:, _R_G1:_R_G1 + 1] * _from_row_records(y2_ref, 0)
                  + route[:, _R_G2:_R_G2 + 1] * _from_row_records(y2_ref, 1))


def moe_add(x1, route, y2_pairs, row_off, tm):
    n = x1.shape[0]
    off = row_off // tm
    return pl.pallas_call(
        _moe_add_kernel, grid=(n // tm,),
        in_specs=[pl.BlockSpec((tm, D_MODEL), lambda i: (i, 0)),
                  pl.BlockSpec((tm, LANES), lambda i: (i, 0)),
                  pl.BlockSpec((tm, 2) + ROW_REC, lambda i: (i + off, 0, 0, 0))],
        out_specs=pl.BlockSpec((tm, D_MODEL), lambda i: (i, 0)),
        out_shape=jax.ShapeDtypeStruct((n, D_MODEL), F32),
        compiler_params=_params(("arbitrary",)),
        name="moe_add",
    )(x1, route, y2_pairs)


def kernel(x_prompt, x_sample, mem_prompt, cache_k, cache_v, cache_mem_k, cache_mem_v, state_conv, page_table, norm1, w_in, conv_w, sb_bias, mem_norm, w_mem_kv, q_norm, k_norm, w_branch, w_o, norm2, w_group, b_group, w_router, b_router, w1, w3, w2):
    nb, seq, _ = x_prompt.shape
    ns, dec_seq, _ = x_sample.shape
    assert dec_seq == 1 and seq % SB_TQ == 0 and SB_TQ % (2 * SB_TK) == 0
    n_mem = mem_prompt.shape[1]
    sbw = SB_HEADS * SB_HEAD_DIM
    xw = X_HEADS * X_HEAD_DIM
    past = page_table.shape[1] * PAGE_SIZE

    assert w_in.shape[1] == _W_MAIN and _C_QX == _C_K + 2 * sbw and _C_GL == _C_QX + xw
    w_main = w_in.astype(BF16)
    w_kvt = kv_weight_t(w_in)
    wb = w_branch.astype(BF16)
    wo_bf = w_o.astype(BF16)
    w_route = jnp.zeros((D_MODEL, LANES), F32)
    w_route = w_route.at[:, :N_EXPERTS].set(w_router).at[:, N_EXPERTS:N_EXPERTS + N_GROUPS].set(w_group)
    b_route = jnp.zeros((1, LANES), F32)
    b_route = b_route.at[0, :N_EXPERTS].set(b_router).at[0, N_EXPERTS:N_EXPERTS + N_GROUPS].set(b_group)
    w_route = w_route.astype(BF16)

    xp = x_prompt.reshape(nb * seq, D_MODEL)
    mk_p, mv_p = memkv(mem_prompt.reshape(nb * n_mem, D_MODEL), mem_norm, w_mem_kv.astype(BF16), k_norm, n_mem)
    (kt_p, vt_p, q_p, qx_p, g1_p, g2_p, mp_p, tail_p, ktb_p, vtb_p) = inproj(
        xp, norm1, w_main, w_kvt, wb[0], conv_w, q_norm, seq, 512)
    ysb_p = sb_prompt(q_p, ktb_p, vtb_p, sb_bias, seq)
    yx_p = xattn_prompt(qx_p, mk_p.astype(BF16), mv_p.astype(BF16), seq, n_mem, 512)

    xs = x_sample.reshape(ns, D_MODEL)
    (kt_s, vt_s, q_s, qx_s, g1_s, g2_s, mp_s, u_s) = inproj(
        xs, norm1, w_main, w_kvt, wb[0], conv_w, q_norm, ns, ns,
        conv_state=(state_conv[:, 0], state_conv[:, 1]))
    ck_t = cache_k.transpose(0, 2, 3, 1).reshape(cache_k.shape[0], sbw, PAGE_SIZE)
    cv_t = cache_v.transpose(0, 2, 3, 1).reshape(cache_v.shape[0], sbw, PAGE_SIZE)
    ysb_s = sb_sample(q_s.astype(F32), kt_s[0], vt_s[0], ck_t, cv_t, page_table, sb_bias,
                      new_key_visible=(past < past))
    yx_s = xattn_sample(qx_s, cache_mem_k, cache_mem_v)

    cnt0 = jnp.zeros((1, LANES), F32)
    x1_p, h_p, route_p, rt_p, cnt1 = merge(xp, mp_p, ysb_p, yx_p, g1_p, g2_p, wb[1], wb[2], wo_bf, norm2,
                                           w_route, b_route, cnt0, 512)
    x1_s, h_s, route_s, rt_s, cnt2 = merge(xs, mp_s, ysb_s, yx_s, g1_s, g2_s, wb[1], wb[2], wo_bf, norm2,
                                           w_route, b_route, cnt1, ns)

    n_tok = nb * seq + ns
    n_slots = 2 * n_tok
    assert n_slots < (1 << 16)
    nblk = -(-(n_slots + N_EXPERTS * (MOE_BM - 1)) // MOE_BM)
    n_pad = nblk * MOE_BM - n_slots
    assert n_pad < (1 << 16)
    counts = cnt2[0, :N_EXPERTS].astype(I32)
    padded = (counts + MOE_BM - 1) // MOE_BM * MOE_BM
    ends = jnp.cumsum(padded)
    n_used = (ends[-1] // MOE_BM).astype(I32).reshape(1)
    blk_start = jnp.arange(nblk, dtype=I32) * MOE_BM
    block_e = jnp.sum(blk_start[:, None] >= ends[None, :], axis=1).astype(I32)
    block_e = jnp.minimum(block_e, block_e[jnp.maximum(n_used[0] - 1, 0)])

    e_all = jnp.concatenate([rt_p[_R_E1:_R_E2 + 1], rt_s[_R_E1:_R_E2 + 1]], axis=1).astype(I32)
    slot_id = 2 * jnp.arange(n_tok, dtype=I32)[None, :] + jnp.arange(2, dtype=I32)[:, None]
    key_real = (e_all << 17) | slot_id
    pad_id = jnp.arange(n_pad, dtype=I32)
    pad_e = jnp.sum(pad_id[:, None] >= jnp.cumsum(padded - counts)[None, :], axis=1).astype(I32)
    key_pad = (pad_e << 17) | (1 << 16) | pad_id
    keys = jnp.sort(jnp.concatenate([key_real.reshape(-1), key_pad]))
    pos = jnp.arange(nblk * MOE_BM, dtype=I32)
    pad_row = n_slots + ((pos // MOE_BM) & 1) * MOE_BM + pos % MOE_BM
    slots = jnp.where((keys >> 16) & 1 == 1, pad_row, keys & 0xFFFF).reshape(nblk, 1, MOE_BM)

    h_all = jnp.concatenate([h_p, h_s], axis=0)
    y2 = experts(h_all, slots, block_e, n_used, w1, w3, w2)
    y2_pairs = y2.reshape((n_tok + MOE_BM, 2) + ROW_REC)
    y_p = moe_add(x1_p, route_p, y2_pairs, 0, 512)
    y_s = moe_add(x1_s, route_s, y2_pairs, nb * seq, ns)

    def heads_last(t, b, s):
        return t.reshape(b, SB_HEADS, SB_HEAD_DIM, s).transpose(0, 3, 1, 2)

    conv_prompt = tail_p[:, SUBLANES - 2:, :]
    conv_sample = jnp.stack([state_conv[:, 1], u_s[0]], axis=1)
    return (y_p.reshape(nb, seq, D_MODEL), y_s.reshape(ns, 1, D_MODEL),
            heads_last(kt_p, nb, seq), heads_last(vt_p, nb, seq), conv_prompt,
            mk_p.reshape(nb, n_mem, X_HEADS, X_HEAD_DIM), mv_p.reshape(nb, n_mem, X_HEADS, X_HEAD_DIM),
            heads_last(kt_s, 1, ns).reshape(ns, 1, SB_HEADS, SB_HEAD_DIM),
            heads_last(vt_s, 1, ns).reshape(ns, 1, SB_HEADS, SB_HEAD_DIM),
            conv_sample)
```

```python
import functools

import jax
import jax.numpy as jnp
from jax import lax
from jax.experimental import pallas as pl
from jax.experimental.pallas import tpu as pltpu

F32 = jnp.float32
BF16 = jnp.bfloat16
I32 = jnp.int32

EPS = 1e-6
D_MODEL = 1024
BRANCH_W = 512
SB_HEADS = 8
SB_HEAD_DIM = 64
X_HEADS = 4
X_HEAD_DIM = 128
N_MEM = 256
N_GROUPS = 4
EXPERTS_PER_GROUP = 8
N_EXPERTS = N_GROUPS * EXPERTS_PER_GROUP
D_EXPERT = 512
PAGE_SIZE = 128

LANES = 128
SUBLANES = 8
VMEM_LIMIT = 56 * 1024 * 1024

SB_TK = 256
SB_TQ = 4 * SB_TK
MOE_BM = 256
XS_TOKENS = 4
ROW_DMA_UNROLL = 8


def _dot(a, b):
    return jnp.dot(a, b, preferred_element_type=F32)


def _dot_nt(a, b):
    return lax.dot_general(a, b, (((1,), (1,)), ((), ())), preferred_element_type=F32)


def _rms(x, g):
    return x * lax.rsqrt(jnp.mean(x * x, axis=-1, keepdims=True) + EPS) * g


def _softplus(z):
    return jnp.maximum(z, 0.0) + jnp.log(1.0 + jnp.exp(-jnp.abs(z)))


def _params(sem, vmem=VMEM_LIMIT):
    return pltpu.CompilerParams(dimension_semantics=sem, vmem_limit_bytes=vmem)


def _const_spec(shape):
    nd = len(shape)
    return pl.BlockSpec(shape, lambda *_: (0,) * nd, pipeline_mode=pl.Buffered(1))


def _memkv_kernel(mem_ref, g_ref, w_ref, kn_ref, mk_ref, mv_ref):
    xn = _rms(mem_ref[...], g_ref[...]).astype(BF16)
    kv = _dot(xn, w_ref[...])
    xw = X_HEADS * X_HEAD_DIM
    for h in range(X_HEADS):
        sl = slice(h * X_HEAD_DIM, (h + 1) * X_HEAD_DIM)
        mk_ref[:, sl] = _rms(kv[:, sl], kn_ref[...])
    mv_ref[...] = kv[:, xw:]


def memkv(mem2d, mem_norm, w_mem_kv_bf, k_norm, n_mem):
    n = mem2d.shape[0]
    xw = X_HEADS * X_HEAD_DIM
    return pl.pallas_call(
        _memkv_kernel,
        grid=(n // n_mem,),
        in_specs=[pl.BlockSpec((n_mem, D_MODEL), lambda i: (i, 0)),
                  _const_spec((1, D_MODEL)),
                  _const_spec((D_MODEL, 2 * xw)),
                  _const_spec((1, X_HEAD_DIM))],
        out_specs=[pl.BlockSpec((n_mem, xw), lambda i: (i, 0)),
                   pl.BlockSpec((n_mem, xw), lambda i: (i, 0))],
        out_shape=[jax.ShapeDtypeStruct((n, xw), F32), jax.ShapeDtypeStruct((n, xw), F32)],
        compiler_params=_params(("arbitrary",)),
        name="memkv",
    )(mem2d, mem_norm.reshape(1, -1), w_mem_kv_bf, k_norm.reshape(1, -1))


_C_UIN, _C_CG, _C_BG, _C_Q, _C_K, _C_QX, _C_GL = 0, 512, 1024, 1536, 2048, 3072, 3584
_W_MAIN = _C_GL + 3 * D_MODEL


def _kvt_kernel(w_ref, o_ref):
    o_ref[...] = w_ref[...].T.astype(BF16)


def kv_weight_t(w_in):
    kvw = 2 * SB_HEADS * SB_HEAD_DIM
    assert _C_K % kvw == 0
    return pl.pallas_call(
        _kvt_kernel, grid=(1,),
        in_specs=[pl.BlockSpec((D_MODEL, kvw), lambda i: (0, _C_K // kvw))],
        out_specs=pl.BlockSpec((kvw, D_MODEL), lambda i: (0, 0)),
        out_shape=jax.ShapeDtypeStruct((kvw, D_MODEL), BF16),
        compiler_params=_params(("arbitrary",)),
        name="kv_weight_t",
    )(w_in)


def _inproj_kernel(carry_conv, tiles_per_seq, tm, *refs):
    if carry_conv:
        (x_ref, n1_ref, wm_ref, wkvt_ref, wb0_ref, cw_ref, qn_ref,
         kt_ref, vt_ref, q_ref, qx_ref, g1_ref, g2_ref, mp_ref, tail_ref, ktb_ref, vtb_ref,
         carry_ref) = refs
    else:
        (x_ref, n1_ref, wm_ref, wkvt_ref, wb0_ref, cw_ref, qn_ref, um2_ref, um1_ref,
         kt_ref, vt_ref, q_ref, qx_ref, g1_ref, g2_ref, mp_ref, tail_ref) = refs

    xn = _rms(x_ref[...], n1_ref[...]).astype(BF16)

    def proj(c0, width):
        return _dot(xn, wm_ref[:, c0:c0 + width])

    kvt = _dot_nt(wkvt_ref[...], xn)
    sbw = SB_HEADS * SB_HEAD_DIM
    kt_ref[...] = kvt[:sbw]
    vt_ref[...] = kvt[sbw:]
    if carry_conv:
        for c in range(tm // SB_TK):
            ktb_ref[c] = kvt[:sbw, c * SB_TK:(c + 1) * SB_TK].astype(BF16)
            vtb_ref[c] = kvt[sbw:, c * SB_TK:(c + 1) * SB_TK].astype(BF16)

    q_ref[...] = (proj(_C_Q, sbw) * (SB_HEAD_DIM ** -0.5)).astype(BF16)

    qx = proj(_C_QX, X_HEADS * X_HEAD_DIM)
    for h in range(X_HEADS):
        sl = slice(h * X_HEAD_DIM, (h + 1) * X_HEAD_DIM)
        qx_ref[:, sl] = _rms(qx[:, sl], qn_ref[...]).astype(BF16)

    u = proj(_C_CG, BRANCH_W) * proj(_C_UIN, BRANCH_W)
    if carry_conv:
        @pl.when(pl.program_id(0) % tiles_per_seq == 0)
        def _():
            carry_ref[...] = jnp.zeros_like(carry_ref)
        ext = jnp.concatenate([carry_ref[...], u], axis=0)
        um1 = pltpu.roll(ext, 1, 0)[SUBLANES:]
        um2 = pltpu.roll(ext, 2, 0)[SUBLANES:]
        carry_ref[...] = u[tm - SUBLANES:]
        tail_ref[...] = u[tm - SUBLANES:]
    else:
        um1 = um1_ref[...]
        um2 = um2_ref[...]
        tail_ref[...] = u
    cw = cw_ref[...]
    y_conv = proj(_C_BG, BRANCH_W) * (um2 * cw[0:1] + um1 * cw[1:2] + u * cw[2:3])

    g0 = jax.nn.sigmoid(proj(_C_GL, D_MODEL))
    mp_ref[...] = g0 * _dot(y_conv.astype(BF16), wb0_ref[...])
    g1_ref[...] = jax.nn.sigmoid(proj(_C_GL + D_MODEL, D_MODEL))
    g2_ref[...] = jax.nn.sigmoid(proj(_C_GL + 2 * D_MODEL, D_MODEL))


def inproj(x2d, norm1, w_main_bf, w_kvt_bf, wb0_bf, conv_w, q_norm, seq_len, tm, conv_state=None):
    n = x2d.shape[0]
    carry = conv_state is None
    nseq = n // seq_len if carry else 1
    slen = seq_len if carry else n
    tps = slen // tm
    sbw = SB_HEADS * SB_HEAD_DIM
    xw = X_HEADS * X_HEAD_DIM
    nkb = max(tm // SB_TK, 1)
    tail_rows = SUBLANES if carry else tm

    row = lambda w: pl.BlockSpec((tm, w), lambda i: (i, 0))
    kt_spec = pl.BlockSpec((None, sbw, tm), lambda i: (i // tps, 0, i % tps))
    ktb_spec = pl.BlockSpec((None, nkb, sbw, SB_TK), lambda i: (i // tps, i % tps, 0, 0))
    in_specs = [row(D_MODEL), _const_spec((1, D_MODEL)), _const_spec((D_MODEL, _W_MAIN)),
                _const_spec((2 * sbw, D_MODEL)), _const_spec((BRANCH_W, D_MODEL)),
                _const_spec((3, BRANCH_W)), _const_spec((1, X_HEAD_DIM))]
    args = [x2d, norm1.reshape(1, -1), w_main_bf, w_kvt_bf, wb0_bf, conv_w, q_norm.reshape(1, -1)]
    scratch = []
    if carry:
        scratch = [pltpu.VMEM((SUBLANES, BRANCH_W), F32)]
    else:
        in_specs += [row(BRANCH_W), row(BRANCH_W)]
        args += list(conv_state)
    out_specs = [kt_spec, kt_spec, row(sbw), row(xw),
                 row(D_MODEL), row(D_MODEL), row(D_MODEL),
                 pl.BlockSpec((None, tail_rows, BRANCH_W), lambda i: (i // tps, 0, 0))]
    out_shape = [jax.ShapeDtypeStruct((nseq, sbw, slen), F32), jax.ShapeDtypeStruct((nseq, sbw, slen), F32),
                 jax.ShapeDtypeStruct((n, sbw), BF16), jax.ShapeDtypeStruct((n, xw), BF16),
                 jax.ShapeDtypeStruct((n, D_MODEL), F32), jax.ShapeDtypeStruct((n, D_MODEL), F32),
                 jax.ShapeDtypeStruct((n, D_MODEL), F32),
                 jax.ShapeDtypeStruct((nseq, tail_rows, BRANCH_W), F32)]
    if carry:
        out_specs += [ktb_spec, ktb_spec]
        out_shape += [jax.ShapeDtypeStruct((nseq, slen // SB_TK, sbw, SB_TK), BF16)] * 2
    return pl.pallas_call(
        functools.partial(_inproj_kernel, carry, tps, tm),
        grid=(n // tm,),
        in_specs=in_specs, out_specs=out_specs, out_shape=out_shape,
        scratch_shapes=scratch,
        compiler_params=_params(("arbitrary",)),
        name="inproj_prompt" if carry else "inproj_sample",
    )(*args)


def _sbp_kernel(bias_ref, q_ref, kt_ref, vt_ref, o_ref, acc_ref, car_ref):
    pair = pl.program_id(1)
    qi = pl.program_id(2)
    tq, tk = SB_TQ, SB_TK
    lane = lax.broadcasted_iota(I32, (tq, LANES), 1)
    q = q_ref[...]
    zero = jnp.zeros_like(q)
    qh = (jnp.where(lane < SB_HEAD_DIM, q, zero), jnp.where(lane >= SB_HEAD_DIM, q, zero))
    r = lax.broadcasted_iota(I32, (tk, tk), 0)
    c = lax.broadcasted_iota(I32, (tk, tk), 1)
    tri = (r >= c).astype(BF16)
    qrow = lax.broadcasted_iota(I32, (tq, tk), 0)
    kcol = lax.broadcasted_iota(I32, (tq, tk), 1)
    acc_ref[...] = jnp.zeros_like(acc_ref)
    car_ref[...] = jnp.zeros_like(car_ref)

    def block(j, diag_sub):
        kt = kt_ref[j]
        vt = vt_ref[j]
        rs = slice(0, tq)
        if diag_sub is not None:
            rs = slice(diag_sub * tk, tq)
            causal = (kcol + diag_sub * tk < qrow)[rs]
        for h in range(2):
            z = _dot(qh[h][rs], kt) + bias_ref[2 * pair + h]
            sp = _softplus(z.astype(BF16))
            if diag_sub is not None:
                sp = jnp.where(causal, sp, jnp.zeros_like(sp))
            car = car_ref[h, rs]
            cum = _dot(sp, tri)
            a = jnp.exp(z - (cum + jnp.concatenate([car] * (tk // LANES), axis=1)))
            if diag_sub is not None:
                a = jnp.where(causal, a, 0.0)
            acc_ref[h, rs] += _dot_nt(a.astype(BF16), vt)
            car_ref[h, rs] = car + jnp.broadcast_to(cum[:, 0:1], car.shape)

    nsub = tq // tk
    for s in reversed(range(nsub)):
        block(qi * nsub + s, s)

    def body(t, carry):
        block(qi * nsub - 1 - 2 * t, None)
        block(qi * nsub - 2 - 2 * t, None)
        return carry

    lax.fori_loop(0, qi * (nsub // 2), body, 0)
    o_ref[...] = jnp.where(lane < SB_HEAD_DIM, acc_ref[0], acc_ref[1]).astype(o_ref.dtype)


def sb_prompt(q_bf, ktb, vtb, sb_bias, seq_len):
    n = q_bf.shape[0]
    nseq = n // seq_len
    nq = seq_len // SB_TQ
    nk = seq_len // SB_TK
    npair = SB_HEADS // 2
    grid_spec = pltpu.PrefetchScalarGridSpec(
        num_scalar_prefetch=0,
        grid=(nseq, npair, nq),
        in_specs=[pl.BlockSpec(memory_space=pltpu.SMEM),
                  pl.BlockSpec((SB_TQ, LANES), lambda b, p, i: (b * nq + i, p)),
                  pl.BlockSpec((None, nk, LANES, SB_TK), lambda b, p, i: (b, 0, p, 0)),
                  pl.BlockSpec((None, nk, LANES, SB_TK), lambda b, p, i: (b, 0, p, 0))],
        out_specs=pl.BlockSpec((SB_TQ, LANES), lambda b, p, i: (b * nq + i, p)),
        scratch_shapes=[pltpu.VMEM((2, SB_TQ, LANES), F32), pltpu.VMEM((2, SB_TQ, LANES), F32)],
    )
    return pl.pallas_call(
        _sbp_kernel, grid_spec=grid_spec,
        out_shape=jax.ShapeDtypeStruct((n, SB_HEADS * SB_HEAD_DIM), BF16),
        compiler_params=_params(("arbitrary", "arbitrary", "arbitrary")),
        name="sb_prompt",
    )(sb_bias.astype(F32), q_bf, ktb, vtb)


def _xp_kernel(qx_ref, mk_ref, mv_ref, o_ref):
    for h in range(X_HEADS):
        sl = slice(h * X_HEAD_DIM, (h + 1) * X_HEAD_DIM)
        s = _dot_nt(qx_ref[:, sl], mk_ref[:, sl]) * (X_HEAD_DIM ** -0.5)
        e = jnp.exp(s - jnp.max(s, axis=-1, keepdims=True))
        p = e / jnp.sum(e, axis=-1, keepdims=True)
        o_ref[:, sl] = _dot(p.astype(BF16), mv_ref[:, sl]).astype(o_ref.dtype)


def xattn_prompt(qx_bf, mk_bf, mv_bf, seq_len, n_mem, tm):
    n = qx_bf.shape[0]
    tps = seq_len // tm
    xw = X_HEADS * X_HEAD_DIM
    return pl.pallas_call(
        _xp_kernel, grid=(n // tm,),
        in_specs=[pl.BlockSpec((tm, xw), lambda i: (i, 0)),
                  pl.BlockSpec((n_mem, xw), lambda i: (i // tps, 0)),
                  pl.BlockSpec((n_mem, xw), lambda i: (i // tps, 0))],
        out_specs=pl.BlockSpec((tm, xw), lambda i: (i, 0)),
        out_shape=jax.ShapeDtypeStruct((n, xw), BF16),
        compiler_params=_params(("arbitrary",)),
        name="xattn_prompt",
    )(qx_bf, mk_bf, mv_bf)


def _split_bf16(x):
    hi = x.astype(BF16)
    return hi, (x - hi.astype(F32)).astype(BF16)


def _sbs_kernel(new_key_visible, npages, pt_ref, bias_ref, qbd_ref, kn_ref, vn_ref, *refs):
    del pt_ref
    k_refs, v_refs, o_ref = refs[:npages], refs[npages:2 * npages], refs[2 * npages]
    b = pl.program_id(0)
    psz = PAGE_SIZE
    sbw = SB_HEADS * SB_HEAD_DIM
    rows = npages * SB_HEADS
    head_bits = SB_HEADS.bit_length() - 1
    qbd = qbd_ref[0]
    bias = bias_ref[...]

    mine = lax.broadcasted_iota(I32, (SB_HEADS, kn_ref.shape[1]), 1) == b
    zs = _dot(qbd, kn_ref[...].astype(BF16)) + bias[:, 0:1]
    seen = jnp.logical_and(mine, new_key_visible)
    sp0 = jnp.where(seen, _softplus(zs), 0.0)
    a0 = jnp.where(seen, jnp.exp(zs - sp0), 0.0)
    sp0_tot = jnp.sum(sp0, axis=-1, keepdims=True)
    a0_rows = jnp.concatenate(
        [jnp.broadcast_to(a0[h:h + 1], (SB_HEAD_DIM, a0.shape[1])) for h in range(SB_HEADS)], axis=0)
    new = jnp.sum(a0_rows * vn_ref[...], axis=-1, keepdims=True)

    z = jnp.concatenate([_dot(qbd, k_refs[j][...].astype(BF16)) + bias for j in range(npages)], axis=0)
    sp = _softplus(z)
    r = lax.broadcasted_iota(I32, (psz, psz), 0)
    c = lax.broadcasted_iota(I32, (psz, psz), 1)
    tri = (r >= c).astype(BF16)
    hi, lo = _split_bf16(sp)
    cum = _dot(hi, tri) + _dot(lo, tri)
    tot = jnp.broadcast_to(cum[:, 0:1], (rows, psz))
    rr = lax.broadcasted_iota(I32, (rows, rows), 0)
    cc = lax.broadcasted_iota(I32, (rows, rows), 1)
    later = jnp.logical_and((cc >> head_bits) > (rr >> head_bits),
                            (cc & (SB_HEADS - 1)) == (rr & (SB_HEADS - 1))).astype(BF16)
    th, tl = _split_bf16(tot)
    carry = _dot(later, th) + _dot(later, tl) + jnp.concatenate([sp0_tot] * npages, axis=0)
    a = jnp.exp(z - cum - carry)

    accs = []
    for h in range(SB_HEADS):
        hs = slice(h * SB_HEAD_DIM, (h + 1) * SB_HEAD_DIM)
        acc = jnp.zeros((SB_HEAD_DIM, psz), F32)
        for j in range(npages):
            rj = j * SB_HEADS + h
            acc = acc + jnp.broadcast_to(a[rj:rj + 1], (SB_HEAD_DIM, psz)) * v_refs[j][hs, :]
        accs.append(acc)
    acc = jnp.concatenate(accs, axis=0)
    lane0 = lax.broadcasted_iota(I32, (sbw, psz), 1) == 0
    acc = acc + jnp.where(lane0, new, 0.0)
    hi2, lo2 = _split_bf16(acc)
    ones = jnp.ones((SUBLANES, psz), BF16)
    y = _dot_nt(ones, hi2) + _dot_nt(ones, lo2)
    o_ref[pl.ds(b, 1), :] = y[0:1]


def sb_sample(q_s, kn_t, vn_t, cache_k_t, cache_v_t, page_table, sb_bias, new_key_visible):
    nb, npages = page_table.shape
    sbw = SB_HEADS * SB_HEAD_DIM
    head_of_col = jnp.arange(sbw) // SB_HEAD_DIM
    sel = head_of_col[None, :] == jnp.arange(SB_HEADS)[:, None]
    qbd = jnp.where(sel[None], q_s[:, None, :], 0.0).astype(BF16)
    assert PAGE_SIZE == LANES
    bias = jnp.broadcast_to(sb_bias.astype(F32)[:, None], (SB_HEADS, LANES))
    page = lambda j: pl.BlockSpec((None, sbw, PAGE_SIZE), lambda b, pt: (pt[b, j], 0, 0))
    grid_spec = pltpu.PrefetchScalarGridSpec(
        num_scalar_prefetch=1,
        grid=(nb,),
        in_specs=[pl.BlockSpec((SB_HEADS, LANES), lambda b, pt: (0, 0)),
                  pl.BlockSpec((1, SB_HEADS, sbw), lambda b, pt: (b, 0, 0)),
                  pl.BlockSpec((sbw, nb), lambda b, pt: (0, 0)),
                  pl.BlockSpec((sbw, nb), lambda b, pt: (0, 0))]
                 + [page(j) for j in range(npages)] * 2,
        out_specs=pl.BlockSpec((nb, sbw), lambda b, pt: (0, 0)),
    )
    return pl.pallas_call(
        functools.partial(_sbs_kernel, new_key_visible, npages), grid_spec=grid_spec,
        out_shape=jax.ShapeDtypeStruct((nb, sbw), F32),
        compiler_params=_params(("arbitrary",)),
        name="sb_sample",
    )(page_table, bias, qbd, kn_t, vn_t, *([cache_k_t] * npages), *([cache_v_t] * npages))


def _xs_kernel(qx_ref, mk_ref, mv_ref, o_ref):
    step = pl.program_id(0)
    rows = N_MEM * X_HEADS
    own = ((lax.broadcasted_iota(I32, (SUBLANES, rows), 1) & (X_HEADS - 1))
           == lax.broadcasted_iota(I32, (SUBLANES, rows), 0))
    for t in range(XS_TOKENS):
        mk = mk_ref[t].astype(BF16)
        mv = mv_ref[t].astype(BF16)
        s = _dot_nt(qx_ref[t], mk) * (X_HEAD_DIM ** -0.5)
        s = jnp.where(own, s, -jnp.inf)
        m = jnp.max(s, axis=-1, keepdims=True)
        m = jnp.where(m == -jnp.inf, 0.0, m)
        e = jnp.exp(s - m)
        den = jnp.sum(e, axis=-1, keepdims=True)
        p = e / jnp.where(den == 0.0, 1.0, den)
        y = _dot(p.astype(BF16), mv)
        o_ref[pl.ds(step * XS_TOKENS + t, 1)] = y[None, 0:X_HEADS]


def xattn_sample(qx_bf, cache_mem_k, cache_mem_v):
    nb = qx_bf.shape[0]
    rows = N_MEM * X_HEADS
    mk = cache_mem_k.reshape(nb, rows, X_HEAD_DIM)
    mv = cache_mem_v.reshape(nb, rows, X_HEAD_DIM)
    qx = qx_bf.reshape(nb, X_HEADS, X_HEAD_DIM)
    qx = jnp.concatenate([qx, jnp.zeros((nb, SUBLANES - X_HEADS, X_HEAD_DIM), BF16)], axis=1)
    assert nb % XS_TOKENS == 0
    out = pl.pallas_call(
        _xs_kernel, grid=(nb // XS_TOKENS,),
        in_specs=[pl.BlockSpec((XS_TOKENS, SUBLANES, X_HEAD_DIM), lambda b: (b, 0, 0)),
                  pl.BlockSpec((XS_TOKENS, rows, X_HEAD_DIM), lambda b: (b, 0, 0)),
                  pl.BlockSpec((XS_TOKENS, rows, X_HEAD_DIM), lambda b: (b, 0, 0))],
        out_specs=pl.BlockSpec((nb, X_HEADS, X_HEAD_DIM), lambda b: (0, 0, 0)),
        out_shape=jax.ShapeDtypeStruct((nb, X_HEADS, X_HEAD_DIM), F32),
        compiler_params=_params(("arbitrary",)),
        name="xattn_sample",
    )(qx, mk, mv)
    return out.reshape(nb, X_HEADS * X_HEAD_DIM)


_R_E1, _R_E2, _R_R1, _R_R2, _R_G1, _R_G2 = 0, 1, 2, 3, 4, 5
_GROUP_LANE0 = N_EXPERTS


def _merge_kernel(tm, x_ref, mp_ref, ysb_ref, yx_ref, g1_ref, g2_ref, wb1_ref, wb2_ref, wo_ref,
                  n2_ref, wr_ref, br_ref, cin_ref, x1_ref, h_ref, route_ref, route_t_ref, cout_ref, cnt_ref):
    @pl.when(pl.program_id(0) == 0)
    def _():
        cnt_ref[...] = cin_ref[...]

    m = (mp_ref[...] + g1_ref[...] * _dot(ysb_ref[...].astype(BF16), wb1_ref[...])
         + g2_ref[...] * _dot(yx_ref[...].astype(BF16), wb2_ref[...]))
    x1 = x_ref[...] + _dot(m.astype(BF16), wo_ref[...])
    x1_ref[...] = x1
    hn = _rms(x1, n2_ref[...])
    h_ref[...] = hn
    logits = _dot(hn.astype(BF16), wr_ref[...]) + br_ref[...]

    lane = lax.broadcasted_iota(I32, (tm, LANES), 1).astype(F32)
    big = float(LANES)
    ninf = -jnp.inf

    def first_argmax(v):
        mx = jnp.max(v, axis=-1, keepdims=True)
        idx = jnp.min(jnp.where(v == mx, lane, big), axis=-1, keepdims=True)
        return mx, idx

    gl = jnp.where((lane >= _GROUP_LANE0) & (lane < _GROUP_LANE0 + N_GROUPS), logits, ninf)
    gmax, glane = first_argmax(gl)
    gprob = 1.0 / jnp.sum(jnp.exp(gl - gmax), axis=-1, keepdims=True)
    e0 = (glane - _GROUP_LANE0) * EXPERTS_PER_GROUP
    el = jnp.where((lane >= e0) & (lane < e0 + EXPERTS_PER_GROUP), logits, ninf)
    m1, i1 = first_argmax(el)
    m2, i2 = first_argmax(jnp.where(lane == i1, ninf, el))
    e2 = jnp.exp(m2 - m1)
    gate1 = gprob / (1.0 + e2)
    gate2 = gprob * e2 / (1.0 + e2)

    oh1 = (lane == i1).astype(F32)
    oh2 = (lane == i2).astype(F32)
    cnt = oh1 + oh2
    r = lax.broadcasted_iota(I32, (tm, tm), 0)
    c = lax.broadcasted_iota(I32, (tm, tm), 1)
    before = _dot((c < r).astype(BF16), cnt.astype(BF16)) + cnt_ref[...]
    rank1 = jnp.sum(oh1 * before, axis=-1, keepdims=True)
    rank2 = jnp.sum(oh2 * before, axis=-1, keepdims=True)
    cnt_ref[...] = cnt_ref[...] + jnp.sum(cnt, axis=0, keepdims=True)
    cout_ref[...] = cnt_ref[...]

    rec = jnp.zeros((tm, LANES), F32)
    for ln, val in ((_R_E1, i1.astype(F32)), (_R_E2, i2.astype(F32)), (_R_R1, rank1), (_R_R2, rank2),
                    (_R_G1, gate1), (_R_G2, gate2)):
        rec = jnp.where(lane == ln, val, rec)
    route_ref[...] = rec
    route_t_ref[...] = rec.T[0:SUBLANES]


def merge(x2d, mp, ysb, yx, g1, g2, wb1_bf, wb2_bf, wo_bf, norm2, w_route_bf, b_route, cnt_in, tm):
    n = x2d.shape[0]
    row = lambda w: pl.BlockSpec((tm, w), lambda i: (i, 0))
    return pl.pallas_call(
        functools.partial(_merge_kernel, tm), grid=(n // tm,),
        in_specs=[row(D_MODEL), row(D_MODEL), row(BRANCH_W), row(BRANCH_W), row(D_MODEL), row(D_MODEL),
                  _const_spec((BRANCH_W, D_MODEL)), _const_spec((BRANCH_W, D_MODEL)),
                  _const_spec((D_MODEL, D_MODEL)), _const_spec((1, D_MODEL)),
                  _const_spec((D_MODEL, LANES)), _const_spec((1, LANES)), _const_spec((1, LANES))],
        out_specs=[row(D_MODEL), row(D_MODEL), row(LANES), pl.BlockSpec((SUBLANES, tm), lambda i: (0, i)),
                   pl.BlockSpec((1, LANES), lambda i: (0, 0))],
        out_shape=[jax.ShapeDtypeStruct((n, D_MODEL), F32), jax.ShapeDtypeStruct((n, D_MODEL), F32),
                   jax.ShapeDtypeStruct((n, LANES), F32), jax.ShapeDtypeStruct((SUBLANES, n), F32),
                   jax.ShapeDtypeStruct((1, LANES), F32)],
        scratch_shapes=[pltpu.VMEM((1, LANES), F32)],
        compiler_params=_params(("arbitrary",)),
        name="merge",
    )(x2d, mp, ysb, yx, g1, g2, wb1_bf, wb2_bf, wo_bf, norm2.reshape(1, -1), w_route_bf, b_route, cnt_in)


def _row_copy(src, src_row, dst, dst_row, sem):
    return pltpu.make_async_copy(src.at[pl.ds(src_row, 1)], dst.at[pl.ds(dst_row, 1)], sem)


def _dispatch_kernel(tm, dest_ref, h_ref, buf_in_ref, buf_ref, sem):
    del buf_in_ref

    def issue(t, carry):
        _row_copy(h_ref, t, buf_ref, dest_ref[0, 0, t], sem).start()
        _row_copy(h_ref, t, buf_ref, dest_ref[0, 0, tm + t], sem).start()
        return carry

    lax.fori_loop(0, tm, issue, 0, unroll=ROW_DMA_UNROLL)
    for _ in range(2):
        pltpu.make_async_copy(h_ref, buf_ref.at[pl.ds(0, tm)], sem).wait()


def _slot_tiles(dest, tm):
    nt = dest.shape[1] // tm
    return dest.reshape(2, nt, tm).transpose(1, 0, 2).reshape(nt, 1, 2 * tm)


def dispatch(h, dest, buf_zero, tm):
    n = h.shape[0]
    nt = n // tm
    return pl.pallas_call(
        functools.partial(_dispatch_kernel, tm), grid=(nt,),
        in_specs=[pl.BlockSpec((1, 1, 2 * tm), lambda i: (i, 0, 0), memory_space=pltpu.SMEM),
                  pl.BlockSpec((tm, D_MODEL), lambda i: (i, 0)),
                  pl.BlockSpec(memory_space=pl.ANY)],
        out_specs=pl.BlockSpec(memory_space=pl.ANY),
        out_shape=jax.ShapeDtypeStruct(buf_zero.shape, buf_zero.dtype),
        scratch_shapes=[pltpu.SemaphoreType.DMA(())],
        input_output_aliases={2: 0},
        compiler_params=_params(("arbitrary",)),
        name="dispatch",
    )(_slot_tiles(dest, tm), h, buf_zero)


def _experts_kernel(be_ref, nu_ref, x_ref, w1_ref, w3_ref, w2_ref, y_ref):
    del be_ref

    @pl.when(pl.program_id(0) < nu_ref[0])
    def _():
        xb = x_ref[...].astype(BF16)
        h1 = _dot(xb, w1_ref[...].astype(BF16))
        h3 = _dot(xb, w3_ref[...].astype(BF16))
        act = (h1 * jax.nn.sigmoid(h1)) * h3
        y_ref[...] = _dot(act.astype(BF16), w2_ref[...].astype(BF16))

    @pl.when(pl.program_id(0) >= nu_ref[0])
    def _():
        y_ref[...] = jnp.zeros_like(y_ref)


def experts(buf, block_e, n_used, w1, w3, w2):
    nblk = buf.shape[0] // MOE_BM
    clamp = lambda i, nu: jnp.minimum(i, nu[0] - 1)
    grid_spec = pltpu.PrefetchScalarGridSpec(
        num_scalar_prefetch=2,
        grid=(nblk,),
        in_specs=[pl.BlockSpec((MOE_BM, D_MODEL), lambda i, be, nu: (clamp(i, nu), 0)),
                  pl.BlockSpec((None, D_MODEL, D_EXPERT), lambda i, be, nu: (be[i], 0, 0)),
                  pl.BlockSpec((None, D_MODEL, D_EXPERT), lambda i, be, nu: (be[i], 0, 0)),
                  pl.BlockSpec((None, D_EXPERT, D_MODEL), lambda i, be, nu: (be[i], 0, 0))],
        out_specs=pl.BlockSpec((MOE_BM, D_MODEL), lambda i, be, nu: (i, 0)),
    )
    return pl.pallas_call(
        _experts_kernel, grid_spec=grid_spec,
        out_shape=jax.ShapeDtypeStruct(buf.shape, F32),
        compiler_params=_params(("arbitrary",)),
        name="experts",
    )(block_e, n_used, buf, w1, w3, w2)


def _combine_kernel(tm, dest_ref, x1_ref, route_ref, ybuf_ref, o_ref, rows_ref, sem):
    def issue(t, carry):
        _row_copy(ybuf_ref, dest_ref[0, 0, t], rows_ref.at[0], t, sem).start()
        _row_copy(ybuf_ref, dest_ref[0, 0, tm + t], rows_ref.at[1], t, sem).start()
        return carry

    lax.fori_loop(0, tm, issue, 0, unroll=ROW_DMA_UNROLL)
    for k in range(2):
        pltpu.make_async_copy(ybuf_ref.at[pl.ds(0, tm)], rows_ref.at[k], sem).wait()
    route = route_ref[...]
    o_ref[...] = (x1_ref[...] + route[:, _R_G1:_R_G1 + 1] * rows_ref[0]
                  + route[:, _R_G2:_R_G2 + 1] * rows_ref[1])


def combine(x1, route, dest, ybuf, tm):
    n = x1.shape[0]
    nt = n // tm
    return pl.pallas_call(
        functools.partial(_combine_kernel, tm), grid=(nt,),
        in_specs=[pl.BlockSpec((1, 1, 2 * tm), lambda i: (i, 0, 0), memory_space=pltpu.SMEM),
                  pl.BlockSpec((tm, D_MODEL), lambda i: (i, 0)),
                  pl.BlockSpec((tm, LANES), lambda i: (i, 0)),
                  pl.BlockSpec(memory_space=pl.ANY)],
        out_specs=pl.BlockSpec((tm, D_MODEL), lambda i: (i, 0)),
        out_shape=jax.ShapeDtypeStruct((n, D_MODEL), F32),
        scratch_shapes=[pltpu.VMEM((2, tm, D_MODEL), F32), pltpu.SemaphoreType.DMA(())],
        compiler_params=_params(("arbitrary",)),
        name="combine",
    )(_slot_tiles(dest, tm), x1, route, ybuf)


def kernel(x_prompt, x_sample, mem_prompt, cache_k, cache_v, cache_mem_k, cache_mem_v, state_conv, page_table, norm1, w_in, conv_w, sb_bias, mem_norm, w_mem_kv, q_norm, k_norm, w_branch, w_o, norm2, w_group, b_group, w_router, b_router, w1, w3, w2):
    nb, seq, _ = x_prompt.shape
    ns, dec_seq, _ = x_sample.shape
    assert dec_seq == 1 and seq % SB_TQ == 0 and SB_TQ % (2 * SB_TK) == 0
    n_mem = mem_prompt.shape[1]
    sbw = SB_HEADS * SB_HEAD_DIM
    xw = X_HEADS * X_HEAD_DIM
    past = page_table.shape[1] * PAGE_SIZE

    assert w_in.shape[1] == _W_MAIN and _C_QX == _C_K + 2 * sbw and _C_GL == _C_QX + xw
    w_main = w_in.astype(BF16)
    w_kvt = kv_weight_t(w_in)
    wb = w_branch.astype(BF16)
    wo_bf = w_o.astype(BF16)
    w_route = jnp.zeros((D_MODEL, LANES), F32)
    w_route = w_route.at[:, :N_EXPERTS].set(w_router).at[:, N_EXPERTS:N_EXPERTS + N_GROUPS].set(w_group)
    b_route = jnp.zeros((1, LANES), F32)
    b_route = b_route.at[0, :N_EXPERTS].set(b_router).at[0, N_EXPERTS:N_EXPERTS + N_GROUPS].set(b_group)
    w_route = w_route.astype(BF16)

    xp = x_prompt.reshape(nb * seq, D_MODEL)
    mk_p, mv_p = memkv(mem_prompt.reshape(nb * n_mem, D_MODEL), mem_norm, w_mem_kv.astype(BF16), k_norm, n_mem)
    (kt_p, vt_p, q_p, qx_p, g1_p, g2_p, mp_p, tail_p, ktb_p, vtb_p) = inproj(
        xp, norm1, w_main, w_kvt, wb[0], conv_w, q_norm, seq, 512)
    ysb_p = sb_prompt(q_p, ktb_p, vtb_p, sb_bias, seq)
    yx_p = xattn_prompt(qx_p, mk_p.astype(BF16), mv_p.astype(BF16), seq, n_mem, 512)

    xs = x_sample.reshape(ns, D_MODEL)
    (kt_s, vt_s, q_s, qx_s, g1_s, g2_s, mp_s, u_s) = inproj(
        xs, norm1, w_main, w_kvt, wb[0], conv_w, q_norm, ns, ns,
        conv_state=(state_conv[:, 0], state_conv[:, 1]))
    ck_t = cache_k.transpose(0, 2, 3, 1).reshape(cache_k.shape[0], sbw, PAGE_SIZE)
    cv_t = cache_v.transpose(0, 2, 3, 1).reshape(cache_v.shape[0], sbw, PAGE_SIZE)
    ysb_s = sb_sample(q_s.astype(F32), kt_s[0], vt_s[0], ck_t, cv_t, page_table, sb_bias,
                      new_key_visible=(past < past))
    yx_s = xattn_sample(qx_s, cache_mem_k, cache_mem_v)

    cnt0 = jnp.zeros((1, LANES), F32)
    x1_p, h_p, route_p, rt_p, cnt1 = merge(xp, mp_p, ysb_p, yx_p, g1_p, g2_p, wb[1], wb[2], wo_bf, norm2,
                                           w_route, b_route, cnt0, 512)
    x1_s, h_s, route_s, rt_s, cnt2 = merge(xs, mp_s, ysb_s, yx_s, g1_s, g2_s, wb[1], wb[2], wo_bf, norm2,
                                           w_route, b_route, cnt1, ns)

    n_slots = 2 * (nb * seq + ns)
    nblk = -(-(n_slots + N_EXPERTS * (MOE_BM - 1)) // MOE_BM)
    counts = cnt2[0, :N_EXPERTS].astype(I32)
    padded = (counts + MOE_BM - 1) // MOE_BM * MOE_BM
    ends = jnp.cumsum(padded)
    starts = ends - padded
    n_used = (ends[-1] // MOE_BM).astype(I32).reshape(1)
    blk_start = jnp.arange(nblk, dtype=I32) * MOE_BM
    block_e = jnp.sum(blk_start[:, None] >= ends[None, :], axis=1).astype(I32)
    block_e = jnp.minimum(block_e, block_e[jnp.maximum(n_used[0] - 1, 0)])

    def dests(route_t):
        e = route_t[_R_E1:_R_E2 + 1].astype(I32)
        rk = route_t[_R_R1:_R_R2 + 1].astype(I32)
        first = jnp.zeros_like(e)
        for x in range(N_EXPERTS):
            first = jnp.where(e == x, starts[x], first)
        return first + rk

    dest_p, dest_s = dests(rt_p), dests(rt_s)
    buf = jnp.zeros((nblk * MOE_BM, D_MODEL), F32)
    buf = dispatch(h_p, dest_p, buf, 512)
    buf = dispatch(h_s, dest_s, buf, ns)
    ybuf = experts(buf, block_e, n_used, w1, w3, w2)
    y_p = combine(x1_p, route_p, dest_p, ybuf, 256)
    y_s = combine(x1_s, route_s, dest_s, ybuf, ns)

    def heads_last(t, b, s):
        return t.reshape(b, SB_HEADS, SB_HEAD_DIM, s).transpose(0, 3, 1, 2)

    conv_prompt = tail_p[:, SUBLANES - 2:, :]
    conv_sample = jnp.stack([state_conv[:, 1], u_s[0]], axis=1)
    return (y_p.reshape(nb, seq, D_MODEL), y_s.reshape(ns, 1, D_MODEL),
            heads_last(kt_p, nb, seq), heads_last(vt_p, nb, seq), conv_prompt,
            mk_p.reshape(nb, n_mem, X_HEADS, X_HEAD_DIM), mv_p.reshape(nb, n_mem, X_HEADS, X_HEAD_DIM),
            heads_last(kt_s, 1, ns).reshape(ns, 1, SB_HEADS, SB_HEAD_DIM),
            heads_last(vt_s, 1, ns).reshape(ns, 1, SB_HEADS, SB_HEAD_DIM),
            conv_sample)
```

```python
import functools

import jax
import jax.numpy as jnp
from jax import lax
from jax.experimental import pallas as pl
from jax.experimental.pallas import tpu as pltpu

F32 = jnp.float32
BF16 = jnp.bfloat16
I32 = jnp.int32

EPS = 1e-6
D_MODEL = 1024
BRANCH_W = 512
SB_HEADS = 8
SB_HEAD_DIM = 64
X_HEADS = 4
X_HEAD_DIM = 128
N_MEM = 256
N_GROUPS = 4
EXPERTS_PER_GROUP = 8
N_EXPERTS = N_GROUPS * EXPERTS_PER_GROUP
D_EXPERT = 512
PAGE_SIZE = 128

LANES = 128
SUBLANES = 8
VMEM_LIMIT = 56 * 1024 * 1024

SB_TK = 256
SB_TQ = 8 * SB_TK
MOE_BM = 256
XS_TOKENS = 4
ROW_DMA_UNROLL = 8


def _dot(a, b):
    return jnp.dot(a, b, preferred_element_type=F32)


def _dot_nt(a, b):
    return lax.dot_general(a, b, (((1,), (1,)), ((), ())), preferred_element_type=F32)


def _rms(x, g):
    return x * lax.rsqrt(jnp.mean(x * x, axis=-1, keepdims=True) + EPS) * g


def _softplus(z):
    return jnp.maximum(z, 0.0) + jnp.log(1.0 + jnp.exp(-jnp.abs(z)))


def _params(sem, vmem=VMEM_LIMIT):
    return pltpu.CompilerParams(dimension_semantics=sem, vmem_limit_bytes=vmem)


def _const_spec(shape):
    nd = len(shape)
    return pl.BlockSpec(shape, lambda *_: (0,) * nd, pipeline_mode=pl.Buffered(1))


def _memkv_kernel(mem_ref, g_ref, w_ref, kn_ref, mk_ref, mv_ref):
    xn = _rms(mem_ref[...], g_ref[...]).astype(BF16)
    kv = _dot(xn, w_ref[...])
    xw = X_HEADS * X_HEAD_DIM
    for h in range(X_HEADS):
        sl = slice(h * X_HEAD_DIM, (h + 1) * X_HEAD_DIM)
        mk_ref[:, sl] = _rms(kv[:, sl], kn_ref[...])
    mv_ref[...] = kv[:, xw:]


def memkv(mem2d, mem_norm, w_mem_kv_bf, k_norm, n_mem):
    n = mem2d.shape[0]
    xw = X_HEADS * X_HEAD_DIM
    return pl.pallas_call(
        _memkv_kernel,
        grid=(n // n_mem,),
        in_specs=[pl.BlockSpec((n_mem, D_MODEL), lambda i: (i, 0)),
                  _const_spec((1, D_MODEL)),
                  _const_spec((D_MODEL, 2 * xw)),
                  _const_spec((1, X_HEAD_DIM))],
        out_specs=[pl.BlockSpec((n_mem, xw), lambda i: (i, 0)),
                   pl.BlockSpec((n_mem, xw), lambda i: (i, 0))],
        out_shape=[jax.ShapeDtypeStruct((n, xw), F32), jax.ShapeDtypeStruct((n, xw), F32)],
        compiler_params=_params(("arbitrary",)),
        name="memkv",
    )(mem2d, mem_norm.reshape(1, -1), w_mem_kv_bf, k_norm.reshape(1, -1))


_C_UIN, _C_CG, _C_BG, _C_Q, _C_K, _C_QX, _C_GL = 0, 512, 1024, 1536, 2048, 3072, 3584
_W_MAIN = _C_GL + 3 * D_MODEL


def _kvt_kernel(w_ref, o_ref):
    o_ref[...] = w_ref[...].T.astype(BF16)


def kv_weight_t(w_in):
    kvw = 2 * SB_HEADS * SB_HEAD_DIM
    assert _C_K % kvw == 0
    return pl.pallas_call(
        _kvt_kernel, grid=(1,),
        in_specs=[pl.BlockSpec((D_MODEL, kvw), lambda i: (0, _C_K // kvw))],
        out_specs=pl.BlockSpec((kvw, D_MODEL), lambda i: (0, 0)),
        out_shape=jax.ShapeDtypeStruct((kvw, D_MODEL), BF16),
        compiler_params=_params(("arbitrary",)),
        name="kv_weight_t",
    )(w_in)


def _inproj_kernel(carry_conv, tiles_per_seq, tm, *refs):
    if carry_conv:
        (x_ref, n1_ref, wm_ref, wkvt_ref, wb0_ref, cw_ref, qn_ref,
         kt_ref, vt_ref, q_ref, qx_ref, g1_ref, g2_ref, mp_ref, tail_ref, ktb_ref, vtb_ref,
         carry_ref) = refs
    else:
        (x_ref, n1_ref, wm_ref, wkvt_ref, wb0_ref, cw_ref, qn_ref, um2_ref, um1_ref,
         kt_ref, vt_ref, q_ref, qx_ref, g1_ref, g2_ref, mp_ref, tail_ref) = refs

    xn = _rms(x_ref[...], n1_ref[...]).astype(BF16)

    def proj(c0, width):
        return _dot(xn, wm_ref[:, c0:c0 + width])

    kvt = _dot_nt(wkvt_ref[...], xn)
    sbw = SB_HEADS * SB_HEAD_DIM
    kt_ref[...] = kvt[:sbw]
    vt_ref[...] = kvt[sbw:]
    if carry_conv:
        for c in range(tm // SB_TK):
            ktb_ref[c] = kvt[:sbw, c * SB_TK:(c + 1) * SB_TK].astype(BF16)
            vtb_ref[c] = kvt[sbw:, c * SB_TK:(c + 1) * SB_TK].astype(BF16)

    q_ref[...] = (proj(_C_Q, sbw) * (SB_HEAD_DIM ** -0.5)).astype(BF16)

    qx = proj(_C_QX, X_HEADS * X_HEAD_DIM)
    for h in range(X_HEADS):
        sl = slice(h * X_HEAD_DIM, (h + 1) * X_HEAD_DIM)
        qx_ref[:, sl] = _rms(qx[:, sl], qn_ref[...]).astype(BF16)

    u = proj(_C_CG, BRANCH_W) * proj(_C_UIN, BRANCH_W)
    if carry_conv:
        @pl.when(pl.program_id(0) % tiles_per_seq == 0)
        def _():
            carry_ref[...] = jnp.zeros_like(carry_ref)
        ext = jnp.concatenate([carry_ref[...], u], axis=0)
        um1 = pltpu.roll(ext, 1, 0)[SUBLANES:]
        um2 = pltpu.roll(ext, 2, 0)[SUBLANES:]
        carry_ref[...] = u[tm - SUBLANES:]
        tail_ref[...] = u[tm - SUBLANES:]
    else:
        um1 = um1_ref[...]
        um2 = um2_ref[...]
        tail_ref[...] = u
    cw = cw_ref[...]
    y_conv = proj(_C_BG, BRANCH_W) * (um2 * cw[0:1] + um1 * cw[1:2] + u * cw[2:3])

    g0 = jax.nn.sigmoid(proj(_C_GL, D_MODEL))
    mp_ref[...] = g0 * _dot(y_conv.astype(BF16), wb0_ref[...])
    g1_ref[...] = jax.nn.sigmoid(proj(_C_GL + D_MODEL, D_MODEL))
    g2_ref[...] = jax.nn.sigmoid(proj(_C_GL + 2 * D_MODEL, D_MODEL))


def inproj(x2d, norm1, w_main_bf, w_kvt_bf, wb0_bf, conv_w, q_norm, seq_len, tm, conv_state=None):
    n = x2d.shape[0]
    carry = conv_state is None
    nseq = n // seq_len if carry else 1
    slen = seq_len if carry else n
    tps = slen // tm
    sbw = SB_HEADS * SB_HEAD_DIM
    xw = X_HEADS * X_HEAD_DIM
    nkb = max(tm // SB_TK, 1)
    tail_rows = SUBLANES if carry else tm

    row = lambda w: pl.BlockSpec((tm, w), lambda i: (i, 0))
    kt_spec = pl.BlockSpec((None, sbw, tm), lambda i: (i // tps, 0, i % tps))
    ktb_spec = pl.BlockSpec((None, nkb, sbw, SB_TK), lambda i: (i // tps, i % tps, 0, 0))
    in_specs = [row(D_MODEL), _const_spec((1, D_MODEL)), _const_spec((D_MODEL, _W_MAIN)),
                _const_spec((2 * sbw, D_MODEL)), _const_spec((BRANCH_W, D_MODEL)),
                _const_spec((3, BRANCH_W)), _const_spec((1, X_HEAD_DIM))]
    args = [x2d, norm1.reshape(1, -1), w_main_bf, w_kvt_bf, wb0_bf, conv_w, q_norm.reshape(1, -1)]
    scratch = []
    if carry:
        scratch = [pltpu.VMEM((SUBLANES, BRANCH_W), F32)]
    else:
        in_specs += [row(BRANCH_W), row(BRANCH_W)]
        args += list(conv_state)
    out_specs = [kt_spec, kt_spec, row(sbw), row(xw),
                 row(D_MODEL), row(D_MODEL), row(D_MODEL),
                 pl.BlockSpec((None, tail_rows, BRANCH_W), lambda i: (i // tps, 0, 0))]
    out_shape = [jax.ShapeDtypeStruct((nseq, sbw, slen), F32), jax.ShapeDtypeStruct((nseq, sbw, slen), F32),
                 jax.ShapeDtypeStruct((n, sbw), BF16), jax.ShapeDtypeStruct((n, xw), BF16),
                 jax.ShapeDtypeStruct((n, D_MODEL), F32), jax.ShapeDtypeStruct((n, D_MODEL), F32),
                 jax.ShapeDtypeStruct((n, D_MODEL), F32),
                 jax.ShapeDtypeStruct((nseq, tail_rows, BRANCH_W), F32)]
    if carry:
        out_specs += [ktb_spec, ktb_spec]
        out_shape += [jax.ShapeDtypeStruct((nseq, slen // SB_TK, sbw, SB_TK), BF16)] * 2
    return pl.pallas_call(
        functools.partial(_inproj_kernel, carry, tps, tm),
        grid=(n // tm,),
        in_specs=in_specs, out_specs=out_specs, out_shape=out_shape,
        scratch_shapes=scratch,
        compiler_params=_params(("arbitrary",)),
        name="inproj_prompt" if carry else "inproj_sample",
    )(*args)


def _sbp_kernel(bias_ref, q_ref, kt_ref, vt_ref, o_ref, acc_ref, car_ref):
    pair = pl.program_id(1)
    qi = pl.program_id(2)
    tq, tk = SB_TQ, SB_TK
    lane = lax.broadcasted_iota(I32, (tq, LANES), 1)
    q = q_ref[...]
    zero = jnp.zeros_like(q)
    qh = (jnp.where(lane < SB_HEAD_DIM, q, zero), jnp.where(lane >= SB_HEAD_DIM, q, zero))
    r = lax.broadcasted_iota(I32, (tk, tk), 0)
    c = lax.broadcasted_iota(I32, (tk, tk), 1)
    tri = (r >= c).astype(BF16)
    qrow = lax.broadcasted_iota(I32, (tq, tk), 0)
    kcol = lax.broadcasted_iota(I32, (tq, tk), 1)
    acc_ref[...] = jnp.zeros_like(acc_ref)
    car_ref[...] = jnp.zeros_like(car_ref)

    def block(j, diag_sub):
        kt = kt_ref[j]
        vt = vt_ref[j]
        rs = slice(0, tq)
        if diag_sub is not None:
            rs = slice(diag_sub * tk, tq)
            causal = (kcol + diag_sub * tk < qrow)[rs]
        for h in range(2):
            z = _dot(qh[h][rs], kt) + bias_ref[2 * pair + h]
            sp = _softplus(z)
            if diag_sub is not None:
                sp = jnp.where(causal, sp, 0.0)
            car = car_ref[h, rs]
            cum = _dot(sp.astype(BF16), tri)
            a = jnp.exp(z - (cum + jnp.concatenate([car] * (tk // LANES), axis=1)))
            if diag_sub is not None:
                a = jnp.where(causal, a, 0.0)
            acc_ref[h, rs] += _dot_nt(a.astype(BF16), vt)
            car_ref[h, rs] = car + jnp.broadcast_to(cum[:, 0:1], car.shape)

    nsub = tq // tk
    for s in reversed(range(nsub)):
        block(qi * nsub + s, s)

    def body(t, carry):
        block(qi * nsub - 1 - 2 * t, None)
        block(qi * nsub - 2 - 2 * t, None)
        return carry

    lax.fori_loop(0, qi * (nsub // 2), body, 0)
    o_ref[...] = jnp.where(lane < SB_HEAD_DIM, acc_ref[0], acc_ref[1]).astype(o_ref.dtype)


def sb_prompt(q_bf, ktb, vtb, sb_bias, seq_len):
    n = q_bf.shape[0]
    nseq = n // seq_len
    nq = seq_len // SB_TQ
    nk = seq_len // SB_TK
    npair = SB_HEADS // 2
    grid_spec = pltpu.PrefetchScalarGridSpec(
        num_scalar_prefetch=0,
        grid=(nseq, npair, nq),
        in_specs=[pl.BlockSpec(memory_space=pltpu.SMEM),
                  pl.BlockSpec((SB_TQ, LANES), lambda b, p, i: (b * nq + i, p)),
                  pl.BlockSpec((None, nk, LANES, SB_TK), lambda b, p, i: (b, 0, p, 0)),
                  pl.BlockSpec((None, nk, LANES, SB_TK), lambda b, p, i: (b, 0, p, 0))],
        out_specs=pl.BlockSpec((SB_TQ, LANES), lambda b, p, i: (b * nq + i, p)),
        scratch_shapes=[pltpu.VMEM((2, SB_TQ, LANES), F32), pltpu.VMEM((2, SB_TQ, LANES), F32)],
    )
    return pl.pallas_call(
        _sbp_kernel, grid_spec=grid_spec,
        out_shape=jax.ShapeDtypeStruct((n, SB_HEADS * SB_HEAD_DIM), BF16),
        compiler_params=_params(("arbitrary", "arbitrary", "arbitrary")),
        name="sb_prompt",
    )(sb_bias.astype(F32), q_bf, ktb, vtb)


def _xp_kernel(qx_ref, mk_ref, mv_ref, o_ref):
    for h in range(X_HEADS):
        sl = slice(h * X_HEAD_DIM, (h + 1) * X_HEAD_DIM)
        s = _dot_nt(qx_ref[:, sl], mk_ref[:, sl]) * (X_HEAD_DIM ** -0.5)
        e = jnp.exp(s - jnp.max(s, axis=-1, keepdims=True))
        p = e / jnp.sum(e, axis=-1, keepdims=True)
        o_ref[:, sl] = _dot(p.astype(BF16), mv_ref[:, sl]).astype(o_ref.dtype)


def xattn_prompt(qx_bf, mk_bf, mv_bf, seq_len, n_mem, tm):
    n = qx_bf.shape[0]
    tps = seq_len // tm
    xw = X_HEADS * X_HEAD_DIM
    return pl.pallas_call(
        _xp_kernel, grid=(n // tm,),
        in_specs=[pl.BlockSpec((tm, xw), lambda i: (i, 0)),
                  pl.BlockSpec((n_mem, xw), lambda i: (i // tps, 0)),
                  pl.BlockSpec((n_mem, xw), lambda i: (i // tps, 0))],
        out_specs=pl.BlockSpec((tm, xw), lambda i: (i, 0)),
        out_shape=jax.ShapeDtypeStruct((n, xw), BF16),
        compiler_params=_params(("arbitrary",)),
        name="xattn_prompt",
    )(qx_bf, mk_bf, mv_bf)


def _split_bf16(x):
    hi = x.astype(BF16)
    return hi, (x - hi.astype(F32)).astype(BF16)


def _sbs_kernel(new_key_visible, npages, pt_ref, bias_ref, qbd_ref, kn_ref, vn_ref, *refs):
    del pt_ref
    k_refs, v_refs, o_ref = refs[:npages], refs[npages:2 * npages], refs[2 * npages]
    b = pl.program_id(0)
    psz = PAGE_SIZE
    sbw = SB_HEADS * SB_HEAD_DIM
    rows = npages * SB_HEADS
    head_bits = SB_HEADS.bit_length() - 1
    qbd = qbd_ref[0]
    bias = bias_ref[...]

    mine = lax.broadcasted_iota(I32, (SB_HEADS, kn_ref.shape[1]), 1) == b
    zs = _dot(qbd, kn_ref[...].astype(BF16)) + bias[:, 0:1]
    seen = jnp.logical_and(mine, new_key_visible)
    sp0 = jnp.where(seen, _softplus(zs), 0.0)
    a0 = jnp.where(seen, jnp.exp(zs - sp0), 0.0)
    sp0_tot = jnp.sum(sp0, axis=-1, keepdims=True)
    a0_rows = jnp.concatenate(
        [jnp.broadcast_to(a0[h:h + 1], (SB_HEAD_DIM, a0.shape[1])) for h in range(SB_HEADS)], axis=0)
    new = jnp.sum(a0_rows * vn_ref[...], axis=-1, keepdims=True)

    z = jnp.concatenate([_dot(qbd, k_refs[j][...].astype(BF16)) + bias for j in range(npages)], axis=0)
    sp = _softplus(z)
    r = lax.broadcasted_iota(I32, (psz, psz), 0)
    c = lax.broadcasted_iota(I32, (psz, psz), 1)
    tri = (r >= c).astype(BF16)
    hi, lo = _split_bf16(sp)
    cum = _dot(hi, tri) + _dot(lo, tri)
    tot = jnp.broadcast_to(cum[:, 0:1], (rows, psz))
    rr = lax.broadcasted_iota(I32, (rows, rows), 0)
    cc = lax.broadcasted_iota(I32, (rows, rows), 1)
    later = jnp.logical_and((cc >> head_bits) > (rr >> head_bits),
                            (cc & (SB_HEADS - 1)) == (rr & (SB_HEADS - 1))).astype(BF16)
    th, tl = _split_bf16(tot)
    carry = _dot(later, th) + _dot(later, tl) + jnp.concatenate([sp0_tot] * npages, axis=0)
    a = jnp.exp(z - cum - carry)

    accs = []
    for h in range(SB_HEADS):
        hs = slice(h * SB_HEAD_DIM, (h + 1) * SB_HEAD_DIM)
        acc = jnp.zeros((SB_HEAD_DIM, psz), F32)
        for j in range(npages):
            rj = j * SB_HEADS + h
            acc = acc + jnp.broadcast_to(a[rj:rj + 1], (SB_HEAD_DIM, psz)) * v_refs[j][hs, :]
        accs.append(acc)
    acc = jnp.concatenate(accs, axis=0)
    lane0 = lax.broadcasted_iota(I32, (sbw, psz), 1) == 0
    acc = acc + jnp.where(lane0, new, 0.0)
    hi2, lo2 = _split_bf16(acc)
    ones = jnp.ones((SUBLANES, psz), BF16)
    y = _dot_nt(ones, hi2) + _dot_nt(ones, lo2)
    o_ref[pl.ds(b, 1), :] = y[0:1]


def sb_sample(q_s, kn_t, vn_t, cache_k_t, cache_v_t, page_table, sb_bias, new_key_visible):
    nb, npages = page_table.shape
    sbw = SB_HEADS * SB_HEAD_DIM
    head_of_col = jnp.arange(sbw) // SB_HEAD_DIM
    sel = head_of_col[None, :] == jnp.arange(SB_HEADS)[:, None]
    qbd = jnp.where(sel[None], q_s[:, None, :], 0.0).astype(BF16)
    assert PAGE_SIZE == LANES
    bias = jnp.broadcast_to(sb_bias.astype(F32)[:, None], (SB_HEADS, LANES))
    page = lambda j: pl.BlockSpec((None, sbw, PAGE_SIZE), lambda b, pt: (pt[b, j], 0, 0))
    grid_spec = pltpu.PrefetchScalarGridSpec(
        num_scalar_prefetch=1,
        grid=(nb,),
        in_specs=[pl.BlockSpec((SB_HEADS, LANES), lambda b, pt: (0, 0)),
                  pl.BlockSpec((1, SB_HEADS, sbw), lambda b, pt: (b, 0, 0)),
                  pl.BlockSpec((sbw, nb), lambda b, pt: (0, 0)),
                  pl.BlockSpec((sbw, nb), lambda b, pt: (0, 0))]
                 + [page(j) for j in range(npages)] * 2,
        out_specs=pl.BlockSpec((nb, sbw), lambda b, pt: (0, 0)),
    )
    return pl.pallas_call(
        functools.partial(_sbs_kernel, new_key_visible, npages), grid_spec=grid_spec,
        out_shape=jax.ShapeDtypeStruct((nb, sbw), F32),
        compiler_params=_params(("arbitrary",)),
        name="sb_sample",
    )(page_table, bias, qbd, kn_t, vn_t, *([cache_k_t] * npages), *([cache_v_t] * npages))


def _xs_kernel(qx_ref, mk_ref, mv_ref, o_ref):
    step = pl.program_id(0)
    rows = N_MEM * X_HEADS
    own = ((lax.broadcasted_iota(I32, (SUBLANES, rows), 1) & (X_HEADS - 1))
           == lax.broadcasted_iota(I32, (SUBLANES, rows), 0))
    for t in range(XS_TOKENS):
        mk = mk_ref[t].astype(BF16)
        mv = mv_ref[t].astype(BF16)
        s = _dot_nt(qx_ref[t], mk) * (X_HEAD_DIM ** -0.5)
        s = jnp.where(own, s, -jnp.inf)
        m = jnp.max(s, axis=-1, keepdims=True)
        m = jnp.where(m == -jnp.inf, 0.0, m)
        e = jnp.exp(s - m)
        den = jnp.sum(e, axis=-1, keepdims=True)
        p = e / jnp.where(den == 0.0, 1.0, den)
        y = _dot(p.astype(BF16), mv)
        o_ref[pl.ds(step * XS_TOKENS + t, 1)] = y[None, 0:X_HEADS]


def xattn_sample(qx_bf, cache_mem_k, cache_mem_v):
    nb = qx_bf.shape[0]
    rows = N_MEM * X_HEADS
    mk = cache_mem_k.reshape(nb, rows, X_HEAD_DIM)
    mv = cache_mem_v.reshape(nb, rows, X_HEAD_DIM)
    qx = qx_bf.reshape(nb, X_HEADS, X_HEAD_DIM)
    qx = jnp.concatenate([qx, jnp.zeros((nb, SUBLANES - X_HEADS, X_HEAD_DIM), BF16)], axis=1)
    assert nb % XS_TOKENS == 0
    out = pl.pallas_call(
        _xs_kernel, grid=(nb // XS_TOKENS,),
        in_specs=[pl.BlockSpec((XS_TOKENS, SUBLANES, X_HEAD_DIM), lambda b: (b, 0, 0)),
                  pl.BlockSpec((XS_TOKENS, rows, X_HEAD_DIM), lambda b: (b, 0, 0)),
                  pl.BlockSpec((XS_TOKENS, rows, X_HEAD_DIM), lambda b: (b, 0, 0))],
        out_specs=pl.BlockSpec((nb, X_HEADS, X_HEAD_DIM), lambda b: (0, 0, 0)),
        out_shape=jax.ShapeDtypeStruct((nb, X_HEADS, X_HEAD_DIM), F32),
        compiler_params=_params(("arbitrary",)),
        name="xattn_sample",
    )(qx, mk, mv)
    return out.reshape(nb, X_HEADS * X_HEAD_DIM)


_R_E1, _R_E2, _R_R1, _R_R2, _R_G1, _R_G2 = 0, 1, 2, 3, 4, 5
_GROUP_LANE0 = N_EXPERTS


def _merge_kernel(tm, x_ref, mp_ref, ysb_ref, yx_ref, g1_ref, g2_ref, wb1_ref, wb2_ref, wo_ref,
                  n2_ref, wr_ref, br_ref, cin_ref, x1_ref, h_ref, route_ref, route_t_ref, cout_ref, cnt_ref):
    @pl.when(pl.program_id(0) == 0)
    def _():
        cnt_ref[...] = cin_ref[...]

    m = (mp_ref[...] + g1_ref[...] * _dot(ysb_ref[...].astype(BF16), wb1_ref[...])
         + g2_ref[...] * _dot(yx_ref[...].astype(BF16), wb2_ref[...]))
    x1 = x_ref[...] + _dot(m.astype(BF16), wo_ref[...])
    x1_ref[...] = x1
    hn = _rms(x1, n2_ref[...])
    h_ref[...] = hn
    logits = _dot(hn.astype(BF16), wr_ref[...]) + br_ref[...]

    lane = lax.broadcasted_iota(I32, (tm, LANES), 1).astype(F32)
    big = float(LANES)
    ninf = -jnp.inf

    def first_argmax(v):
        mx = jnp.max(v, axis=-1, keepdims=True)
        idx = jnp.min(jnp.where(v == mx, lane, big), axis=-1, keepdims=True)
        return mx, idx

    gl = jnp.where((lane >= _GROUP_LANE0) & (lane < _GROUP_LANE0 + N_GROUPS), logits, ninf)
    gmax, glane = first_argmax(gl)
    gprob = 1.0 / jnp.sum(jnp.exp(gl - gmax), axis=-1, keepdims=True)
    e0 = (glane - _GROUP_LANE0) * EXPERTS_PER_GROUP
    el = jnp.where((lane >= e0) & (lane < e0 + EXPERTS_PER_GROUP), logits, ninf)
    m1, i1 = first_argmax(el)
    m2, i2 = first_argmax(jnp.where(lane == i1, ninf, el))
    e2 = jnp.exp(m2 - m1)
    gate1 = gprob / (1.0 + e2)
    gate2 = gprob * e2 / (1.0 + e2)

    oh1 = (lane == i1).astype(F32)
    oh2 = (lane == i2).astype(F32)
    cnt = oh1 + oh2
    r = lax.broadcasted_iota(I32, (tm, tm), 0)
    c = lax.broadcasted_iota(I32, (tm, tm), 1)
    before = _dot((c < r).astype(BF16), cnt.astype(BF16)) + cnt_ref[...]
    rank1 = jnp.sum(oh1 * before, axis=-1, keepdims=True)
    rank2 = jnp.sum(oh2 * before, axis=-1, keepdims=True)
    cnt_ref[...] = cnt_ref[...] + jnp.sum(cnt, axis=0, keepdims=True)
    cout_ref[...] = cnt_ref[...]

    rec = jnp.zeros((tm, LANES), F32)
    for ln, val in ((_R_E1, i1.astype(F32)), (_R_E2, i2.astype(F32)), (_R_R1, rank1), (_R_R2, rank2),
                    (_R_G1, gate1), (_R_G2, gate2)):
        rec = jnp.where(lane == ln, val, rec)
    route_ref[...] = rec
    route_t_ref[...] = rec.T[0:SUBLANES]


def merge(x2d, mp, ysb, yx, g1, g2, wb1_bf, wb2_bf, wo_bf, norm2, w_route_bf, b_route, cnt_in, tm):
    n = x2d.shape[0]
    row = lambda w: pl.BlockSpec((tm, w), lambda i: (i, 0))
    return pl.pallas_call(
        functools.partial(_merge_kernel, tm), grid=(n // tm,),
        in_specs=[row(D_MODEL), row(D_MODEL), row(BRANCH_W), row(BRANCH_W), row(D_MODEL), row(D_MODEL),
                  _const_spec((BRANCH_W, D_MODEL)), _const_spec((BRANCH_W, D_MODEL)),
                  _const_spec((D_MODEL, D_MODEL)), _const_spec((1, D_MODEL)),
                  _const_spec((D_MODEL, LANES)), _const_spec((1, LANES)), _const_spec((1, LANES))],
        out_specs=[row(D_MODEL), row(D_MODEL), row(LANES), pl.BlockSpec((SUBLANES, tm), lambda i: (0, i)),
                   pl.BlockSpec((1, LANES), lambda i: (0, 0))],
        out_shape=[jax.ShapeDtypeStruct((n, D_MODEL), F32), jax.ShapeDtypeStruct((n, D_MODEL), F32),
                   jax.ShapeDtypeStruct((n, LANES), F32), jax.ShapeDtypeStruct((SUBLANES, n), F32),
                   jax.ShapeDtypeStruct((1, LANES), F32)],
        scratch_shapes=[pltpu.VMEM((1, LANES), F32)],
        compiler_params=_params(("arbitrary",)),
        name="merge",
    )(x2d, mp, ysb, yx, g1, g2, wb1_bf, wb2_bf, wo_bf, norm2.reshape(1, -1), w_route_bf, b_route, cnt_in)


def _row_copy(src, src_row, dst, dst_row, sem):
    return pltpu.make_async_copy(src.at[pl.ds(src_row, 1)], dst.at[pl.ds(dst_row, 1)], sem)


def _dispatch_kernel(tm, dest_ref, h_ref, buf_in_ref, buf_ref, sem):
    del buf_in_ref

    def issue(t, carry):
        _row_copy(h_ref, t, buf_ref, dest_ref[0, 0, t], sem).start(priority=0)
        _row_copy(h_ref, t, buf_ref, dest_ref[0, 0, tm + t], sem).start(priority=1)
        return carry

    lax.fori_loop(0, tm, issue, 0, unroll=ROW_DMA_UNROLL)
    for _ in range(2):
        pltpu.make_async_copy(h_ref, buf_ref.at[pl.ds(0, tm)], sem).wait()


def _slot_tiles(dest, tm):
    nt = dest.shape[1] // tm
    return dest.reshape(2, nt, tm).transpose(1, 0, 2).reshape(nt, 1, 2 * tm)


def dispatch(h, dest, buf_zero, tm):
    n = h.shape[0]
    nt = n // tm
    return pl.pallas_call(
        functools.partial(_dispatch_kernel, tm), grid=(nt,),
        in_specs=[pl.BlockSpec((1, 1, 2 * tm), lambda i: (i, 0, 0), memory_space=pltpu.SMEM),
                  pl.BlockSpec((tm, D_MODEL), lambda i: (i, 0)),
                  pl.BlockSpec(memory_space=pl.ANY)],
        out_specs=pl.BlockSpec(memory_space=pl.ANY),
        out_shape=jax.ShapeDtypeStruct(buf_zero.shape, buf_zero.dtype),
        scratch_shapes=[pltpu.SemaphoreType.DMA(())],
        input_output_aliases={2: 0},
        compiler_params=_params(("arbitrary",)),
        name="dispatch",
    )(_slot_tiles(dest, tm), h, buf_zero)


def _experts_kernel(be_ref, nu_ref, x_ref, w1_ref, w3_ref, w2_ref, y_ref):
    del be_ref

    @pl.when(pl.program_id(0) < nu_ref[0])
    def _():
        xb = x_ref[...].astype(BF16)
        h1 = _dot(xb, w1_ref[...].astype(BF16))
        h3 = _dot(xb, w3_ref[...].astype(BF16))
        act = (h1 * jax.nn.sigmoid(h1)) * h3
        y_ref[...] = _dot(act.astype(BF16), w2_ref[...].astype(BF16))

    @pl.when(pl.program_id(0) >= nu_ref[0])
    def _():
        y_ref[...] = jnp.zeros_like(y_ref)


def experts(buf, block_e, n_used, w1, w3, w2):
    nblk = buf.shape[0] // MOE_BM
    clamp = lambda i, nu: jnp.minimum(i, nu[0] - 1)
    grid_spec = pltpu.PrefetchScalarGridSpec(
        num_scalar_prefetch=2,
        grid=(nblk,),
        in_specs=[pl.BlockSpec((MOE_BM, D_MODEL), lambda i, be, nu: (clamp(i, nu), 0)),
                  pl.BlockSpec((None, D_MODEL, D_EXPERT), lambda i, be, nu: (be[i], 0, 0)),
                  pl.BlockSpec((None, D_MODEL, D_EXPERT), lambda i, be, nu: (be[i], 0, 0)),
                  pl.BlockSpec((None, D_EXPERT, D_MODEL), lambda i, be, nu: (be[i], 0, 0))],
        out_specs=pl.BlockSpec((MOE_BM, D_MODEL), lambda i, be, nu: (i, 0)),
    )
    return pl.pallas_call(
        _experts_kernel, grid_spec=grid_spec,
        out_shape=jax.ShapeDtypeStruct(buf.shape, F32),
        compiler_params=_params(("arbitrary",)),
        name="experts",
    )(block_e, n_used, buf, w1, w3, w2)


def _combine_kernel(tm, dest_ref, x1_ref, route_ref, ybuf_ref, o_ref, rows_ref, sem):
    def issue(t, carry):
        _row_copy(ybuf_ref, dest_ref[0, 0, t], rows_ref.at[0], t, sem).start(priority=0)
        _row_copy(ybuf_ref, dest_ref[0, 0, tm + t], rows_ref.at[1], t, sem).start(priority=1)
        return carry

    lax.fori_loop(0, tm, issue, 0, unroll=ROW_DMA_UNROLL)
    for k in range(2):
        pltpu.make_async_copy(ybuf_ref.at[pl.ds(0, tm)], rows_ref.at[k], sem).wait()
    route = route_ref[...]
    o_ref[...] = (x1_ref[...] + route[:, _R_G1:_R_G1 + 1] * rows_ref[0]
                  + route[:, _R_G2:_R_G2 + 1] * rows_ref[1])


def combine(x1, route, dest, ybuf, tm):
    n = x1.shape[0]
    nt = n // tm
    return pl.pallas_call(
        functools.partial(_combine_kernel, tm), grid=(nt,),
        in_specs=[pl.BlockSpec((1, 1, 2 * tm), lambda i: (i, 0, 0), memory_space=pltpu.SMEM),
                  pl.BlockSpec((tm, D_MODEL), lambda i: (i, 0)),
                  pl.BlockSpec((tm, LANES), lambda i: (i, 0)),
                  pl.BlockSpec(memory_space=pl.ANY)],
        out_specs=pl.BlockSpec((tm, D_MODEL), lambda i: (i, 0)),
        out_shape=jax.ShapeDtypeStruct((n, D_MODEL), F32),
        scratch_shapes=[pltpu.VMEM((2, tm, D_MODEL), F32), pltpu.SemaphoreType.DMA(())],
        compiler_params=_params(("arbitrary",)),
        name="combine",
    )(_slot_tiles(dest, tm), x1, route, ybuf)


def kernel(x_prompt, x_sample, mem_prompt, cache_k, cache_v, cache_mem_k, cache_mem_v, state_conv, page_table, norm1, w_in, conv_w, sb_bias, mem_norm, w_mem_kv, q_norm, k_norm, w_branch, w_o, norm2, w_group, b_group, w_router, b_router, w1, w3, w2):
    nb, seq, _ = x_prompt.shape
    ns, dec_seq, _ = x_sample.shape
    assert dec_seq == 1 and seq % SB_TQ == 0 and SB_TQ % (2 * SB_TK) == 0
    n_mem = mem_prompt.shape[1]
    sbw = SB_HEADS * SB_HEAD_DIM
    xw = X_HEADS * X_HEAD_DIM
    past = page_table.shape[1] * PAGE_SIZE

    assert w_in.shape[1] == _W_MAIN and _C_QX == _C_K + 2 * sbw and _C_GL == _C_QX + xw
    w_main = w_in.astype(BF16)
    w_kvt = kv_weight_t(w_in)
    wb = w_branch.astype(BF16)
    wo_bf = w_o.astype(BF16)
    w_route = jnp.zeros((D_MODEL, LANES), F32)
    w_route = w_route.at[:, :N_EXPERTS].set(w_router).at[:, N_EXPERTS:N_EXPERTS + N_GROUPS].set(w_group)
    b_route = jnp.zeros((1, LANES), F32)
    b_route = b_route.at[0, :N_EXPERTS].set(b_router).at[0, N_EXPERTS:N_EXPERTS + N_GROUPS].set(b_group)
    w_route = w_route.astype(BF16)

    xp = x_prompt.reshape(nb * seq, D_MODEL)
    mk_p, mv_p = memkv(mem_prompt.reshape(nb * n_mem, D_MODEL), mem_norm, w_mem_kv.astype(BF16), k_norm, n_mem)
    (kt_p, vt_p, q_p, qx_p, g1_p, g2_p, mp_p, tail_p, ktb_p, vtb_p) = inproj(
        xp, norm1, w_main, w_kvt, wb[0], conv_w, q_norm, seq, 512)
    ysb_p = sb_prompt(q_p, ktb_p, vtb_p, sb_bias, seq)
    yx_p = xattn_prompt(qx_p, mk_p.astype(BF16), mv_p.astype(BF16), seq, n_mem, 512)

    xs = x_sample.reshape(ns, D_MODEL)
    (kt_s, vt_s, q_s, qx_s, g1_s, g2_s, mp_s, u_s) = inproj(
        xs, norm1, w_main, w_kvt, wb[0], conv_w, q_norm, ns, ns,
        conv_state=(state_conv[:, 0], state_conv[:, 1]))
    ck_t = cache_k.transpose(0, 2, 3, 1).reshape(cache_k.shape[0], sbw, PAGE_SIZE)
    cv_t = cache_v.transpose(0, 2, 3, 1).reshape(cache_v.shape[0], sbw, PAGE_SIZE)
    ysb_s = sb_sample(q_s.astype(F32), kt_s[0], vt_s[0], ck_t, cv_t, page_table, sb_bias,
                      new_key_visible=(past < past))
    yx_s = xattn_sample(qx_s, cache_mem_k, cache_mem_v)

    cnt0 = jnp.zeros((1, LANES), F32)
    x1_p, h_p, route_p, rt_p, cnt1 = merge(xp, mp_p, ysb_p, yx_p, g1_p, g2_p, wb[1], wb[2], wo_bf, norm2,
                                           w_route, b_route, cnt0, 512)
    x1_s, h_s, route_s, rt_s, cnt2 = merge(xs, mp_s, ysb_s, yx_s, g1_s, g2_s, wb[1], wb[2], wo_bf, norm2,
                                           w_route, b_route, cnt1, ns)

    n_slots = 2 * (nb * seq + ns)
    nblk = -(-(n_slots + N_EXPERTS * (MOE_BM - 1)) // MOE_BM)
    counts = cnt2[0, :N_EXPERTS].astype(I32)
    padded = (counts + MOE_BM - 1) // MOE_BM * MOE_BM
    ends = jnp.cumsum(padded)
    starts = ends - padded
    n_used = (ends[-1] // MOE_BM).astype(I32).reshape(1)
    blk_start = jnp.arange(nblk, dtype=I32) * MOE_BM
    block_e = jnp.sum(blk_start[:, None] >= ends[None, :], axis=1).astype(I32)
    block_e = jnp.minimum(block_e, block_e[jnp.maximum(n_used[0] - 1, 0)])

    def dests(route_t):
        e = route_t[_R_E1:_R_E2 + 1].astype(I32)
        rk = route_t[_R_R1:_R_R2 + 1].astype(I32)
        first = jnp.zeros_like(e)
        for x in range(N_EXPERTS):
            first = jnp.where(e == x, starts[x], first)
        return first + rk

    dest_p, dest_s = dests(rt_p), dests(rt_s)
    buf = jnp.zeros((nblk * MOE_BM, D_MODEL), F32)
    buf = dispatch(h_p, dest_p, buf, 512)
    buf = dispatch(h_s, dest_s, buf, ns)
    ybuf = experts(buf, block_e, n_used, w1, w3, w2)
    y_p = combine(x1_p, route_p, dest_p, ybuf, 256)
    y_s = combine(x1_s, route_s, dest_s, ybuf, ns)

    def heads_last(t, b, s):
        return t.reshape(b, SB_HEADS, SB_HEAD_DIM, s).transpose(0, 3, 1, 2)

    conv_prompt = tail_p[:, SUBLANES - 2:, :]
    conv_sample = jnp.stack([state_conv[:, 1], u_s[0]], axis=1)
    return (y_p.reshape(nb, seq, D_MODEL), y_s.reshape(ns, 1, D_MODEL),
            heads_last(kt_p, nb, seq), heads_last(vt_p, nb, seq), conv_prompt,
            mk_p.reshape(nb, n_mem, X_HEADS, X_HEAD_DIM), mv_p.reshape(nb, n_mem, X_HEADS, X_HEAD_DIM),
            heads_last(kt_s, 1, ns).reshape(ns, 1, SB_HEADS, SB_HEAD_DIM),
            heads_last(vt_s, 1, ns).reshape(ns, 1, SB_HEADS, SB_HEAD_DIM),
            conv_sample)
```

```python
import functools

import jax
import jax.numpy as jnp
from jax import lax
from jax.experimental import pallas as pl
from jax.experimental.pallas import tpu as pltpu

F32 = jnp.float32
BF16 = jnp.bfloat16
I32 = jnp.int32

EPS = 1e-6
D_MODEL = 1024
BRANCH_W = 512
SB_HEADS = 8
SB_HEAD_DIM = 64
X_HEADS = 4
X_HEAD_DIM = 128
N_MEM = 256
N_GROUPS = 4
EXPERTS_PER_GROUP = 8
N_EXPERTS = N_GROUPS * EXPERTS_PER_GROUP
D_EXPERT = 512
PAGE_SIZE = 128

LANES = 128
SUBLANES = 8
VMEM_LIMIT = 56 * 1024 * 1024

SB_TK = 256
SB_TQ = 8 * SB_TK
MOE_BM = 256
XS_TOKENS = 4
ROW_DMA_UNROLL = 8


def _dot(a, b):
    return jnp.dot(a, b, preferred_element_type=F32)


def _dot_nt(a, b):
    return lax.dot_general(a, b, (((1,), (1,)), ((), ())), preferred_element_type=F32)


def _rms(x, g):
    return x * lax.rsqrt(jnp.mean(x * x, axis=-1, keepdims=True) + EPS) * g


def _softplus(z):
    return jnp.maximum(z, 0.0) + jnp.log(1.0 + jnp.exp(-jnp.abs(z)))


def _params(sem, vmem=VMEM_LIMIT):
    return pltpu.CompilerParams(dimension_semantics=sem, vmem_limit_bytes=vmem)


def _const_spec(shape):
    nd = len(shape)
    return pl.BlockSpec(shape, lambda *_: (0,) * nd, pipeline_mode=pl.Buffered(1))


def _memkv_kernel(mem_ref, g_ref, w_ref, kn_ref, mk_ref, mv_ref):
    xn = _rms(mem_ref[...], g_ref[...]).astype(BF16)
    kv = _dot(xn, w_ref[...])
    xw = X_HEADS * X_HEAD_DIM
    for h in range(X_HEADS):
        sl = slice(h * X_HEAD_DIM, (h + 1) * X_HEAD_DIM)
        mk_ref[:, sl] = _rms(kv[:, sl], kn_ref[...])
    mv_ref[...] = kv[:, xw:]


def memkv(mem2d, mem_norm, w_mem_kv_bf, k_norm, n_mem):
    n = mem2d.shape[0]
    xw = X_HEADS * X_HEAD_DIM
    return pl.pallas_call(
        _memkv_kernel,
        grid=(n // n_mem,),
        in_specs=[pl.BlockSpec((n_mem, D_MODEL), lambda i: (i, 0)),
                  _const_spec((1, D_MODEL)),
                  _const_spec((D_MODEL, 2 * xw)),
                  _const_spec((1, X_HEAD_DIM))],
        out_specs=[pl.BlockSpec((n_mem, xw), lambda i: (i, 0)),
                   pl.BlockSpec((n_mem, xw), lambda i: (i, 0))],
        out_shape=[jax.ShapeDtypeStruct((n, xw), F32), jax.ShapeDtypeStruct((n, xw), F32)],
        compiler_params=_params(("arbitrary",)),
        name="memkv",
    )(mem2d, mem_norm.reshape(1, -1), w_mem_kv_bf, k_norm.reshape(1, -1))


_C_UIN, _C_CG, _C_BG, _C_Q, _C_K, _C_QX, _C_GL = 0, 512, 1024, 1536, 2048, 3072, 3584
_W_MAIN = _C_GL + 3 * D_MODEL


def _kvt_kernel(w_ref, o_ref):
    o_ref[...] = w_ref[...].T.astype(BF16)


def kv_weight_t(w_in):
    kvw = 2 * SB_HEADS * SB_HEAD_DIM
    assert _C_K % kvw == 0
    return pl.pallas_call(
        _kvt_kernel, grid=(1,),
        in_specs=[pl.BlockSpec((D_MODEL, kvw), lambda i: (0, _C_K // kvw))],
        out_specs=pl.BlockSpec((kvw, D_MODEL), lambda i: (0, 0)),
        out_shape=jax.ShapeDtypeStruct((kvw, D_MODEL), BF16),
        compiler_params=_params(("arbitrary",)),
        name="kv_weight_t",
    )(w_in)


def _inproj_kernel(carry_conv, tiles_per_seq, tm, *refs):
    if carry_conv:
        (x_ref, n1_ref, wm_ref, wkvt_ref, wb0_ref, cw_ref, qn_ref,
         kt_ref, vt_ref, q_ref, qx_ref, g1_ref, g2_ref, mp_ref, tail_ref, ktb_ref, vtb_ref,
         carry_ref) = refs
    else:
        (x_ref, n1_ref, wm_ref, wkvt_ref, wb0_ref, cw_ref, qn_ref, um2_ref, um1_ref,
         kt_ref, vt_ref, q_ref, qx_ref, g1_ref, g2_ref, mp_ref, tail_ref) = refs

    xn = _rms(x_ref[...], n1_ref[...]).astype(BF16)

    def proj(c0, width):
        return _dot(xn, wm_ref[:, c0:c0 + width])

    kvt = _dot_nt(wkvt_ref[...], xn)
    sbw = SB_HEADS * SB_HEAD_DIM
    kt_ref[...] = kvt[:sbw]
    vt_ref[...] = kvt[sbw:]
    if carry_conv:
        for c in range(tm // SB_TK):
            ktb_ref[c] = kvt[:sbw, c * SB_TK:(c + 1) * SB_TK].astype(BF16)
            vtb_ref[c] = kvt[sbw:, c * SB_TK:(c + 1) * SB_TK].astype(BF16)

    q_ref[...] = (proj(_C_Q, sbw) * (SB_HEAD_DIM ** -0.5)).astype(BF16)

    qx = proj(_C_QX, X_HEADS * X_HEAD_DIM)
    for h in range(X_HEADS):
        sl = slice(h * X_HEAD_DIM, (h + 1) * X_HEAD_DIM)
        qx_ref[:, sl] = _rms(qx[:, sl], qn_ref[...]).astype(BF16)

    u = proj(_C_CG, BRANCH_W) * proj(_C_UIN, BRANCH_W)
    if carry_conv:
        @pl.when(pl.program_id(0) % tiles_per_seq == 0)
        def _():
            carry_ref[...] = jnp.zeros_like(carry_ref)
        ext = jnp.concatenate([carry_ref[...], u], axis=0)
        um1 = pltpu.roll(ext, 1, 0)[SUBLANES:]
        um2 = pltpu.roll(ext, 2, 0)[SUBLANES:]
        carry_ref[...] = u[tm - SUBLANES:]
        tail_ref[...] = u[tm - SUBLANES:]
    else:
        um1 = um1_ref[...]
        um2 = um2_ref[...]
        tail_ref[...] = u
    cw = cw_ref[...]
    y_conv = proj(_C_BG, BRANCH_W) * (um2 * cw[0:1] + um1 * cw[1:2] + u * cw[2:3])

    g0 = jax.nn.sigmoid(proj(_C_GL, D_MODEL))
    mp_ref[...] = g0 * _dot(y_conv.astype(BF16), wb0_ref[...])
    g1_ref[...] = jax.nn.sigmoid(proj(_C_GL + D_MODEL, D_MODEL))
    g2_ref[...] = jax.nn.sigmoid(proj(_C_GL + 2 * D_MODEL, D_MODEL))


def inproj(x2d, norm1, w_main_bf, w_kvt_bf, wb0_bf, conv_w, q_norm, seq_len, tm, conv_state=None):
    n = x2d.shape[0]
    carry = conv_state is None
    nseq = n // seq_len if carry else 1
    slen = seq_len if carry else n
    tps = slen // tm
    sbw = SB_HEADS * SB_HEAD_DIM
    xw = X_HEADS * X_HEAD_DIM
    nkb = max(tm // SB_TK, 1)
    tail_rows = SUBLANES if carry else tm

    row = lambda w: pl.BlockSpec((tm, w), lambda i: (i, 0))
    kt_spec = pl.BlockSpec((None, sbw, tm), lambda i: (i // tps, 0, i % tps))
    ktb_spec = pl.BlockSpec((None, nkb, sbw, SB_TK), lambda i: (i // tps, i % tps, 0, 0))
    in_specs = [row(D_MODEL), _const_spec((1, D_MODEL)), _const_spec((D_MODEL, _W_MAIN)),
                _const_spec((2 * sbw, D_MODEL)), _const_spec((BRANCH_W, D_MODEL)),
                _const_spec((3, BRANCH_W)), _const_spec((1, X_HEAD_DIM))]
    args = [x2d, norm1.reshape(1, -1), w_main_bf, w_kvt_bf, wb0_bf, conv_w, q_norm.reshape(1, -1)]
    scratch = []
    if carry:
        scratch = [pltpu.VMEM((SUBLANES, BRANCH_W), F32)]
    else:
        in_specs += [row(BRANCH_W), row(BRANCH_W)]
        args += list(conv_state)
    out_specs = [kt_spec, kt_spec, row(sbw), row(xw),
                 row(D_MODEL), row(D_MODEL), row(D_MODEL),
                 pl.BlockSpec((None, tail_rows, BRANCH_W), lambda i: (i // tps, 0, 0))]
    out_shape = [jax.ShapeDtypeStruct((nseq, sbw, slen), F32), jax.ShapeDtypeStruct((nseq, sbw, slen), F32),
                 jax.ShapeDtypeStruct((n, sbw), BF16), jax.ShapeDtypeStruct((n, xw), BF16),
                 jax.ShapeDtypeStruct((n, D_MODEL), F32), jax.ShapeDtypeStruct((n, D_MODEL), F32),
                 jax.ShapeDtypeStruct((n, D_MODEL), F32),
                 jax.ShapeDtypeStruct((nseq, tail_rows, BRANCH_W), F32)]
    if carry:
        out_specs += [ktb_spec, ktb_spec]
        out_shape += [jax.ShapeDtypeStruct((nseq, slen // SB_TK, sbw, SB_TK), BF16)] * 2
    return pl.pallas_call(
        functools.partial(_inproj_kernel, carry, tps, tm),
        grid=(n // tm,),
        in_specs=in_specs, out_specs=out_specs, out_shape=out_shape,
        scratch_shapes=scratch,
        compiler_params=_params(("arbitrary",)),
        name="inproj_prompt" if carry else "inproj_sample",
    )(*args)


def _sbp_kernel(bias_ref, q_ref, kt_ref, vt_ref, o_ref, acc_ref, car_ref):
    pair = pl.program_id(1)
    qi = pl.program_id(2)
    tq, tk = SB_TQ, SB_TK
    lane = lax.broadcasted_iota(I32, (tq, LANES), 1)
    q = q_ref[...]
    zero = jnp.zeros_like(q)
    qh = (jnp.where(lane < SB_HEAD_DIM, q, zero), jnp.where(lane >= SB_HEAD_DIM, q, zero))
    r = lax.broadcasted_iota(I32, (tk, tk), 0)
    c = lax.broadcasted_iota(I32, (tk, tk), 1)
    tri = (r >= c).astype(BF16)
    qrow = lax.broadcasted_iota(I32, (tq, tk), 0)
    kcol = lax.broadcasted_iota(I32, (tq, tk), 1)
    acc_ref[...] = jnp.zeros_like(acc_ref)
    car_ref[...] = jnp.zeros_like(car_ref)

    def block(j, diag_sub):
        kt = kt_ref[j]
        vt = vt_ref[j]
        rs = slice(0, tq)
        if diag_sub is not None:
            rs = slice(diag_sub * tk, tq)
            causal = (kcol + diag_sub * tk < qrow)[rs]
        for h in range(2):
            z = _dot(qh[h][rs], kt) + bias_ref[2 * pair + h]
            sp = _softplus(z)
            if diag_sub is not None:
                sp = jnp.where(causal, sp, 0.0)
            car = car_ref[h, rs]
            cum = _dot(sp.astype(BF16), tri)
            a = jnp.exp(z - (cum + jnp.concatenate([car] * (tk // LANES), axis=1)))
            if diag_sub is not None:
                a = jnp.where(causal, a, 0.0)
            acc_ref[h, rs] += _dot_nt(a.astype(BF16), vt)
            car_ref[h, rs] = car + jnp.broadcast_to(cum[:, 0:1], car.shape)

    nsub = tq // tk
    for s in reversed(range(nsub)):
        block(qi * nsub + s, s)

    def body(t, carry):
        block(qi * nsub - 1 - 2 * t, None)
        block(qi * nsub - 2 - 2 * t, None)
        return carry

    lax.fori_loop(0, qi * (nsub // 2), body, 0)
    o_ref[...] = jnp.where(lane < SB_HEAD_DIM, acc_ref[0], acc_ref[1]).astype(o_ref.dtype)


def sb_prompt(q_bf, ktb, vtb, sb_bias, seq_len):
    n = q_bf.shape[0]
    nseq = n // seq_len
    nq = seq_len // SB_TQ
    nk = seq_len // SB_TK
    npair = SB_HEADS // 2
    grid_spec = pltpu.PrefetchScalarGridSpec(
        num_scalar_prefetch=0,
        grid=(nseq, npair, nq),
        in_specs=[pl.BlockSpec(memory_space=pltpu.SMEM),
                  pl.BlockSpec((SB_TQ, LANES), lambda b, p, i: (b * nq + i, p)),
                  pl.BlockSpec((None, nk, LANES, SB_TK), lambda b, p, i: (b, 0, p, 0)),
                  pl.BlockSpec((None, nk, LANES, SB_TK), lambda b, p, i: (b, 0, p, 0))],
        out_specs=pl.BlockSpec((SB_TQ, LANES), lambda b, p, i: (b * nq + i, p)),
        scratch_shapes=[pltpu.VMEM((2, SB_TQ, LANES), F32), pltpu.VMEM((2, SB_TQ, LANES), F32)],
    )
    return pl.pallas_call(
        _sbp_kernel, grid_spec=grid_spec,
        out_shape=jax.ShapeDtypeStruct((n, SB_HEADS * SB_HEAD_DIM), BF16),
        compiler_params=_params(("arbitrary", "arbitrary", "arbitrary")),
        name="sb_prompt",
    )(sb_bias.astype(F32), q_bf, ktb, vtb)


def _xp_kernel(qx_ref, mk_ref, mv_ref, o_ref):
    for h in range(X_HEADS):
        sl = slice(h * X_HEAD_DIM, (h + 1) * X_HEAD_DIM)
        s = _dot_nt(qx_ref[:, sl], mk_ref[:, sl]) * (X_HEAD_DIM ** -0.5)
        e = jnp.exp(s - jnp.max(s, axis=-1, keepdims=True))
        p = e / jnp.sum(e, axis=-1, keepdims=True)
        o_ref[:, sl] = _dot(p.astype(BF16), mv_ref[:, sl]).astype(o_ref.dtype)


def xattn_prompt(qx_bf, mk_bf, mv_bf, seq_len, n_mem, tm):
    n = qx_bf.shape[0]
    tps = seq_len // tm
    xw = X_HEADS * X_HEAD_DIM
    return pl.pallas_call(
        _xp_kernel, grid=(n // tm,),
        in_specs=[pl.BlockSpec((tm, xw), lambda i: (i, 0)),
                  pl.BlockSpec((n_mem, xw), lambda i: (i // tps, 0)),
                  pl.BlockSpec((n_mem, xw), lambda i: (i // tps, 0))],
        out_specs=pl.BlockSpec((tm, xw), lambda i: (i, 0)),
        out_shape=jax.ShapeDtypeStruct((n, xw), BF16),
        compiler_params=_params(("arbitrary",)),
        name="xattn_prompt",
    )(qx_bf, mk_bf, mv_bf)


def _split_bf16(x):
    hi = x.astype(BF16)
    return hi, (x - hi.astype(F32)).astype(BF16)


def _sbs_kernel(new_key_visible, npages, pt_ref, bias_ref, qbd_ref, kn_ref, vn_ref, *refs):
    del pt_ref
    k_refs, v_refs, o_ref = refs[:npages], refs[npages:2 * npages], refs[2 * npages]
    b = pl.program_id(0)
    psz = PAGE_SIZE
    sbw = SB_HEADS * SB_HEAD_DIM
    rows = npages * SB_HEADS
    head_bits = SB_HEADS.bit_length() - 1
    qbd = qbd_ref[0]
    bias = bias_ref[...]

    mine = lax.broadcasted_iota(I32, (SB_HEADS, kn_ref.shape[1]), 1) == b
    zs = _dot(qbd, kn_ref[...].astype(BF16)) + bias[:, 0:1]
    seen = jnp.logical_and(mine, new_key_visible)
    sp0 = jnp.where(seen, _softplus(zs), 0.0)
    a0 = jnp.where(seen, jnp.exp(zs - sp0), 0.0)
    sp0_tot = jnp.sum(sp0, axis=-1, keepdims=True)
    a0_rows = jnp.concatenate(
        [jnp.broadcast_to(a0[h:h + 1], (SB_HEAD_DIM, a0.shape[1])) for h in range(SB_HEADS)], axis=0)
    new = jnp.sum(a0_rows * vn_ref[...], axis=-1, keepdims=True)

    z = jnp.concatenate([_dot(qbd, k_refs[j][...].astype(BF16)) + bias for j in range(npages)], axis=0)
    sp = _softplus(z)
    r = lax.broadcasted_iota(I32, (psz, psz), 0)
    c = lax.broadcasted_iota(I32, (psz, psz), 1)
    tri = (r >= c).astype(BF16)
    hi, lo = _split_bf16(sp)
    cum = _dot(hi, tri) + _dot(lo, tri)
    tot = jnp.broadcast_to(cum[:, 0:1], (rows, psz))
    rr = lax.broadcasted_iota(I32, (rows, rows), 0)
    cc = lax.broadcasted_iota(I32, (rows, rows), 1)
    later = jnp.logical_and((cc >> head_bits) > (rr >> head_bits),
                            (cc & (SB_HEADS - 1)) == (rr & (SB_HEADS - 1))).astype(BF16)
    th, tl = _split_bf16(tot)
    carry = _dot(later, th) + _dot(later, tl) + jnp.concatenate([sp0_tot] * npages, axis=0)
    a = jnp.exp(z - cum - carry)

    accs = []
    for h in range(SB_HEADS):
        hs = slice(h * SB_HEAD_DIM, (h + 1) * SB_HEAD_DIM)
        acc = jnp.zeros((SB_HEAD_DIM, psz), F32)
        for j in range(npages):
            rj = j * SB_HEADS + h
            acc = acc + jnp.broadcast_to(a[rj:rj + 1], (SB_HEAD_DIM, psz)) * v_refs[j][hs, :]
        accs.append(acc)
    acc = jnp.concatenate(accs, axis=0)
    lane0 = lax.broadcasted_iota(I32, (sbw, psz), 1) == 0
    acc = acc + jnp.where(lane0, new, 0.0)
    hi2, lo2 = _split_bf16(acc)
    ones = jnp.ones((SUBLANES, psz), BF16)
    y = _dot_nt(ones, hi2) + _dot_nt(ones, lo2)
    o_ref[pl.ds(b, 1), :] = y[0:1]


def sb_sample(q_s, kn_t, vn_t, cache_k_t, cache_v_t, page_table, sb_bias, new_key_visible):
    nb, npages = page_table.shape
    sbw = SB_HEADS * SB_HEAD_DIM
    head_of_col = jnp.arange(sbw) // SB_HEAD_DIM
    sel = head_of_col[None, :] == jnp.arange(SB_HEADS)[:, None]
    qbd = jnp.where(sel[None], q_s[:, None, :], 0.0).astype(BF16)
    assert PAGE_SIZE == LANES
    bias = jnp.broadcast_to(sb_bias.astype(F32)[:, None], (SB_HEADS, LANES))
    page = lambda j: pl.BlockSpec((None, sbw, PAGE_SIZE), lambda b, pt: (pt[b, j], 0, 0))
    grid_spec = pltpu.PrefetchScalarGridSpec(
        num_scalar_prefetch=1,
        grid=(nb,),
        in_specs=[pl.BlockSpec((SB_HEADS, LANES), lambda b, pt: (0, 0)),
                  pl.BlockSpec((1, SB_HEADS, sbw), lambda b, pt: (b, 0, 0)),
                  pl.BlockSpec((sbw, nb), lambda b, pt: (0, 0)),
                  pl.BlockSpec((sbw, nb), lambda b, pt: (0, 0))]
                 + [page(j) for j in range(npages)] * 2,
        out_specs=pl.BlockSpec((nb, sbw), lambda b, pt: (0, 0)),
    )
    return pl.pallas_call(
        functools.partial(_sbs_kernel, new_key_visible, npages), grid_spec=grid_spec,
        out_shape=jax.ShapeDtypeStruct((nb, sbw), F32),
        compiler_params=_params(("arbitrary",)),
        name="sb_sample",
    )(page_table, bias, qbd, kn_t, vn_t, *([cache_k_t] * npages), *([cache_v_t] * npages))


def _xs_kernel(qx_ref, mk_ref, mv_ref, o_ref):
    step = pl.program_id(0)
    rows = N_MEM * X_HEADS
    own = ((lax.broadcasted_iota(I32, (SUBLANES, rows), 1) & (X_HEADS - 1))
           == lax.broadcasted_iota(I32, (SUBLANES, rows), 0))
    for t in range(XS_TOKENS):
        mk = mk_ref[t].astype(BF16)
        mv = mv_ref[t].astype(BF16)
        s = _dot_nt(qx_ref[t], mk) * (X_HEAD_DIM ** -0.5)
        s = jnp.where(own, s, -jnp.inf)
        m = jnp.max(s, axis=-1, keepdims=True)
        m = jnp.where(m == -jnp.inf, 0.0, m)
        e = jnp.exp(s - m)
        den = jnp.sum(e, axis=-1, keepdims=True)
        p = e / jnp.where(den == 0.0, 1.0, den)
        y = _dot(p.astype(BF16), mv)
        o_ref[pl.ds(step * XS_TOKENS + t, 1)] = y[None, 0:X_HEADS]


def xattn_sample(qx_bf, cache_mem_k, cache_mem_v):
    nb = qx_bf.shape[0]
    rows = N_MEM * X_HEADS
    mk = cache_mem_k.reshape(nb, rows, X_HEAD_DIM)
    mv = cache_mem_v.reshape(nb, rows, X_HEAD_DIM)
    qx = qx_bf.reshape(nb, X_HEADS, X_HEAD_DIM)
    qx = jnp.concatenate([qx, jnp.zeros((nb, SUBLANES - X_HEADS, X_HEAD_DIM), BF16)], axis=1)
    assert nb % XS_TOKENS == 0
    out = pl.pallas_call(
        _xs_kernel, grid=(nb // XS_TOKENS,),
        in_specs=[pl.BlockSpec((XS_TOKENS, SUBLANES, X_HEAD_DIM), lambda b: (b, 0, 0)),
                  pl.BlockSpec((XS_TOKENS, rows, X_HEAD_DIM), lambda b: (b, 0, 0)),
                  pl.BlockSpec((XS_TOKENS, rows, X_HEAD_DIM), lambda b: (b, 0, 0))],
        out_specs=pl.BlockSpec((nb, X_HEADS, X_HEAD_DIM), lambda b: (0, 0, 0)),
        out_shape=jax.ShapeDtypeStruct((nb, X_HEADS, X_HEAD_DIM), F32),
        compiler_params=_params(("arbitrary",)),
        name="xattn_sample",
    )(qx, mk, mv)
    return out.reshape(nb, X_HEADS * X_HEAD_DIM)


_R_E1, _R_E2, _R_R1, _R_R2, _R_G1, _R_G2 = 0, 1, 2, 3, 4, 5
_GROUP_LANE0 = N_EXPERTS


def _merge_kernel(tm, x_ref, mp_ref, ysb_ref, yx_ref, g1_ref, g2_ref, wb1_ref, wb2_ref, wo_ref,
                  n2_ref, wr_ref, br_ref, cin_ref, x1_ref, h_ref, route_ref, route_t_ref, cout_ref, cnt_ref):
    @pl.when(pl.program_id(0) == 0)
    def _():
        cnt_ref[...] = cin_ref[...]

    m = (mp_ref[...] + g1_ref[...] * _dot(ysb_ref[...].astype(BF16), wb1_ref[...])
         + g2_ref[...] * _dot(yx_ref[...].astype(BF16), wb2_ref[...]))
    x1 = x_ref[...] + _dot(m.astype(BF16), wo_ref[...])
    x1_ref[...] = x1
    hn = _rms(x1, n2_ref[...])
    h_ref[...] = hn
    logits = _dot(hn.astype(BF16), wr_ref[...]) + br_ref[...]

    lane = lax.broadcasted_iota(I32, (tm, LANES), 1).astype(F32)
    big = float(LANES)
    ninf = -jnp.inf

    def first_argmax(v):
        mx = jnp.max(v, axis=-1, keepdims=True)
        idx = jnp.min(jnp.where(v == mx, lane, big), axis=-1, keepdims=True)
        return mx, idx

    gl = jnp.where((lane >= _GROUP_LANE0) & (lane < _GROUP_LANE0 + N_GROUPS), logits, ninf)
    gmax, glane = first_argmax(gl)
    gprob = 1.0 / jnp.sum(jnp.exp(gl - gmax), axis=-1, keepdims=True)
    e0 = (glane - _GROUP_LANE0) * EXPERTS_PER_GROUP
    el = jnp.where((lane >= e0) & (lane < e0 + EXPERTS_PER_GROUP), logits, ninf)
    m1, i1 = first_argmax(el)
    m2, i2 = first_argmax(jnp.where(lane == i1, ninf, el))
    e2 = jnp.exp(m2 - m1)
    gate1 = gprob / (1.0 + e2)
    gate2 = gprob * e2 / (1.0 + e2)

    oh1 = (lane == i1).astype(F32)
    oh2 = (lane == i2).astype(F32)
    cnt = oh1 + oh2
    r = lax.broadcasted_iota(I32, (tm, tm), 0)
    c = lax.broadcasted_iota(I32, (tm, tm), 1)
    before = _dot((c < r).astype(BF16), cnt.astype(BF16)) + cnt_ref[...]
    rank1 = jnp.sum(oh1 * before, axis=-1, keepdims=True)
    rank2 = jnp.sum(oh2 * before, axis=-1, keepdims=True)
    cnt_ref[...] = cnt_ref[...] + jnp.sum(cnt, axis=0, keepdims=True)
    cout_ref[...] = cnt_ref[...]

    rec = jnp.zeros((tm, LANES), F32)
    for ln, val in ((_R_E1, i1.astype(F32)), (_R_E2, i2.astype(F32)), (_R_R1, rank1), (_R_R2, rank2),
                    (_R_G1, gate1), (_R_G2, gate2)):
        rec = jnp.where(lane == ln, val, rec)
    route_ref[...] = rec
    route_t_ref[...] = rec.T[0:SUBLANES]


def merge(x2d, mp, ysb, yx, g1, g2, wb1_bf, wb2_bf, wo_bf, norm2, w_route_bf, b_route, cnt_in, tm):
    n = x2d.shape[0]
    row = lambda w: pl.BlockSpec((tm, w), lambda i: (i, 0))
    return pl.pallas_call(
        functools.partial(_merge_kernel, tm), grid=(n // tm,),
        in_specs=[row(D_MODEL), row(D_MODEL), row(BRANCH_W), row(BRANCH_W), row(D_MODEL), row(D_MODEL),
                  _const_spec((BRANCH_W, D_MODEL)), _const_spec((BRANCH_W, D_MODEL)),
                  _const_spec((D_MODEL, D_MODEL)), _const_spec((1, D_MODEL)),
                  _const_spec((D_MODEL, LANES)), _const_spec((1, LANES)), _const_spec((1, LANES))],
        out_specs=[row(D_MODEL), row(D_MODEL), row(LANES), pl.BlockSpec((SUBLANES, tm), lambda i: (0, i)),
                   pl.BlockSpec((1, LANES), lambda i: (0, 0))],
        out_shape=[jax.ShapeDtypeStruct((n, D_MODEL), F32), jax.ShapeDtypeStruct((n, D_MODEL), F32),
                   jax.ShapeDtypeStruct((n, LANES), F32), jax.ShapeDtypeStruct((SUBLANES, n), F32),
                   jax.ShapeDtypeStruct((1, LANES), F32)],
        scratch_shapes=[pltpu.VMEM((1, LANES), F32)],
        compiler_params=_params(("arbitrary",)),
        name="merge",
    )(x2d, mp, ysb, yx, g1, g2, wb1_bf, wb2_bf, wo_bf, norm2.reshape(1, -1), w_route_bf, b_route, cnt_in)


def _row_copy(src, src_row, dst, dst_row, sem):
    return pltpu.make_async_copy(src.at[pl.ds(src_row, 1)], dst.at[pl.ds(dst_row, 1)], sem)


def _dispatch_kernel(tm, dest_ref, h_ref, buf_in_ref, buf_ref, sem):
    del buf_in_ref

    def issue(t, carry):
        _row_copy(h_ref, t, buf_ref, dest_ref[0, 0, t], sem).start(priority=0)
        _row_copy(h_ref, t, buf_ref, dest_ref[0, 0, tm + t], sem).start(priority=1)
        return carry

    lax.fori_loop(0, tm, issue, 0, unroll=ROW_DMA_UNROLL)
    for _ in range(2):
        pltpu.make_async_copy(h_ref, buf_ref.at[pl.ds(0, tm)], sem).wait()


def _slot_tiles(dest, tm):
    nt = dest.shape[1] // tm
    return dest.reshape(2, nt, tm).transpose(1, 0, 2).reshape(nt, 1, 2 * tm)


def dispatch(h, dest, buf_zero, tm):
    n = h.shape[0]
    nt = n // tm
    return pl.pallas_call(
        functools.partial(_dispatch_kernel, tm), grid=(nt,),
        in_specs=[pl.BlockSpec((1, 1, 2 * tm), lambda i: (i, 0, 0), memory_space=pltpu.SMEM),
                  pl.BlockSpec((tm, D_MODEL), lambda i: (i, 0)),
                  pl.BlockSpec(memory_space=pl.ANY)],
        out_specs=pl.BlockSpec(memory_space=pl.ANY),
        out_shape=jax.ShapeDtypeStruct(buf_zero.shape, buf_zero.dtype),
        scratch_shapes=[pltpu.SemaphoreType.DMA(())],
        input_output_aliases={2: 0},
        compiler_params=_params(("arbitrary",)),
        name="dispatch",
    )(_slot_tiles(dest, tm), h, buf_zero)


def _experts_kernel(nblk, first_ref, count_ref, x_hbm, w1_ref, w3_ref, w2_ref, y_hbm,
                    xbuf, ybuf, w1b, w3b, w2b, sem_in, sem_out):
    e = pl.program_id(0)
    first = first_ref[e]
    count = count_ref[e]
    bm = MOE_BM

    def load(blk, slot):
        return pltpu.make_async_copy(x_hbm.at[pl.ds(blk * bm, bm)], xbuf.at[slot], sem_in.at[slot])

    def store(blk, slot):
        return pltpu.make_async_copy(ybuf.at[slot], y_hbm.at[pl.ds(blk * bm, bm)], sem_out.at[slot])

    @pl.when(count > 0)
    def _():
        load(first, 0).start()
        w1b[...] = w1_ref[...].astype(BF16)
        w3b[...] = w3_ref[...].astype(BF16)
        w2b[...] = w2_ref[...].astype(BF16)

        def body(j, carry):
            slot = j & 1

            @pl.when(j + 1 < count)
            def _():
                load(first + j + 1, 1 - slot).start()

            load(first + j, slot).wait()

            @pl.when(j >= 2)
            def _():
                store(first + j - 2, slot).wait()

            xb = xbuf[slot].astype(BF16)
            h1 = _dot(xb, w1b[...])
            h3 = _dot(xb, w3b[...])
            act = (h1 * jax.nn.sigmoid(h1)) * h3
            ybuf[slot] = _dot(act.astype(BF16), w2b[...])
            store(first + j, slot).start()
            return carry

        lax.fori_loop(0, count, body, 0)
        for back in (1, 2):
            @pl.when(count >= back)
            def _():
                store(first + count - back, (count - back) & 1).wait()

    @pl.when(e == pl.num_programs(0) - 1)
    def _():
        ybuf[0] = jnp.zeros((bm, D_MODEL), F32)

        def fill(blk, carry):
            cp = store(blk, 0)
            cp.start()
            cp.wait()
            return carry

        lax.fori_loop(first + count, nblk, fill, 0)


def experts(buf, first_blk, n_blk, w1, w3, w2):
    nblk = buf.shape[0] // MOE_BM
    grid_spec = pltpu.PrefetchScalarGridSpec(
        num_scalar_prefetch=2,
        grid=(N_EXPERTS,),
        in_specs=[pl.BlockSpec(memory_space=pl.ANY),
                  pl.BlockSpec((None, D_MODEL, D_EXPERT), lambda e, fb, nb: (e, 0, 0)),
                  pl.BlockSpec((None, D_MODEL, D_EXPERT), lambda e, fb, nb: (e, 0, 0)),
                  pl.BlockSpec((None, D_EXPERT, D_MODEL), lambda e, fb, nb: (e, 0, 0))],
        out_specs=pl.BlockSpec(memory_space=pl.ANY),
        scratch_shapes=[pltpu.VMEM((2, MOE_BM, D_MODEL), F32), pltpu.VMEM((2, MOE_BM, D_MODEL), F32),
                        pltpu.VMEM((D_MODEL, D_EXPERT), BF16), pltpu.VMEM((D_MODEL, D_EXPERT), BF16),
                        pltpu.VMEM((D_EXPERT, D_MODEL), BF16),
                        pltpu.SemaphoreType.DMA((2,)), pltpu.SemaphoreType.DMA((2,))],
    )
    return pl.pallas_call(
        functools.partial(_experts_kernel, nblk), grid_spec=grid_spec,
        out_shape=jax.ShapeDtypeStruct(buf.shape, F32),
        compiler_params=_params(("arbitrary",)),
        name="experts",
    )(first_blk, n_blk, buf, w1, w3, w2)


def _combine_kernel(tm, dest_ref, x1_ref, route_ref, ybuf_ref, o_ref, rows_ref, sem):
    def issue(t, carry):
        _row_copy(ybuf_ref, dest_ref[0, 0, t], rows_ref.at[0], t, sem).start(priority=0)
        _row_copy(ybuf_ref, dest_ref[0, 0, tm + t], rows_ref.at[1], t, sem).start(priority=1)
        return carry

    lax.fori_loop(0, tm, issue, 0, unroll=ROW_DMA_UNROLL)
    for k in range(2):
        pltpu.make_async_copy(ybuf_ref.at[pl.ds(0, tm)], rows_ref.at[k], sem).wait()
    route = route_ref[...]
    o_ref[...] = (x1_ref[...] + route[:, _R_G1:_R_G1 + 1] * rows_ref[0]
                  + route[:, _R_G2:_R_G2 + 1] * rows_ref[1])


def combine(x1, route, dest, ybuf, tm):
    n = x1.shape[0]
    nt = n // tm
    return pl.pallas_call(
        functools.partial(_combine_kernel, tm), grid=(nt,),
        in_specs=[pl.BlockSpec((1, 1, 2 * tm), lambda i: (i, 0, 0), memory_space=pltpu.SMEM),
                  pl.BlockSpec((tm, D_MODEL), lambda i: (i, 0)),
                  pl.BlockSpec((tm, LANES), lambda i: (i, 0)),
                  pl.BlockSpec(memory_space=pl.ANY)],
        out_specs=pl.BlockSpec((tm, D_MODEL), lambda i: (i, 0)),
        out_shape=jax.ShapeDtypeStruct((n, D_MODEL), F32),
        scratch_shapes=[pltpu.VMEM((2, tm, D_MODEL), F32), pltpu.SemaphoreType.DMA(())],
        compiler_params=_params(("arbitrary",)),
        name="combine",
    )(_slot_tiles(dest, tm), x1, route, ybuf)


def kernel(x_prompt, x_sample, mem_prompt, cache_k, cache_v, cache_mem_k, cache_mem_v, state_conv, page_table, norm1, w_in, conv_w, sb_bias, mem_norm, w_mem_kv, q_norm, k_norm, w_branch, w_o, norm2, w_group, b_group, w_router, b_router, w1, w3, w2):
    nb, seq, _ = x_prompt.shape
    ns, dec_seq, _ = x_sample.shape
    assert dec_seq == 1 and seq % SB_TQ == 0 and SB_TQ % (2 * SB_TK) == 0
    n_mem = mem_prompt.shape[1]
    sbw = SB_HEADS * SB_HEAD_DIM
    xw = X_HEADS * X_HEAD_DIM
    past = page_table.shape[1] * PAGE_SIZE

    assert w_in.shape[1] == _W_MAIN and _C_QX == _C_K + 2 * sbw and _C_GL == _C_QX + xw
    w_main = w_in.astype(BF16)
    w_kvt = kv_weight_t(w_in)
    wb = w_branch.astype(BF16)
    wo_bf = w_o.astype(BF16)
    w_route = jnp.zeros((D_MODEL, LANES), F32)
    w_route = w_route.at[:, :N_EXPERTS].set(w_router).at[:, N_EXPERTS:N_EXPERTS + N_GROUPS].set(w_group)
    b_route = jnp.zeros((1, LANES), F32)
    b_route = b_route.at[0, :N_EXPERTS].set(b_router).at[0, N_EXPERTS:N_EXPERTS + N_GROUPS].set(b_group)
    w_route = w_route.astype(BF16)

    xp = x_prompt.reshape(nb * seq, D_MODEL)
    mk_p, mv_p = memkv(mem_prompt.reshape(nb * n_mem, D_MODEL), mem_norm, w_mem_kv.astype(BF16), k_norm, n_mem)
    (kt_p, vt_p, q_p, qx_p, g1_p, g2_p, mp_p, tail_p, ktb_p, vtb_p) = inproj(
        xp, norm1, w_main, w_kvt, wb[0], conv_w, q_norm, seq, 512)
    ysb_p = sb_prompt(q_p, ktb_p, vtb_p, sb_bias, seq)
    yx_p = xattn_prompt(qx_p, mk_p.astype(BF16), mv_p.astype(BF16), seq, n_mem, 512)

    xs = x_sample.reshape(ns, D_MODEL)
    (kt_s, vt_s, q_s, qx_s, g1_s, g2_s, mp_s, u_s) = inproj(
        xs, norm1, w_main, w_kvt, wb[0], conv_w, q_norm, ns, ns,
        conv_state=(state_conv[:, 0], state_conv[:, 1]))
    ck_t = cache_k.transpose(0, 2, 3, 1).reshape(cache_k.shape[0], sbw, PAGE_SIZE)
    cv_t = cache_v.transpose(0, 2, 3, 1).reshape(cache_v.shape[0], sbw, PAGE_SIZE)
    ysb_s = sb_sample(q_s.astype(F32), kt_s[0], vt_s[0], ck_t, cv_t, page_table, sb_bias,
                      new_key_visible=(past < past))
    yx_s = xattn_sample(qx_s, cache_mem_k, cache_mem_v)

    cnt0 = jnp.zeros((1, LANES), F32)
    x1_p, h_p, route_p, rt_p, cnt1 = merge(xp, mp_p, ysb_p, yx_p, g1_p, g2_p, wb[1], wb[2], wo_bf, norm2,
                                           w_route, b_route, cnt0, 512)
    x1_s, h_s, route_s, rt_s, cnt2 = merge(xs, mp_s, ysb_s, yx_s, g1_s, g2_s, wb[1], wb[2], wo_bf, norm2,
                                           w_route, b_route, cnt1, ns)

    n_slots = 2 * (nb * seq + ns)
    nblk = -(-(n_slots + N_EXPERTS * (MOE_BM - 1)) // MOE_BM)
    counts = cnt2[0, :N_EXPERTS].astype(I32)
    padded = (counts + MOE_BM - 1) // MOE_BM * MOE_BM
    ends = jnp.cumsum(padded)
    starts = ends - padded

    def dests(route_t):
        e = route_t[_R_E1:_R_E2 + 1].astype(I32)
        rk = route_t[_R_R1:_R_R2 + 1].astype(I32)
        first = jnp.zeros_like(e)
        for x in range(N_EXPERTS):
            first = jnp.where(e == x, starts[x], first)
        return first + rk

    dest_p, dest_s = dests(rt_p), dests(rt_s)
    buf = jnp.zeros((nblk * MOE_BM, D_MODEL), F32)
    buf = dispatch(h_p, dest_p, buf, 512)
    buf = dispatch(h_s, dest_s, buf, ns)
    ybuf = experts(buf, starts // MOE_BM, padded // MOE_BM, w1, w3, w2)
    y_p = combine(x1_p, route_p, dest_p, ybuf, 256)
    y_s = combine(x1_s, route_s, dest_s, ybuf, ns)

    def heads_last(t, b, s):
        return t.reshape(b, SB_HEADS, SB_HEAD_DIM, s).transpose(0, 3, 1, 2)

    conv_prompt = tail_p[:, SUBLANES - 2:, :]
    conv_sample = jnp.stack([state_conv[:, 1], u_s[0]], axis=1)
    return (y_p.reshape(nb, seq, D_MODEL), y_s.reshape(ns, 1, D_MODEL),
            heads_last(kt_p, nb, seq), heads_last(vt_p, nb, seq), conv_prompt,
            mk_p.reshape(nb, n_mem, X_HEADS, X_HEAD_DIM), mv_p.reshape(nb, n_mem, X_HEADS, X_HEAD_DIM),
            heads_last(kt_s, 1, ns).reshape(ns, 1, SB_HEADS, SB_HEAD_DIM),
            heads_last(vt_s, 1, ns).reshape(ns, 1, SB_HEADS, SB_HEAD_DIM),
            conv_sample)
```

```python
import functools

import jax
import jax.numpy as jnp
from jax import lax
from jax.experimental import pallas as pl
from jax.experimental.pallas import tpu as pltpu

F32 = jnp.float32
BF16 = jnp.bfloat16
I32 = jnp.int32

EPS = 1e-6
D_MODEL = 1024
BRANCH_W = 512
SB_HEADS = 8
SB_HEAD_DIM = 64
X_HEADS = 4
X_HEAD_DIM = 128
N_MEM = 256
N_GROUPS = 4
EXPERTS_PER_GROUP = 8
N_EXPERTS = N_GROUPS * EXPERTS_PER_GROUP
D_EXPERT = 512
PAGE_SIZE = 128

LANES = 128
SUBLANES = 8
VMEM_LIMIT = 56 * 1024 * 1024

SB_TK = 256
SB_TQ = 8 * SB_TK
MOE_BM = 256
XS_TOKENS = 8
ROW_DMA_UNROLL = 8


def _dot(a, b):
    return jnp.dot(a, b, preferred_element_type=F32)


def _dot_nt(a, b):
    return lax.dot_general(a, b, (((1,), (1,)), ((), ())), preferred_element_type=F32)


def _rms(x, g):
    return x * lax.rsqrt(jnp.mean(x * x, axis=-1, keepdims=True) + EPS) * g


def _softplus(z):
    return jnp.maximum(z, 0.0) + jnp.log(1.0 + jnp.exp(-jnp.abs(z)))


def _params(sem, vmem=VMEM_LIMIT):
    return pltpu.CompilerParams(dimension_semantics=sem, vmem_limit_bytes=vmem)


def _const_spec(shape):
    nd = len(shape)
    return pl.BlockSpec(shape, lambda *_: (0,) * nd, pipeline_mode=pl.Buffered(1))


def _memkv_kernel(mem_ref, g_ref, w_ref, kn_ref, mk_ref, mv_ref):
    xn = _rms(mem_ref[...], g_ref[...]).astype(BF16)
    kv = _dot(xn, w_ref[...])
    xw = X_HEADS * X_HEAD_DIM
    for h in range(X_HEADS):
        sl = slice(h * X_HEAD_DIM, (h + 1) * X_HEAD_DIM)
        mk_ref[:, sl] = _rms(kv[:, sl], kn_ref[...])
    mv_ref[...] = kv[:, xw:]


def memkv(mem2d, mem_norm, w_mem_kv_bf, k_norm, n_mem):
    n = mem2d.shape[0]
    xw = X_HEADS * X_HEAD_DIM
    return pl.pallas_call(
        _memkv_kernel,
        grid=(n // n_mem,),
        in_specs=[pl.BlockSpec((n_mem, D_MODEL), lambda i: (i, 0)),
                  _const_spec((1, D_MODEL)),
                  _const_spec((D_MODEL, 2 * xw)),
                  _const_spec((1, X_HEAD_DIM))],
        out_specs=[pl.BlockSpec((n_mem, xw), lambda i: (i, 0)),
                   pl.BlockSpec((n_mem, xw), lambda i: (i, 0))],
        out_shape=[jax.ShapeDtypeStruct((n, xw), F32), jax.ShapeDtypeStruct((n, xw), F32)],
        compiler_params=_params(("arbitrary",)),
        name="memkv",
    )(mem2d, mem_norm.reshape(1, -1), w_mem_kv_bf, k_norm.reshape(1, -1))


_C_UIN, _C_CG, _C_BG, _C_Q, _C_K, _C_QX, _C_GL = 0, 512, 1024, 1536, 2048, 3072, 3584
_W_MAIN = _C_GL + 3 * D_MODEL


def _kvt_kernel(w_ref, o_ref):
    o_ref[...] = w_ref[...].T.astype(BF16)


def kv_weight_t(w_in):
    kvw = 2 * SB_HEADS * SB_HEAD_DIM
    assert _C_K % kvw == 0
    return pl.pallas_call(
        _kvt_kernel, grid=(1,),
        in_specs=[pl.BlockSpec((D_MODEL, kvw), lambda i: (0, _C_K // kvw))],
        out_specs=pl.BlockSpec((kvw, D_MODEL), lambda i: (0, 0)),
        out_shape=jax.ShapeDtypeStruct((kvw, D_MODEL), BF16),
        compiler_params=_params(("arbitrary",)),
        name="kv_weight_t",
    )(w_in)


def _inproj_kernel(carry_conv, tiles_per_seq, tm, *refs):
    if carry_conv:
        (x_ref, n1_ref, wm_ref, wkvt_ref, wb0_ref, cw_ref, qn_ref,
         kt_ref, vt_ref, q_ref, qx_ref, g1_ref, g2_ref, mp_ref, tail_ref, ktb_ref, vtb_ref,
         carry_ref) = refs
    else:
        (x_ref, n1_ref, wm_ref, wkvt_ref, wb0_ref, cw_ref, qn_ref, um2_ref, um1_ref,
         kt_ref, vt_ref, q_ref, qx_ref, g1_ref, g2_ref, mp_ref, tail_ref) = refs

    xn = _rms(x_ref[...], n1_ref[...]).astype(BF16)

    def proj(c0, width):
        return _dot(xn, wm_ref[:, c0:c0 + width])

    kvt = _dot_nt(wkvt_ref[...], xn)
    sbw = SB_HEADS * SB_HEAD_DIM
    kt_ref[...] = kvt[:sbw]
    vt_ref[...] = kvt[sbw:]
    if carry_conv:
        for c in range(tm // SB_TK):
            ktb_ref[c] = kvt[:sbw, c * SB_TK:(c + 1) * SB_TK].astype(BF16)
            vtb_ref[c] = kvt[sbw:, c * SB_TK:(c + 1) * SB_TK].astype(BF16)

    q_ref[...] = (proj(_C_Q, sbw) * (SB_HEAD_DIM ** -0.5)).astype(BF16)

    qx = proj(_C_QX, X_HEADS * X_HEAD_DIM)
    for h in range(X_HEADS):
        sl = slice(h * X_HEAD_DIM, (h + 1) * X_HEAD_DIM)
        qx_ref[:, sl] = _rms(qx[:, sl], qn_ref[...]).astype(BF16)

    u = proj(_C_CG, BRANCH_W) * proj(_C_UIN, BRANCH_W)
    if carry_conv:
        @pl.when(pl.program_id(0) % tiles_per_seq == 0)
        def _():
            carry_ref[...] = jnp.zeros_like(carry_ref)
        ext = jnp.concatenate([carry_ref[...], u], axis=0)
        um1 = pltpu.roll(ext, 1, 0)[SUBLANES:]
        um2 = pltpu.roll(ext, 2, 0)[SUBLANES:]
        carry_ref[...] = u[tm - SUBLANES:]
        tail_ref[...] = u[tm - SUBLANES:]
    else:
        um1 = um1_ref[...]
        um2 = um2_ref[...]
        tail_ref[...] = u
    cw = cw_ref[...]
    y_conv = proj(_C_BG, BRANCH_W) * (um2 * cw[0:1] + um1 * cw[1:2] + u * cw[2:3])

    g0 = jax.nn.sigmoid(proj(_C_GL, D_MODEL))
    mp_ref[...] = g0 * _dot(y_conv.astype(BF16), wb0_ref[...])
    g1_ref[...] = jax.nn.sigmoid(proj(_C_GL + D_MODEL, D_MODEL))
    g2_ref[...] = jax.nn.sigmoid(proj(_C_GL + 2 * D_MODEL, D_MODEL))


def inproj(x2d, norm1, w_main_bf, w_kvt_bf, wb0_bf, conv_w, q_norm, seq_len, tm, conv_state=None):
    n = x2d.shape[0]
    carry = conv_state is None
    nseq = n // seq_len if carry else 1
    slen = seq_len if carry else n
    tps = slen // tm
    sbw = SB_HEADS * SB_HEAD_DIM
    xw = X_HEADS * X_HEAD_DIM
    nkb = max(tm // SB_TK, 1)
    tail_rows = SUBLANES if carry else tm

    row = lambda w: pl.BlockSpec((tm, w), lambda i: (i, 0))
    kt_spec = pl.BlockSpec((None, sbw, tm), lambda i: (i // tps, 0, i % tps))
    ktb_spec = pl.BlockSpec((None, nkb, sbw, SB_TK), lambda i: (i // tps, i % tps, 0, 0))
    in_specs = [row(D_MODEL), _const_spec((1, D_MODEL)), _const_spec((D_MODEL, _W_MAIN)),
                _const_spec((2 * sbw, D_MODEL)), _const_spec((BRANCH_W, D_MODEL)),
                _const_spec((3, BRANCH_W)), _const_spec((1, X_HEAD_DIM))]
    args = [x2d, norm1.reshape(1, -1), w_main_bf, w_kvt_bf, wb0_bf, conv_w, q_norm.reshape(1, -1)]
    scratch = []
    if carry:
        scratch = [pltpu.VMEM((SUBLANES, BRANCH_W), F32)]
    else:
        in_specs += [row(BRANCH_W), row(BRANCH_W)]
        args += list(conv_state)
    out_specs = [kt_spec, kt_spec, row(sbw), row(xw),
                 row(D_MODEL), row(D_MODEL), row(D_MODEL),
                 pl.BlockSpec((None, tail_rows, BRANCH_W), lambda i: (i // tps, 0, 0))]
    out_shape = [jax.ShapeDtypeStruct((nseq, sbw, slen), F32), jax.ShapeDtypeStruct((nseq, sbw, slen), F32),
                 jax.ShapeDtypeStruct((n, sbw), BF16), jax.ShapeDtypeStruct((n, xw), BF16),
                 jax.ShapeDtypeStruct((n, D_MODEL), F32), jax.ShapeDtypeStruct((n, D_MODEL), F32),
                 jax.ShapeDtypeStruct((n, D_MODEL), F32),
                 jax.ShapeDtypeStruct((nseq, tail_rows, BRANCH_W), F32)]
    if carry:
        out_specs += [ktb_spec, ktb_spec]
        out_shape += [jax.ShapeDtypeStruct((nseq, slen // SB_TK, sbw, SB_TK), BF16)] * 2
    return pl.pallas_call(
        functools.partial(_inproj_kernel, carry, tps, tm),
        grid=(n // tm,),
        in_specs=in_specs, out_specs=out_specs, out_shape=out_shape,
        scratch_shapes=scratch,
        compiler_params=_params(("arbitrary",)),
        name="inproj_prompt" if carry else "inproj_sample",
    )(*args)


def _sbp_kernel(bias_ref, q_ref, kt_ref, vt_ref, o_ref, acc_ref, car_ref):
    pair = pl.program_id(1)
    qi = pl.program_id(2)
    tq, tk = SB_TQ, SB_TK
    lane = lax.broadcasted_iota(I32, (tq, LANES), 1)
    q = q_ref[...]
    zero = jnp.zeros_like(q)
    qh = (jnp.where(lane < SB_HEAD_DIM, q, zero), jnp.where(lane >= SB_HEAD_DIM, q, zero))
    r = lax.broadcasted_iota(I32, (tk, tk), 0)
    c = lax.broadcasted_iota(I32, (tk, tk), 1)
    tri = (r >= c).astype(BF16)
    qrow = lax.broadcasted_iota(I32, (tq, tk), 0)
    kcol = lax.broadcasted_iota(I32, (tq, tk), 1)
    acc_ref[...] = jnp.zeros_like(acc_ref)
    car_ref[...] = jnp.zeros_like(car_ref)

    def block(j, diag_sub):
        kt = kt_ref[j]
        vt = vt_ref[j]
        rs = slice(0, tq)
        if diag_sub is not None:
            rs = slice(diag_sub * tk, tq)
            causal = (kcol + diag_sub * tk < qrow)[rs]
        for h in range(2):
            z = _dot(qh[h][rs], kt) + bias_ref[2 * pair + h]
            sp = _softplus(z)
            if diag_sub is not None:
                sp = jnp.where(causal, sp, 0.0)
            car = car_ref[h, rs]
            cum = _dot(sp.astype(BF16), tri)
            a = jnp.exp(z - (cum + jnp.concatenate([car] * (tk // LANES), axis=1)))
            if diag_sub is not None:
                a = jnp.where(causal, a, 0.0)
            acc_ref[h, rs] += _dot_nt(a.astype(BF16), vt)
            car_ref[h, rs] = car + jnp.broadcast_to(cum[:, 0:1], car.shape)

    nsub = tq // tk
    for s in reversed(range(nsub)):
        block(qi * nsub + s, s)

    def body(t, carry):
        block(qi * nsub - 1 - 2 * t, None)
        block(qi * nsub - 2 - 2 * t, None)
        return carry

    lax.fori_loop(0, qi * (nsub // 2), body, 0)
    o_ref[...] = jnp.where(lane < SB_HEAD_DIM, acc_ref[0], acc_ref[1]).astype(o_ref.dtype)


def sb_prompt(q_bf, ktb, vtb, sb_bias, seq_len):
    n = q_bf.shape[0]
    nseq = n // seq_len
    nq = seq_len // SB_TQ
    nk = seq_len // SB_TK
    npair = SB_HEADS // 2
    grid_spec = pltpu.PrefetchScalarGridSpec(
        num_scalar_prefetch=0,
        grid=(nseq, npair, nq),
        in_specs=[pl.BlockSpec(memory_space=pltpu.SMEM),
                  pl.BlockSpec((SB_TQ, LANES), lambda b, p, i: (b * nq + i, p)),
                  pl.BlockSpec((None, nk, LANES, SB_TK), lambda b, p, i: (b, 0, p, 0)),
                  pl.BlockSpec((None, nk, LANES, SB_TK), lambda b, p, i: (b, 0, p, 0))],
        out_specs=pl.BlockSpec((SB_TQ, LANES), lambda b, p, i: (b * nq + i, p)),
        scratch_shapes=[pltpu.VMEM((2, SB_TQ, LANES), F32), pltpu.VMEM((2, SB_TQ, LANES), F32)],
    )
    return pl.pallas_call(
        _sbp_kernel, grid_spec=grid_spec,
        out_shape=jax.ShapeDtypeStruct((n, SB_HEADS * SB_HEAD_DIM), BF16),
        compiler_params=_params(("arbitrary", "arbitrary", "arbitrary")),
        name="sb_prompt",
    )(sb_bias.astype(F32), q_bf, ktb, vtb)


def _xp_kernel(qx_ref, mk_ref, mv_ref, o_ref):
    for h in range(X_HEADS):
        sl = slice(h * X_HEAD_DIM, (h + 1) * X_HEAD_DIM)
        s = _dot_nt(qx_ref[:, sl], mk_ref[:, sl]) * (X_HEAD_DIM ** -0.5)
        e = jnp.exp(s - jnp.max(s, axis=-1, keepdims=True))
        p = e / jnp.sum(e, axis=-1, keepdims=True)
        o_ref[:, sl] = _dot(p.astype(BF16), mv_ref[:, sl]).astype(o_ref.dtype)


def xattn_prompt(qx_bf, mk_bf, mv_bf, seq_len, n_mem, tm):
    n = qx_bf.shape[0]
    tps = seq_len // tm
    xw = X_HEADS * X_HEAD_DIM
    return pl.pallas_call(
        _xp_kernel, grid=(n // tm,),
        in_specs=[pl.BlockSpec((tm, xw), lambda i: (i, 0)),
                  pl.BlockSpec((n_mem, xw), lambda i: (i // tps, 0)),
                  pl.BlockSpec((n_mem, xw), lambda i: (i // tps, 0))],
        out_specs=pl.BlockSpec((tm, xw), lambda i: (i, 0)),
        out_shape=jax.ShapeDtypeStruct((n, xw), BF16),
        compiler_params=_params(("arbitrary",)),
        name="xattn_prompt",
    )(qx_bf, mk_bf, mv_bf)


def _split_bf16(x):
    hi = x.astype(BF16)
    return hi, (x - hi.astype(F32)).astype(BF16)


def _sbs_kernel(new_key_visible, npages, pt_ref, bias_ref, qbd_ref, kn_ref, vn_ref, *refs):
    del pt_ref
    k_refs, v_refs, o_ref = refs[:npages], refs[npages:2 * npages], refs[2 * npages]
    b = pl.program_id(0)
    psz = PAGE_SIZE
    sbw = SB_HEADS * SB_HEAD_DIM
    rows = npages * SB_HEADS
    head_bits = SB_HEADS.bit_length() - 1
    qbd = qbd_ref[0]
    bias = bias_ref[...]

    mine = lax.broadcasted_iota(I32, (SB_HEADS, kn_ref.shape[1]), 1) == b
    zs = _dot(qbd, kn_ref[...].astype(BF16)) + bias[:, 0:1]
    seen = jnp.logical_and(mine, new_key_visible)
    sp0 = jnp.where(seen, _softplus(zs), 0.0)
    a0 = jnp.where(seen, jnp.exp(zs - sp0), 0.0)
    sp0_tot = jnp.sum(sp0, axis=-1, keepdims=True)
    a0_rows = jnp.concatenate(
        [jnp.broadcast_to(a0[h:h + 1], (SB_HEAD_DIM, a0.shape[1])) for h in range(SB_HEADS)], axis=0)
    new = jnp.sum(a0_rows * vn_ref[...], axis=-1, keepdims=True)

    z = jnp.concatenate([_dot(qbd, k_refs[j][...].astype(BF16)) + bias for j in range(npages)], axis=0)
    sp = _softplus(z)
    r = lax.broadcasted_iota(I32, (psz, psz), 0)
    c = lax.broadcasted_iota(I32, (psz, psz), 1)
    tri = (r >= c).astype(BF16)
    hi, lo = _split_bf16(sp)
    cum = _dot(hi, tri) + _dot(lo, tri)
    tot = jnp.broadcast_to(cum[:, 0:1], (rows, psz))
    rr = lax.broadcasted_iota(I32, (rows, rows), 0)
    cc = lax.broadcasted_iota(I32, (rows, rows), 1)
    later = jnp.logical_and((cc >> head_bits) > (rr >> head_bits),
                            (cc & (SB_HEADS - 1)) == (rr & (SB_HEADS - 1))).astype(BF16)
    th, tl = _split_bf16(tot)
    carry = _dot(later, th) + _dot(later, tl) + jnp.concatenate([sp0_tot] * npages, axis=0)
    a = jnp.exp(z - cum - carry)

    accs = []
    for h in range(SB_HEADS):
        hs = slice(h * SB_HEAD_DIM, (h + 1) * SB_HEAD_DIM)
        acc = jnp.zeros((SB_HEAD_DIM, psz), F32)
        for j in range(npages):
            rj = j * SB_HEADS + h
            acc = acc + jnp.broadcast_to(a[rj:rj + 1], (SB_HEAD_DIM, psz)) * v_refs[j][hs, :]
        accs.append(acc)
    acc = jnp.concatenate(accs, axis=0)
    lane0 = lax.broadcasted_iota(I32, (sbw, psz), 1) == 0
    acc = acc + jnp.where(lane0, new, 0.0)
    hi2, lo2 = _split_bf16(acc)
    ones = jnp.ones((SUBLANES, psz), BF16)
    y = _dot_nt(ones, hi2) + _dot_nt(ones, lo2)
    o_ref[pl.ds(b, 1), :] = y[0:1]


def sb_sample(q_s, kn_t, vn_t, cache_k_t, cache_v_t, page_table, sb_bias, new_key_visible):
    nb, npages = page_table.shape
    sbw = SB_HEADS * SB_HEAD_DIM
    head_of_col = jnp.arange(sbw) // SB_HEAD_DIM
    sel = head_of_col[None, :] == jnp.arange(SB_HEADS)[:, None]
    qbd = jnp.where(sel[None], q_s[:, None, :], 0.0).astype(BF16)
    assert PAGE_SIZE == LANES
    bias = jnp.broadcast_to(sb_bias.astype(F32)[:, None], (SB_HEADS, LANES))
    page = lambda j: pl.BlockSpec((None, sbw, PAGE_SIZE), lambda b, pt: (pt[b, j], 0, 0))
    grid_spec = pltpu.PrefetchScalarGridSpec(
        num_scalar_prefetch=1,
        grid=(nb,),
        in_specs=[pl.BlockSpec((SB_HEADS, LANES), lambda b, pt: (0, 0)),
                  pl.BlockSpec((1, SB_HEADS, sbw), lambda b, pt: (b, 0, 0)),
                  pl.BlockSpec((sbw, nb), lambda b, pt: (0, 0)),
                  pl.BlockSpec((sbw, nb), lambda b, pt: (0, 0))]
                 + [page(j) for j in range(npages)] * 2,
        out_specs=pl.BlockSpec((nb, sbw), lambda b, pt: (0, 0)),
    )
    return pl.pallas_call(
        functools.partial(_sbs_kernel, new_key_visible, npages), grid_spec=grid_spec,
        out_shape=jax.ShapeDtypeStruct((nb, sbw), F32),
        compiler_params=_params(("arbitrary",)),
        name="sb_sample",
    )(page_table, bias, qbd, kn_t, vn_t, *([cache_k_t] * npages), *([cache_v_t] * npages))


def _xs_kernel(qx_ref, mk_ref, mv_ref, o_ref):
    step = pl.program_id(0)
    rows = N_MEM * X_HEADS
    own = ((lax.broadcasted_iota(I32, (SUBLANES, rows), 1) & (X_HEADS - 1))
           == lax.broadcasted_iota(I32, (SUBLANES, rows), 0))
    for t in range(XS_TOKENS):
        mk = mk_ref[t].astype(BF16)
        mv = mv_ref[t].astype(BF16)
        s = _dot_nt(qx_ref[t], mk) * (X_HEAD_DIM ** -0.5)
        s = jnp.where(own, s, -jnp.inf)
        m = jnp.max(s, axis=-1, keepdims=True)
        m = jnp.where(m == -jnp.inf, 0.0, m)
        e = jnp.exp(s - m)
        den = jnp.sum(e, axis=-1, keepdims=True)
        p = e / jnp.where(den == 0.0, 1.0, den)
        y = _dot(p.astype(BF16), mv)
        o_ref[pl.ds(step * XS_TOKENS + t, 1)] = y[None, 0:X_HEADS]


def xattn_sample(qx_bf, cache_mem_k, cache_mem_v):
    nb = qx_bf.shape[0]
    rows = N_MEM * X_HEADS
    mk = cache_mem_k.reshape(nb, rows, X_HEAD_DIM)
    mv = cache_mem_v.reshape(nb, rows, X_HEAD_DIM)
    qx = qx_bf.reshape(nb, X_HEADS, X_HEAD_DIM)
    qx = jnp.concatenate([qx, jnp.zeros((nb, SUBLANES - X_HEADS, X_HEAD_DIM), BF16)], axis=1)
    assert nb % XS_TOKENS == 0
    out = pl.pallas_call(
        _xs_kernel, grid=(nb // XS_TOKENS,),
        in_specs=[pl.BlockSpec((XS_TOKENS, SUBLANES, X_HEAD_DIM), lambda b: (b, 0, 0)),
                  pl.BlockSpec((XS_TOKENS, rows, X_HEAD_DIM), lambda b: (b, 0, 0)),
                  pl.BlockSpec((XS_TOKENS, rows, X_HEAD_DIM), lambda b: (b, 0, 0))],
        out_specs=pl.BlockSpec((nb, X_HEADS, X_HEAD_DIM), lambda b: (0, 0, 0)),
        out_shape=jax.ShapeDtypeStruct((nb, X_HEADS, X_HEAD_DIM), F32),
        compiler_params=_params(("arbitrary",)),
        name="xattn_sample",
    )(qx, mk, mv)
    return out.reshape(nb, X_HEADS * X_HEAD_DIM)


_R_E1, _R_E2, _R_R1, _R_R2, _R_G1, _R_G2 = 0, 1, 2, 3, 4, 5
_GROUP_LANE0 = N_EXPERTS


def _merge_kernel(tm, x_ref, mp_ref, ysb_ref, yx_ref, g1_ref, g2_ref, wb1_ref, wb2_ref, wo_ref,
                  n2_ref, wr_ref, br_ref, cin_ref, x1_ref, h_ref, route_ref, route_t_ref, cout_ref, cnt_ref):
    @pl.when(pl.program_id(0) == 0)
    def _():
        cnt_ref[...] = cin_ref[...]

    m = (mp_ref[...] + g1_ref[...] * _dot(ysb_ref[...].astype(BF16), wb1_ref[...])
         + g2_ref[...] * _dot(yx_ref[...].astype(BF16), wb2_ref[...]))
    x1 = x_ref[...] + _dot(m.astype(BF16), wo_ref[...])
    x1_ref[...] = x1
    hn = _rms(x1, n2_ref[...])
    h_ref[...] = hn
    logits = _dot(hn.astype(BF16), wr_ref[...]) + br_ref[...]

    lane = lax.broadcasted_iota(I32, (tm, LANES), 1).astype(F32)
    big = float(LANES)
    ninf = -jnp.inf

    def first_argmax(v):
        mx = jnp.max(v, axis=-1, keepdims=True)
        idx = jnp.min(jnp.where(v == mx, lane, big), axis=-1, keepdims=True)
        return mx, idx

    gl = jnp.where((lane >= _GROUP_LANE0) & (lane < _GROUP_LANE0 + N_GROUPS), logits, ninf)
    gmax, glane = first_argmax(gl)
    gprob = 1.0 / jnp.sum(jnp.exp(gl - gmax), axis=-1, keepdims=True)
    e0 = (glane - _GROUP_LANE0) * EXPERTS_PER_GROUP
    el = jnp.where((lane >= e0) & (lane < e0 + EXPERTS_PER_GROUP), logits, ninf)
    m1, i1 = first_argmax(el)
    m2, i2 = first_argmax(jnp.where(lane == i1, ninf, el))
    e2 = jnp.exp(m2 - m1)
    gate1 = gprob / (1.0 + e2)
    gate2 = gprob * e2 / (1.0 + e2)

    oh1 = (lane == i1).astype(F32)
    oh2 = (lane == i2).astype(F32)
    cnt = oh1 + oh2
    r = lax.broadcasted_iota(I32, (tm, tm), 0)
    c = lax.broadcasted_iota(I32, (tm, tm), 1)
    before = _dot((c < r).astype(BF16), cnt.astype(BF16)) + cnt_ref[...]
    rank1 = jnp.sum(oh1 * before, axis=-1, keepdims=True)
    rank2 = jnp.sum(oh2 * before, axis=-1, keepdims=True)
    cnt_ref[...] = cnt_ref[...] + jnp.sum(cnt, axis=0, keepdims=True)
    cout_ref[...] = cnt_ref[...]

    rec = jnp.zeros((tm, LANES), F32)
    for ln, val in ((_R_E1, i1.astype(F32)), (_R_E2, i2.astype(F32)), (_R_R1, rank1), (_R_R2, rank2),
                    (_R_G1, gate1), (_R_G2, gate2)):
        rec = jnp.where(lane == ln, val, rec)
    route_ref[...] = rec
    route_t_ref[...] = rec.T[0:SUBLANES]


def merge(x2d, mp, ysb, yx, g1, g2, wb1_bf, wb2_bf, wo_bf, norm2, w_route_bf, b_route, cnt_in, tm):
    n = x2d.shape[0]
    row = lambda w: pl.BlockSpec((tm, w), lambda i: (i, 0))
    return pl.pallas_call(
        functools.partial(_merge_kernel, tm), grid=(n // tm,),
        in_specs=[row(D_MODEL), row(D_MODEL), row(BRANCH_W), row(BRANCH_W), row(D_MODEL), row(D_MODEL),
                  _const_spec((BRANCH_W, D_MODEL)), _const_spec((BRANCH_W, D_MODEL)),
                  _const_spec((D_MODEL, D_MODEL)), _const_spec((1, D_MODEL)),
                  _const_spec((D_MODEL, LANES)), _const_spec((1, LANES)), _const_spec((1, LANES))],
        out_specs=[row(D_MODEL), row(D_MODEL), row(LANES), pl.BlockSpec((SUBLANES, tm), lambda i: (0, i)),
                   pl.BlockSpec((1, LANES), lambda i: (0, 0))],
        out_shape=[jax.ShapeDtypeStruct((n, D_MODEL), F32), jax.ShapeDtypeStruct((n, D_MODEL), F32),
                   jax.ShapeDtypeStruct((n, LANES), F32), jax.ShapeDtypeStruct((SUBLANES, n), F32),
                   jax.ShapeDtypeStruct((1, LANES), F32)],
        scratch_shapes=[pltpu.VMEM((1, LANES), F32)],
        compiler_params=_params(("arbitrary",)),
        name="merge",
    )(x2d, mp, ysb, yx, g1, g2, wb1_bf, wb2_bf, wo_bf, norm2.reshape(1, -1), w_route_bf, b_route, cnt_in)


def _row_copy(src, src_row, dst, dst_row, sem):
    return pltpu.make_async_copy(src.at[pl.ds(src_row, 1)], dst.at[pl.ds(dst_row, 1)], sem)


def _dispatch_kernel(tm, dest_ref, h_ref, buf_in_ref, buf_ref, sem):
    del buf_in_ref

    def issue(t, carry):
        _row_copy(h_ref, t, buf_ref, dest_ref[0, 0, t], sem).start(priority=0)
        _row_copy(h_ref, t, buf_ref, dest_ref[0, 0, tm + t], sem).start(priority=1)
        return carry

    lax.fori_loop(0, tm, issue, 0, unroll=ROW_DMA_UNROLL)
    for _ in range(2):
        pltpu.make_async_copy(h_ref, buf_ref.at[pl.ds(0, tm)], sem).wait()


def _slot_tiles(dest, tm):
    nt = dest.shape[1] // tm
    return dest.reshape(2, nt, tm).transpose(1, 0, 2).reshape(nt, 1, 2 * tm)


def dispatch(h, dest, buf_zero, tm):
    n = h.shape[0]
    nt = n // tm
    return pl.pallas_call(
        functools.partial(_dispatch_kernel, tm), grid=(nt,),
        in_specs=[pl.BlockSpec((1, 1, 2 * tm), lambda i: (i, 0, 0), memory_space=pltpu.SMEM),
                  pl.BlockSpec((tm, D_MODEL), lambda i: (i, 0)),
                  pl.BlockSpec(memory_space=pl.ANY)],
        out_specs=pl.BlockSpec(memory_space=pl.ANY),
        out_shape=jax.ShapeDtypeStruct(buf_zero.shape, buf_zero.dtype),
        scratch_shapes=[pltpu.SemaphoreType.DMA(())],
        input_output_aliases={2: 0},
        compiler_params=_params(("arbitrary",)),
        name="dispatch",
    )(_slot_tiles(dest, tm), h, buf_zero)


def _experts_kernel(be_ref, nu_ref, x_ref, w1_ref, w3_ref, w2_ref, y_ref):
    del be_ref

    @pl.when(pl.program_id(0) < nu_ref[0])
    def _():
        xb = x_ref[...].astype(BF16)
        h1 = _dot(xb, w1_ref[...].astype(BF16))
        h3 = _dot(xb, w3_ref[...].astype(BF16))
        act = (h1 * jax.nn.sigmoid(h1)) * h3
        y_ref[...] = _dot(act.astype(BF16), w2_ref[...].astype(BF16))

    @pl.when(pl.program_id(0) >= nu_ref[0])
    def _():
        y_ref[...] = jnp.zeros_like(y_ref)


def experts(buf, block_e, n_used, w1, w3, w2):
    nblk = buf.shape[0] // MOE_BM
    clamp = lambda i, nu: jnp.minimum(i, nu[0] - 1)
    grid_spec = pltpu.PrefetchScalarGridSpec(
        num_scalar_prefetch=2,
        grid=(nblk,),
        in_specs=[pl.BlockSpec((MOE_BM, D_MODEL), lambda i, be, nu: (clamp(i, nu), 0)),
                  pl.BlockSpec((None, D_MODEL, D_EXPERT), lambda i, be, nu: (be[i], 0, 0)),
                  pl.BlockSpec((None, D_MODEL, D_EXPERT), lambda i, be, nu: (be[i], 0, 0)),
                  pl.BlockSpec((None, D_EXPERT, D_MODEL), lambda i, be, nu: (be[i], 0, 0))],
        out_specs=pl.BlockSpec((MOE_BM, D_MODEL), lambda i, be, nu: (i, 0)),
    )
    return pl.pallas_call(
        _experts_kernel, grid_spec=grid_spec,
        out_shape=jax.ShapeDtypeStruct(buf.shape, F32),
        compiler_params=_params(("arbitrary",)),
        name="experts",
    )(block_e, n_used, buf, w1, w3, w2)


def _combine_kernel(tm, dest_ref, x1_ref, route_ref, ybuf_ref, o_ref, rows_ref, sem):
    def issue(t, carry):
        _row_copy(ybuf_ref, dest_ref[0, 0, t], rows_ref.at[0], t, sem).start(priority=0)
        _row_copy(ybuf_ref, dest_ref[0, 0, tm + t], rows_ref.at[1], t, sem).start(priority=1)
        return carry

    lax.fori_loop(0, tm, issue, 0, unroll=ROW_DMA_UNROLL)
    for k in range(2):
        pltpu.make_async_copy(ybuf_ref.at[pl.ds(0, tm)], rows_ref.at[k], sem).wait()
    route = route_ref[...]
    o_ref[...] = (x1_ref[...] + route[:, _R_G1:_R_G1 + 1] * rows_ref[0]
                  + route[:, _R_G2:_R_G2 + 1] * rows_ref[1])


def combine(x1, route, dest, ybuf, tm):
    n = x1.shape[0]
    nt = n // tm
    return pl.pallas_call(
        functools.partial(_combine_kernel, tm), grid=(nt,),
        in_specs=[pl.BlockSpec((1, 1, 2 * tm), lambda i: (i, 0, 0), memory_space=pltpu.SMEM),
                  pl.BlockSpec((tm, D_MODEL), lambda i: (i, 0)),
                  pl.BlockSpec((tm, LANES), lambda i: (i, 0)),
                  pl.BlockSpec(memory_space=pl.ANY)],
        out_specs=pl.BlockSpec((tm, D_MODEL), lambda i: (i, 0)),
        out_shape=jax.ShapeDtypeStruct((n, D_MODEL), F32),
        scratch_shapes=[pltpu.VMEM((2, tm, D_MODEL), F32), pltpu.SemaphoreType.DMA(())],
        compiler_params=_params(("arbitrary",)),
        name="combine",
    )(_slot_tiles(dest, tm), x1, route, ybuf)


def kernel(x_prompt, x_sample, mem_prompt, cache_k, cache_v, cache_mem_k, cache_mem_v, state_conv, page_table, norm1, w_in, conv_w, sb_bias, mem_norm, w_mem_kv, q_norm, k_norm, w_branch, w_o, norm2, w_group, b_group, w_router, b_router, w1, w3, w2):
    nb, seq, _ = x_prompt.shape
    ns, dec_seq, _ = x_sample.shape
    assert dec_seq == 1 and seq % SB_TQ == 0 and SB_TQ % (2 * SB_TK) == 0
    n_mem = mem_prompt.shape[1]
    sbw = SB_HEADS * SB_HEAD_DIM
    xw = X_HEADS * X_HEAD_DIM
    past = page_table.shape[1] * PAGE_SIZE

    assert w_in.shape[1] == _W_MAIN and _C_QX == _C_K + 2 * sbw and _C_GL == _C_QX + xw
    w_main = w_in.astype(BF16)
    w_kvt = kv_weight_t(w_in)
    wb = w_branch.astype(BF16)
    wo_bf = w_o.astype(BF16)
    w_route = jnp.zeros((D_MODEL, LANES), F32)
    w_route = w_route.at[:, :N_EXPERTS].set(w_router).at[:, N_EXPERTS:N_EXPERTS + N_GROUPS].set(w_group)
    b_route = jnp.zeros((1, LANES), F32)
    b_route = b_route.at[0, :N_EXPERTS].set(b_router).at[0, N_EXPERTS:N_EXPERTS + N_GROUPS].set(b_group)
    w_route = w_route.astype(BF16)

    xp = x_prompt.reshape(nb * seq, D_MODEL)
    mk_p, mv_p = memkv(mem_prompt.reshape(nb * n_mem, D_MODEL), mem_norm, w_mem_kv.astype(BF16), k_norm, n_mem)
    (kt_p, vt_p, q_p, qx_p, g1_p, g2_p, mp_p, tail_p, ktb_p, vtb_p) = inproj(
        xp, norm1, w_main, w_kvt, wb[0], conv_w, q_norm, seq, 512)
    ysb_p = sb_prompt(q_p, ktb_p, vtb_p, sb_bias, seq)
    yx_p = xattn_prompt(qx_p, mk_p.astype(BF16), mv_p.astype(BF16), seq, n_mem, 512)

    xs = x_sample.reshape(ns, D_MODEL)
    (kt_s, vt_s, q_s, qx_s, g1_s, g2_s, mp_s, u_s) = inproj(
        xs, norm1, w_main, w_kvt, wb[0], conv_w, q_norm, ns, ns,
        conv_state=(state_conv[:, 0], state_conv[:, 1]))
    ck_t = cache_k.transpose(0, 2, 3, 1).reshape(cache_k.shape[0], sbw, PAGE_SIZE)
    cv_t = cache_v.transpose(0, 2, 3, 1).reshape(cache_v.shape[0], sbw, PAGE_SIZE)
    ysb_s = sb_sample(q_s.astype(F32), kt_s[0], vt_s[0], ck_t, cv_t, page_table, sb_bias,
                      new_key_visible=(past < past))
    yx_s = xattn_sample(qx_s, cache_mem_k, cache_mem_v)

    cnt0 = jnp.zeros((1, LANES), F32)
    x1_p, h_p, route_p, rt_p, cnt1 = merge(xp, mp_p, ysb_p, yx_p, g1_p, g2_p, wb[1], wb[2], wo_bf, norm2,
                                           w_route, b_route, cnt0, 512)
    x1_s, h_s, route_s, rt_s, cnt2 = merge(xs, mp_s, ysb_s, yx_s, g1_s, g2_s, wb[1], wb[2], wo_bf, norm2,
                                           w_route, b_route, cnt1, ns)

    n_slots = 2 * (nb * seq + ns)
    nblk = -(-(n_slots + N_EXPERTS * (MOE_BM - 1)) // MOE_BM)
    counts = cnt2[0, :N_EXPERTS].astype(I32)
    padded = (counts + MOE_BM - 1) // MOE_BM * MOE_BM
    ends = jnp.cumsum(padded)
    starts = ends - padded
    n_used = (ends[-1] // MOE_BM).astype(I32).reshape(1)
    blk_start = jnp.arange(nblk, dtype=I32) * MOE_BM
    block_e = jnp.sum(blk_start[:, None] >= ends[None, :], axis=1).astype(I32)
    block_e = jnp.minimum(block_e, block_e[jnp.maximum(n_used[0] - 1, 0)])

    def dests(route_t):
        e = route_t[_R_E1:_R_E2 + 1].astype(I32)
        rk = route_t[_R_R1:_R_R2 + 1].astype(I32)
        first = jnp.zeros_like(e)
        for x in range(N_EXPERTS):
            first = jnp.where(e == x, starts[x], first)
        return first + rk

    dest_p, dest_s = dests(rt_p), dests(rt_s)
    buf = jnp.zeros((nblk * MOE_BM, D_MODEL), F32)
    buf = dispatch(h_p, dest_p, buf, 512)
    buf = dispatch(h_s, dest_s, buf, ns)
    ybuf = experts(buf, block_e, n_used, w1, w3, w2)
    y_p = combine(x1_p, route_p, dest_p, ybuf, 512)
    y_s = combine(x1_s, route_s, dest_s, ybuf, ns)

    def heads_last(t, b, s):
        return t.reshape(b, SB_HEADS, SB_HEAD_DIM, s).transpose(0, 3, 1, 2)

    conv_prompt = tail_p[:, SUBLANES - 2:, :]
    conv_sample = jnp.stack([state_conv[:, 1], u_s[0]], axis=1)
    return (y_p.reshape(nb, seq, D_MODEL), y_s.reshape(ns, 1, D_MODEL),
            heads_last(kt_p, nb, seq), heads_last(vt_p, nb, seq), conv_prompt,
            mk_p.reshape(nb, n_mem, X_HEADS, X_HEAD_DIM), mv_p.reshape(nb, n_mem, X_HEADS, X_HEAD_DIM),
            heads_last(kt_s, 1, ns).reshape(ns, 1, SB_HEADS, SB_HEAD_DIM),
            heads_last(vt_s, 1, ns).reshape(ns, 1, SB_HEADS, SB_HEAD_DIM),
            conv_sample)
```

```python
import functools

import jax
import jax.numpy as jnp
from jax import lax
from jax.experimental import pallas as pl
from jax.experimental.pallas import tpu as pltpu

F32 = jnp.float32
BF16 = jnp.bfloat16
I32 = jnp.int32

EPS = 1e-6
D_MODEL = 1024
BRANCH_W = 512
SB_HEADS = 8
SB_HEAD_DIM = 64
X_HEADS = 4
X_HEAD_DIM = 128
N_MEM = 256
N_GROUPS = 4
EXPERTS_PER_GROUP = 8
N_EXPERTS = N_GROUPS * EXPERTS_PER_GROUP
D_EXPERT = 512
PAGE_SIZE = 128

LANES = 128
SUBLANES = 8
VMEM_LIMIT = 56 * 1024 * 1024

SB_TK = 256
SB_TQ = 8 * SB_TK
MOE_BM = 256
XS_TOKENS = 8
ROW_DMA_UNROLL = 8


def _dot(a, b):
    return jnp.dot(a, b, preferred_element_type=F32)


def _dot_nt(a, b):
    return lax.dot_general(a, b, (((1,), (1,)), ((), ())), preferred_element_type=F32)


def _rms(x, g):
    return x * lax.rsqrt(jnp.mean(x * x, axis=-1, keepdims=True) + EPS) * g


def _softplus(z):
    return jnp.maximum(z, 0.0) + jnp.log(1.0 + jnp.exp(-jnp.abs(z)))


def _params(sem, vmem=VMEM_LIMIT):
    return pltpu.CompilerParams(dimension_semantics=sem, vmem_limit_bytes=vmem)


def _const_spec(shape):
    nd = len(shape)
    return pl.BlockSpec(shape, lambda *_: (0,) * nd, pipeline_mode=pl.Buffered(1))


def _memkv_kernel(mem_ref, g_ref, w_ref, kn_ref, mk_ref, mv_ref):
    xn = _rms(mem_ref[...], g_ref[...]).astype(BF16)
    kv = _dot(xn, w_ref[...])
    xw = X_HEADS * X_HEAD_DIM
    for h in range(X_HEADS):
        sl = slice(h * X_HEAD_DIM, (h + 1) * X_HEAD_DIM)
        mk_ref[:, sl] = _rms(kv[:, sl], kn_ref[...])
    mv_ref[...] = kv[:, xw:]


def memkv(mem2d, mem_norm, w_mem_kv_bf, k_norm, n_mem):
    n = mem2d.shape[0]
    xw = X_HEADS * X_HEAD_DIM
    return pl.pallas_call(
        _memkv_kernel,
        grid=(n // n_mem,),
        in_specs=[pl.BlockSpec((n_mem, D_MODEL), lambda i: (i, 0)),
                  _const_spec((1, D_MODEL)),
                  _const_spec((D_MODEL, 2 * xw)),
                  _const_spec((1, X_HEAD_DIM))],
        out_specs=[pl.BlockSpec((n_mem, xw), lambda i: (i, 0)),
                   pl.BlockSpec((n_mem, xw), lambda i: (i, 0))],
        out_shape=[jax.ShapeDtypeStruct((n, xw), F32), jax.ShapeDtypeStruct((n, xw), F32)],
        compiler_params=_params(("arbitrary",)),
        name="memkv",
    )(mem2d, mem_norm.reshape(1, -1), w_mem_kv_bf, k_norm.reshape(1, -1))


_C_UIN, _C_CG, _C_BG, _C_Q, _C_K, _C_QX, _C_GL = 0, 512, 1024, 1536, 2048, 3072, 3584
_W_MAIN = _C_GL + 3 * D_MODEL


def _kvt_kernel(w_ref, o_ref):
    o_ref[...] = w_ref[...].T.astype(BF16)


def kv_weight_t(w_in):
    kvw = 2 * SB_HEADS * SB_HEAD_DIM
    assert _C_K % kvw == 0
    return pl.pallas_call(
        _kvt_kernel, grid=(1,),
        in_specs=[pl.BlockSpec((D_MODEL, kvw), lambda i: (0, _C_K // kvw))],
        out_specs=pl.BlockSpec((kvw, D_MODEL), lambda i: (0, 0)),
        out_shape=jax.ShapeDtypeStruct((kvw, D_MODEL), BF16),
        compiler_params=_params(("arbitrary",)),
        name="kv_weight_t",
    )(w_in)


def _inproj_kernel(carry_conv, tiles_per_seq, tm, *refs):
    if carry_conv:
        (x_ref, n1_ref, wm_ref, wkvt_ref, wb0_ref, cw_ref, qn_ref,
         kt_ref, vt_ref, q_ref, qx_ref, g1_ref, g2_ref, mp_ref, tail_ref, ktb_ref, vtb_ref,
         carry_ref) = refs
    else:
        (x_ref, n1_ref, wm_ref, wkvt_ref, wb0_ref, cw_ref, qn_ref, um2_ref, um1_ref,
         kt_ref, vt_ref, q_ref, qx_ref, g1_ref, g2_ref, mp_ref, tail_ref) = refs

    xn = _rms(x_ref[...], n1_ref[...]).astype(BF16)

    def proj(c0, width):
        return _dot(xn, wm_ref[:, c0:c0 + width])

    kvt = _dot_nt(wkvt_ref[...], xn)
    sbw = SB_HEADS * SB_HEAD_DIM
    kt_ref[...] = kvt[:sbw]
    vt_ref[...] = kvt[sbw:]
    if carry_conv:
        for c in range(tm // SB_TK):
            ktb_ref[c] = kvt[:sbw, c * SB_TK:(c + 1) * SB_TK].astype(BF16)
            vtb_ref[c] = kvt[sbw:, c * SB_TK:(c + 1) * SB_TK].astype(BF16)

    q_ref[...] = (proj(_C_Q, sbw) * (SB_HEAD_DIM ** -0.5)).astype(BF16)

    qx = proj(_C_QX, X_HEADS * X_HEAD_DIM)
    for h in range(X_HEADS):
        sl = slice(h * X_HEAD_DIM, (h + 1) * X_HEAD_DIM)
        qx_ref[:, sl] = _rms(qx[:, sl], qn_ref[...]).astype(BF16)

    u = proj(_C_CG, BRANCH_W) * proj(_C_UIN, BRANCH_W)
    if carry_conv:
        @pl.when(pl.program_id(0) % tiles_per_seq == 0)
        def _():
            carry_ref[...] = jnp.zeros_like(carry_ref)
        ext = jnp.concatenate([carry_ref[...], u], axis=0)
        um1 = pltpu.roll(ext, 1, 0)[SUBLANES:]
        um2 = pltpu.roll(ext, 2, 0)[SUBLANES:]
        carry_ref[...] = u[tm - SUBLANES:]
        tail_ref[...] = u[tm - SUBLANES:]
    else:
        um1 = um1_ref[...]
        um2 = um2_ref[...]
        tail_ref[...] = u
    cw = cw_ref[...]
    y_conv = proj(_C_BG, BRANCH_W) * (um2 * cw[0:1] + um1 * cw[1:2] + u * cw[2:3])

    g0 = jax.nn.sigmoid(proj(_C_GL, D_MODEL))
    mp_ref[...] = g0 * _dot(y_conv.astype(BF16), wb0_ref[...])
    g1_ref[...] = jax.nn.sigmoid(proj(_C_GL + D_MODEL, D_MODEL))
    g2_ref[...] = jax.nn.sigmoid(proj(_C_GL + 2 * D_MODEL, D_MODEL))


def inproj(x2d, norm1, w_main_bf, w_kvt_bf, wb0_bf, conv_w, q_norm, seq_len, tm, conv_state=None):
    n = x2d.shape[0]
    carry = conv_state is None
    nseq = n // seq_len if carry else 1
    slen = seq_len if carry else n
    tps = slen // tm
    sbw = SB_HEADS * SB_HEAD_DIM
    xw = X_HEADS * X_HEAD_DIM
    nkb = max(tm // SB_TK, 1)
    tail_rows = SUBLANES if carry else tm

    row = lambda w: pl.BlockSpec((tm, w), lambda i: (i, 0))
    kt_spec = pl.BlockSpec((None, sbw, tm), lambda i: (i // tps, 0, i % tps))
    ktb_spec = pl.BlockSpec((None, nkb, sbw, SB_TK), lambda i: (i // tps, i % tps, 0, 0))
    in_specs = [row(D_MODEL), _const_spec((1, D_MODEL)), _const_spec((D_MODEL, _W_MAIN)),
                _const_spec((2 * sbw, D_MODEL)), _const_spec((BRANCH_W, D_MODEL)),
                _const_spec((3, BRANCH_W)), _const_spec((1, X_HEAD_DIM))]
    args = [x2d, norm1.reshape(1, -1), w_main_bf, w_kvt_bf, wb0_bf, conv_w, q_norm.reshape(1, -1)]
    scratch = []
    if carry:
        scratch = [pltpu.VMEM((SUBLANES, BRANCH_W), F32)]
    else:
        in_specs += [row(BRANCH_W), row(BRANCH_W)]
        args += list(conv_state)
    out_specs = [kt_spec, kt_spec, row(sbw), row(xw),
                 row(D_MODEL), row(D_MODEL), row(D_MODEL),
                 pl.BlockSpec((None, tail_rows, BRANCH_W), lambda i: (i // tps, 0, 0))]
    out_shape = [jax.ShapeDtypeStruct((nseq, sbw, slen), F32), jax.ShapeDtypeStruct((nseq, sbw, slen), F32),
                 jax.ShapeDtypeStruct((n, sbw), BF16), jax.ShapeDtypeStruct((n, xw), BF16),
                 jax.ShapeDtypeStruct((n, D_MODEL), F32), jax.ShapeDtypeStruct((n, D_MODEL), F32),
                 jax.ShapeDtypeStruct((n, D_MODEL), F32),
                 jax.ShapeDtypeStruct((nseq, tail_rows, BRANCH_W), F32)]
    if carry:
        out_specs += [ktb_spec, ktb_spec]
        out_shape += [jax.ShapeDtypeStruct((nseq, slen // SB_TK, sbw, SB_TK), BF16)] * 2
    return pl.pallas_call(
        functools.partial(_inproj_kernel, carry, tps, tm),
        grid=(n // tm,),
        in_specs=in_specs, out_specs=out_specs, out_shape=out_shape,
        scratch_shapes=scratch,
        compiler_params=_params(("arbitrary",)),
        name="inproj_prompt" if carry else "inproj_sample",
    )(*args)


def _sbp_kernel(bias_ref, q_ref, kt_ref, vt_ref, o_ref, acc_ref, car_ref):
    pair = pl.program_id(1)
    qi = pl.program_id(2)
    tq, tk = SB_TQ, SB_TK
    lane = lax.broadcasted_iota(I32, (tq, LANES), 1)
    q = q_ref[...]
    zero = jnp.zeros_like(q)
    qh = (jnp.where(lane < SB_HEAD_DIM, q, zero), jnp.where(lane >= SB_HEAD_DIM, q, zero))
    r = lax.broadcasted_iota(I32, (tk, tk), 0)
    c = lax.broadcasted_iota(I32, (tk, tk), 1)
    tri = (r >= c).astype(BF16)
    qrow = lax.broadcasted_iota(I32, (tq, tk), 0)
    kcol = lax.broadcasted_iota(I32, (tq, tk), 1)
    acc_ref[...] = jnp.zeros_like(acc_ref)
    car_ref[...] = jnp.zeros_like(car_ref)

    def block(j, diag_sub):
        kt = kt_ref[j]
        vt = vt_ref[j]
        rs = slice(0, tq)
        if diag_sub is not None:
            rs = slice(diag_sub * tk, tq)
            causal = (kcol + diag_sub * tk < qrow)[rs]
        for h in range(2):
            z = _dot(qh[h][rs], kt) + bias_ref[2 * pair + h]
            sp = _softplus(z)
            if diag_sub is not None:
                sp = jnp.where(causal, sp, 0.0)
            car = car_ref[h, rs]
            cum = _dot(sp.astype(BF16), tri)
            a = jnp.exp(z - (cum + jnp.concatenate([car] * (tk // LANES), axis=1)))
            if diag_sub is not None:
                a = jnp.where(causal, a, 0.0)
            acc_ref[h, rs] += _dot_nt(a.astype(BF16), vt)
            car_ref[h, rs] = car + jnp.broadcast_to(cum[:, 0:1], car.shape)

    nsub = tq // tk
    for s in reversed(range(nsub)):
        block(qi * nsub + s, s)

    def body(t, carry):
        block(qi * nsub - 1 - 2 * t, None)
        block(qi * nsub - 2 - 2 * t, None)
        return carry

    lax.fori_loop(0, qi * (nsub // 2), body, 0)
    o_ref[...] = jnp.where(lane < SB_HEAD_DIM, acc_ref[0], acc_ref[1]).astype(o_ref.dtype)


def sb_prompt(q_bf, ktb, vtb, sb_bias, seq_len):
    n = q_bf.shape[0]
    nseq = n // seq_len
    nq = seq_len // SB_TQ
    nk = seq_len // SB_TK
    npair = SB_HEADS // 2
    grid_spec = pltpu.PrefetchScalarGridSpec(
        num_scalar_prefetch=0,
        grid=(nseq, npair, nq),
        in_specs=[pl.BlockSpec(memory_space=pltpu.SMEM),
                  pl.BlockSpec((SB_TQ, LANES), lambda b, p, i: (b * nq + i, p)),
                  pl.BlockSpec((None, nk, LANES, SB_TK), lambda b, p, i: (b, 0, p, 0)),
                  pl.BlockSpec((None, nk, LANES, SB_TK), lambda b, p, i: (b, 0, p, 0))],
        out_specs=pl.BlockSpec((SB_TQ, LANES), lambda b, p, i: (b * nq + i, p)),
        scratch_shapes=[pltpu.VMEM((2, SB_TQ, LANES), F32), pltpu.VMEM((2, SB_TQ, LANES), F32)],
    )
    return pl.pallas_call(
        _sbp_kernel, grid_spec=grid_spec,
        out_shape=jax.ShapeDtypeStruct((n, SB_HEADS * SB_HEAD_DIM), BF16),
        compiler_params=_params(("arbitrary", "arbitrary", "arbitrary")),
        name="sb_prompt",
    )(sb_bias.astype(F32), q_bf, ktb, vtb)


def _xp_kernel(qx_ref, mk_ref, mv_ref, o_ref):
    for h in range(X_HEADS):
        sl = slice(h * X_HEAD_DIM, (h + 1) * X_HEAD_DIM)
        s = _dot_nt(qx_ref[:, sl], mk_ref[:, sl]) * (X_HEAD_DIM ** -0.5)
        e = jnp.exp(s - jnp.max(s, axis=-1, keepdims=True))
        p = e / jnp.sum(e, axis=-1, keepdims=True)
        o_ref[:, sl] = _dot(p.astype(BF16), mv_ref[:, sl]).astype(o_ref.dtype)


def xattn_prompt(qx_bf, mk_bf, mv_bf, seq_len, n_mem, tm):
    n = qx_bf.shape[0]
    tps = seq_len // tm
    xw = X_HEADS * X_HEAD_DIM
    return pl.pallas_call(
        _xp_kernel, grid=(n // tm,),
        in_specs=[pl.BlockSpec((tm, xw), lambda i: (i, 0)),
                  pl.BlockSpec((n_mem, xw), lambda i: (i // tps, 0)),
                  pl.BlockSpec((n_mem, xw), lambda i: (i // tps, 0))],
        out_specs=pl.BlockSpec((tm, xw), lambda i: (i, 0)),
        out_shape=jax.ShapeDtypeStruct((n, xw), BF16),
        compiler_params=_params(("arbitrary",)),
        name="xattn_prompt",
    )(qx_bf, mk_bf, mv_bf)


def _split_bf16(x):
    hi = x.astype(BF16)
    return hi, (x - hi.astype(F32)).astype(BF16)


def _sbs_kernel(new_key_visible, npages, pt_ref, bias_ref, qbd_ref, kn_ref, vn_ref, *refs):
    del pt_ref
    k_refs, v_refs, o_ref = refs[:npages], refs[npages:2 * npages], refs[2 * npages]
    b = pl.program_id(0)
    psz = PAGE_SIZE
    sbw = SB_HEADS * SB_HEAD_DIM
    rows = npages * SB_HEADS
    head_bits = SB_HEADS.bit_length() - 1
    qbd = qbd_ref[0]
    bias = bias_ref[...]

    mine = lax.broadcasted_iota(I32, (SB_HEADS, kn_ref.shape[1]), 1) == b
    zs = _dot(qbd, kn_ref[...].astype(BF16)) + bias[:, 0:1]
    seen = jnp.logical_and(mine, new_key_visible)
    sp0 = jnp.where(seen, _softplus(zs), 0.0)
    a0 = jnp.where(seen, jnp.exp(zs - sp0), 0.0)
    sp0_tot = jnp.sum(sp0, axis=-1, keepdims=True)
    a0_rows = jnp.concatenate(
        [jnp.broadcast_to(a0[h:h + 1], (SB_HEAD_DIM, a0.shape[1])) for h in range(SB_HEADS)], axis=0)
    new = jnp.sum(a0_rows * vn_ref[...], axis=-1, keepdims=True)

    z = jnp.concatenate([_dot(qbd, k_refs[j][...].astype(BF16)) + bias for j in range(npages)], axis=0)
    sp = _softplus(z)
    r = lax.broadcasted_iota(I32, (psz, psz), 0)
    c = lax.broadcasted_iota(I32, (psz, psz), 1)
    tri = (r >= c).astype(BF16)
    hi, lo = _split_bf16(sp)
    cum = _dot(hi, tri) + _dot(lo, tri)
    tot = jnp.broadcast_to(cum[:, 0:1], (rows, psz))
    rr = lax.broadcasted_iota(I32, (rows, rows), 0)
    cc = lax.broadcasted_iota(I32, (rows, rows), 1)
    later = jnp.logical_and((cc >> head_bits) > (rr >> head_bits),
                            (cc & (SB_HEADS - 1)) == (rr & (SB_HEADS - 1))).astype(BF16)
    th, tl = _split_bf16(tot)
    carry = _dot(later, th) + _dot(later, tl) + jnp.concatenate([sp0_tot] * npages, axis=0)
    a = jnp.exp(z - cum - carry)

    accs = []
    for h in range(SB_HEADS):
        hs = slice(h * SB_HEAD_DIM, (h + 1) * SB_HEAD_DIM)
        acc = jnp.zeros((SB_HEAD_DIM, psz), F32)
        for j in range(npages):
            rj = j * SB_HEADS + h
            acc = acc + jnp.broadcast_to(a[rj:rj + 1], (SB_HEAD_DIM, psz)) * v_refs[j][hs, :]
        accs.append(acc)
    acc = jnp.concatenate(accs, axis=0)
    lane0 = lax.broadcasted_iota(I32, (sbw, psz), 1) == 0
    acc = acc + jnp.where(lane0, new, 0.0)
    hi2, lo2 = _split_bf16(acc)
    ones = jnp.ones((SUBLANES, psz), BF16)
    y = _dot_nt(ones, hi2) + _dot_nt(ones, lo2)
    o_ref[pl.ds(b, 1), :] = y[0:1]


def sb_sample(q_s, kn_t, vn_t, cache_k_t, cache_v_t, page_table, sb_bias, new_key_visible):
    nb, npages = page_table.shape
    sbw = SB_HEADS * SB_HEAD_DIM
    head_of_col = jnp.arange(sbw) // SB_HEAD_DIM
    sel = head_of_col[None, :] == jnp.arange(SB_HEADS)[:, None]
    qbd = jnp.where(sel[None], q_s[:, None, :], 0.0).astype(BF16)
    assert PAGE_SIZE == LANES
    bias = jnp.broadcast_to(sb_bias.astype(F32)[:, None], (SB_HEADS, LANES))
    page = lambda j: pl.BlockSpec((None, sbw, PAGE_SIZE), lambda b, pt: (pt[b, j], 0, 0))
    grid_spec = pltpu.PrefetchScalarGridSpec(
        num_scalar_prefetch=1,
        grid=(nb,),
        in_specs=[pl.BlockSpec((SB_HEADS, LANES), lambda b, pt: (0, 0)),
                  pl.BlockSpec((1, SB_HEADS, sbw), lambda b, pt: (b, 0, 0)),
                  pl.BlockSpec((sbw, nb), lambda b, pt: (0, 0)),
                  pl.BlockSpec((sbw, nb), lambda b, pt: (0, 0))]
                 + [page(j) for j in range(npages)] * 2,
        out_specs=pl.BlockSpec((nb, sbw), lambda b, pt: (0, 0)),
    )
    return pl.pallas_call(
        functools.partial(_sbs_kernel, new_key_visible, npages), grid_spec=grid_spec,
        out_shape=jax.ShapeDtypeStruct((nb, sbw), F32),
        compiler_params=_params(("arbitrary",)),
        name="sb_sample",
    )(page_table, bias, qbd, kn_t, vn_t, *([cache_k_t] * npages), *([cache_v_t] * npages))


def _xs_kernel(qx_ref, mk_ref, mv_ref, o_ref):
    step = pl.program_id(0)
    rows = N_MEM * X_HEADS
    own = ((lax.broadcasted_iota(I32, (SUBLANES, rows), 1) & (X_HEADS - 1))
           == lax.broadcasted_iota(I32, (SUBLANES, rows), 0))
    for t in range(XS_TOKENS):
        mk = mk_ref[t].astype(BF16)
        mv = mv_ref[t].astype(BF16)
        s = _dot_nt(qx_ref[t], mk) * (X_HEAD_DIM ** -0.5)
        s = jnp.where(own, s, -jnp.inf)
        m = jnp.max(s, axis=-1, keepdims=True)
        m = jnp.where(m == -jnp.inf, 0.0, m)
        e = jnp.exp(s - m)
        den = jnp.sum(e, axis=-1, keepdims=True)
        p = e / jnp.where(den == 0.0, 1.0, den)
        y = _dot(p.astype(BF16), mv)
        o_ref[pl.ds(step * XS_TOKENS + t, 1)] = y[None, 0:X_HEADS]


def xattn_sample(qx_bf, cache_mem_k, cache_mem_v):
    nb = qx_bf.shape[0]
    rows = N_MEM * X_HEADS
    mk = cache_mem_k.reshape(nb, rows, X_HEAD_DIM)
    mv = cache_mem_v.reshape(nb, rows, X_HEAD_DIM)
    qx = qx_bf.reshape(nb, X_HEADS, X_HEAD_DIM)
    qx = jnp.concatenate([qx, jnp.zeros((nb, SUBLANES - X_HEADS, X_HEAD_DIM), BF16)], axis=1)
    assert nb % XS_TOKENS == 0
    out = pl.pallas_call(
        _xs_kernel, grid=(nb // XS_TOKENS,),
        in_specs=[pl.BlockSpec((XS_TOKENS, SUBLANES, X_HEAD_DIM), lambda b: (b, 0, 0)),
                  pl.BlockSpec((XS_TOKENS, rows, X_HEAD_DIM), lambda b: (b, 0, 0)),
                  pl.BlockSpec((XS_TOKENS, rows, X_HEAD_DIM), lambda b: (b, 0, 0))],
        out_specs=pl.BlockSpec((nb, X_HEADS, X_HEAD_DIM), lambda b: (0, 0, 0)),
        out_shape=jax.ShapeDtypeStruct((nb, X_HEADS, X_HEAD_DIM), F32),
        compiler_params=_params(("arbitrary",)),
        name="xattn_sample",
    )(qx, mk, mv)
    return out.reshape(nb, X_HEADS * X_HEAD_DIM)


_R_E1, _R_E2, _R_R1, _R_R2, _R_G1, _R_G2 = 0, 1, 2, 3, 4, 5
_GROUP_LANE0 = N_EXPERTS


def _merge_kernel(tm, x_ref, mp_ref, ysb_ref, yx_ref, g1_ref, g2_ref, wb1_ref, wb2_ref, wo_ref,
                  n2_ref, wr_ref, br_ref, cin_ref, x1_ref, h_ref, route_ref, route_t_ref, cout_ref, cnt_ref):
    @pl.when(pl.program_id(0) == 0)
    def _():
        cnt_ref[...] = cin_ref[...]

    m = (mp_ref[...] + g1_ref[...] * _dot(ysb_ref[...].astype(BF16), wb1_ref[...])
         + g2_ref[...] * _dot(yx_ref[...].astype(BF16), wb2_ref[...]))
    x1 = x_ref[...] + _dot(m.astype(BF16), wo_ref[...])
    x1_ref[...] = x1
    hn = _rms(x1, n2_ref[...])
    h_ref[...] = hn
    logits = _dot(hn.astype(BF16), wr_ref[...]) + br_ref[...]

    lane = lax.broadcasted_iota(I32, (tm, LANES), 1).astype(F32)
    big = float(LANES)
    ninf = -jnp.inf

    def first_argmax(v):
        mx = jnp.max(v, axis=-1, keepdims=True)
        idx = jnp.min(jnp.where(v == mx, lane, big), axis=-1, keepdims=True)
        return mx, idx

    gl = jnp.where((lane >= _GROUP_LANE0) & (lane < _GROUP_LANE0 + N_GROUPS), logits, ninf)
    gmax, glane = first_argmax(gl)
    gprob = 1.0 / jnp.sum(jnp.exp(gl - gmax), axis=-1, keepdims=True)
    e0 = (glane - _GROUP_LANE0) * EXPERTS_PER_GROUP
    el = jnp.where((lane >= e0) & (lane < e0 + EXPERTS_PER_GROUP), logits, ninf)
    m1, i1 = first_argmax(el)
    m2, i2 = first_argmax(jnp.where(lane == i1, ninf, el))
    e2 = jnp.exp(m2 - m1)
    gate1 = gprob / (1.0 + e2)
    gate2 = gprob * e2 / (1.0 + e2)

    oh1 = (lane == i1).astype(F32)
    oh2 = (lane == i2).astype(F32)
    cnt = oh1 + oh2
    r = lax.broadcasted_iota(I32, (tm, tm), 0)
    c = lax.broadcasted_iota(I32, (tm, tm), 1)
    before = _dot((c < r).astype(BF16), cnt.astype(BF16)) + cnt_ref[...]
    rank1 = jnp.sum(oh1 * before, axis=-1, keepdims=True)
    rank2 = jnp.sum(oh2 * before, axis=-1, keepdims=True)
    cnt_ref[...] = cnt_ref[...] + jnp.sum(cnt, axis=0, keepdims=True)
    cout_ref[...] = cnt_ref[...]

    rec = jnp.zeros((tm, LANES), F32)
    for ln, val in ((_R_E1, i1.astype(F32)), (_R_E2, i2.astype(F32)), (_R_R1, rank1), (_R_R2, rank2),
                    (_R_G1, gate1), (_R_G2, gate2)):
        rec = jnp.where(lane == ln, val, rec)
    route_ref[...] = rec
    route_t_ref[...] = rec.T[0:SUBLANES]


def merge(x2d, mp, ysb, yx, g1, g2, wb1_bf, wb2_bf, wo_bf, norm2, w_route_bf, b_route, cnt_in, tm):
    n = x2d.shape[0]
    row = lambda w: pl.BlockSpec((tm, w), lambda i: (i, 0))
    return pl.pallas_call(
        functools.partial(_merge_kernel, tm), grid=(n // tm,),
        in_specs=[row(D_MODEL), row(D_MODEL), row(BRANCH_W), row(BRANCH_W), row(D_MODEL), row(D_MODEL),
                  _const_spec((BRANCH_W, D_MODEL)), _const_spec((BRANCH_W, D_MODEL)),
                  _const_spec((D_MODEL, D_MODEL)), _const_spec((1, D_MODEL)),
                  _const_spec((D_MODEL, LANES)), _const_spec((1, LANES)), _const_spec((1, LANES))],
        out_specs=[row(D_MODEL), row(D_MODEL), row(LANES), pl.BlockSpec((SUBLANES, tm), lambda i: (0, i)),
                   pl.BlockSpec((1, LANES), lambda i: (0, 0))],
        out_shape=[jax.ShapeDtypeStruct((n, D_MODEL), F32), jax.ShapeDtypeStruct((n, D_MODEL), F32),
                   jax.ShapeDtypeStruct((n, LANES), F32), jax.ShapeDtypeStruct((SUBLANES, n), F32),
                   jax.ShapeDtypeStruct((1, LANES), F32)],
        scratch_shapes=[pltpu.VMEM((1, LANES), F32)],
        compiler_params=_params(("arbitrary",)),
        name="merge",
    )(x2d, mp, ysb, yx, g1, g2, wb1_bf, wb2_bf, wo_bf, norm2.reshape(1, -1), w_route_bf, b_route, cnt_in)


def _row_copy(src, src_row, dst, dst_row, sem):
    return pltpu.make_async_copy(src.at[pl.ds(src_row, 1)], dst.at[pl.ds(dst_row, 1)], sem)


def _dispatch_kernel(tm, dest_ref, h_ref, buf_in_ref, buf_ref, sem):
    del buf_in_ref

    def issue(t, carry):
        _row_copy(h_ref, t, buf_ref, dest_ref[0, 0, t], sem).start(priority=0)
        _row_copy(h_ref, t, buf_ref, dest_ref[0, 0, tm + t], sem).start(priority=1)
        return carry

    lax.fori_loop(0, tm, issue, 0, unroll=ROW_DMA_UNROLL)
    for _ in range(2):
        pltpu.make_async_copy(h_ref, buf_ref.at[pl.ds(0, tm)], sem).wait()


def _slot_tiles(dest, tm):
    nt = dest.shape[1] // tm
    return dest.reshape(2, nt, tm).transpose(1, 0, 2).reshape(nt, 1, 2 * tm)


def dispatch(h, dest, buf_zero, tm):
    n = h.shape[0]
    nt = n // tm
    return pl.pallas_call(
        functools.partial(_dispatch_kernel, tm), grid=(nt,),
        in_specs=[pl.BlockSpec((1, 1, 2 * tm), lambda i: (i, 0, 0), memory_space=pltpu.SMEM),
                  pl.BlockSpec((tm, D_MODEL), lambda i: (i, 0)),
                  pl.BlockSpec(memory_space=pl.ANY)],
        out_specs=pl.BlockSpec(memory_space=pl.ANY),
        out_shape=jax.ShapeDtypeStruct(buf_zero.shape, buf_zero.dtype),
        scratch_shapes=[pltpu.SemaphoreType.DMA(())],
        input_output_aliases={2: 0},
        compiler_params=_params(("arbitrary",)),
        name="dispatch",
    )(_slot_tiles(dest, tm), h, buf_zero)


def _experts_kernel(be_ref, nu_ref, x_ref, w1_ref, w3_ref, w2_ref, y_ref):
    del be_ref

    @pl.when(pl.program_id(0) < nu_ref[0])
    def _():
        xb = x_ref[...].astype(BF16)
        h1 = _dot(xb, w1_ref[...].astype(BF16))
        h3 = _dot(xb, w3_ref[...].astype(BF16))
        act = (h1 * jax.nn.sigmoid(h1)) * h3
        y_ref[...] = _dot(act.astype(BF16), w2_ref[...].astype(BF16))

    @pl.when(pl.program_id(0) >= nu_ref[0])
    def _():
        y_ref[...] = jnp.zeros_like(y_ref)


def experts(buf, block_e, n_used, w1, w3, w2):
    nblk = buf.shape[0] // MOE_BM
    clamp = lambda i, nu: jnp.minimum(i, nu[0] - 1)
    grid_spec = pltpu.PrefetchScalarGridSpec(
        num_scalar_prefetch=2,
        grid=(nblk,),
        in_specs=[pl.BlockSpec((MOE_BM, D_MODEL), lambda i, be, nu: (clamp(i, nu), 0)),
                  pl.BlockSpec((None, D_MODEL, D_EXPERT), lambda i, be, nu: (be[i], 0, 0)),
                  pl.BlockSpec((None, D_MODEL, D_EXPERT), lambda i, be, nu: (be[i], 0, 0)),
                  pl.BlockSpec((None, D_EXPERT, D_MODEL), lambda i, be, nu: (be[i], 0, 0))],
        out_specs=pl.BlockSpec((MOE_BM, D_MODEL), lambda i, be, nu: (i, 0)),
    )
    return pl.pallas_call(
        _experts_kernel, grid_spec=grid_spec,
        out_shape=jax.ShapeDtypeStruct(buf.shape, F32),
        compiler_params=_params(("arbitrary",)),
        name="experts",
    )(block_e, n_used, buf, w1, w3, w2)


def _combine_kernel(tm, dest_ref, x1_ref, route_ref, ybuf_ref, o_ref, rows_ref, sem):
    def issue(t, carry):
        _row_copy(ybuf_ref, dest_ref[0, 0, t], rows_ref.at[0], t, sem).start(priority=0)
        _row_copy(ybuf_ref, dest_ref[0, 0, tm + t], rows_ref.at[1], t, sem).start(priority=1)
        return carry

    lax.fori_loop(0, tm, issue, 0, unroll=ROW_DMA_UNROLL)
    for k in range(2):
        pltpu.make_async_copy(ybuf_ref.at[pl.ds(0, tm)], rows_ref.at[k], sem).wait()
    route = route_ref[...]
    o_ref[...] = (x1_ref[...] + route[:, _R_G1:_R_G1 + 1] * rows_ref[0]
                  + route[:, _R_G2:_R_G2 + 1] * rows_ref[1])


def combine(x1, route, dest, ybuf, tm):
    n = x1.shape[0]
    nt = n // tm
    return pl.pallas_call(
        functools.partial(_combine_kernel, tm), grid=(nt,),
        in_specs=[pl.BlockSpec((1, 1, 2 * tm), lambda i: (i, 0, 0), memory_space=pltpu.SMEM),
                  pl.BlockSpec((tm, D_MODEL), lambda i: (i, 0)),
                  pl.BlockSpec((tm, LANES), lambda i: (i, 0)),
                  pl.BlockSpec(memory_space=pl.ANY)],
        out_specs=pl.BlockSpec((tm, D_MODEL), lambda i: (i, 0)),
        out_shape=jax.ShapeDtypeStruct((n, D_MODEL), F32),
        scratch_shapes=[pltpu.VMEM((2, tm, D_MODEL), F32), pltpu.SemaphoreType.DMA(())],
        compiler_params=_params(("arbitrary",)),
        name="combine",
    )(_slot_tiles(dest, tm), x1, route, ybuf)


def kernel(x_prompt, x_sample, mem_prompt, cache_k, cache_v, cache_mem_k, cache_mem_v, state_conv, page_table, norm1, w_in, conv_w, sb_bias, mem_norm, w_mem_kv, q_norm, k_norm, w_branch, w_o, norm2, w_group, b_group, w_router, b_router, w1, w3, w2):
    nb, seq, _ = x_prompt.shape
    ns, dec_seq, _ = x_sample.shape
    assert dec_seq == 1 and seq % SB_TQ == 0 and SB_TQ % (2 * SB_TK) == 0
    n_mem = mem_prompt.shape[1]
    sbw = SB_HEADS * SB_HEAD_DIM
    xw = X_HEADS * X_HEAD_DIM
    past = page_table.shape[1] * PAGE_SIZE

    assert w_in.shape[1] == _W_MAIN and _C_QX == _C_K + 2 * sbw and _C_GL == _C_QX + xw
    w_main = w_in.astype(BF16)
    w_kvt = kv_weight_t(w_in)
    wb = w_branch.astype(BF16)
    wo_bf = w_o.astype(BF16)
    w_route = jnp.zeros((D_MODEL, LANES), F32)
    w_route = w_route.at[:, :N_EXPERTS].set(w_router).at[:, N_EXPERTS:N_EXPERTS + N_GROUPS].set(w_group)
    b_route = jnp.zeros((1, LANES), F32)
    b_route = b_route.at[0, :N_EXPERTS].set(b_router).at[0, N_EXPERTS:N_EXPERTS + N_GROUPS].set(b_group)
    w_route = w_route.astype(BF16)

    xp = x_prompt.reshape(nb * seq, D_MODEL)
    mk_p, mv_p = memkv(mem_prompt.reshape(nb * n_mem, D_MODEL), mem_norm, w_mem_kv.astype(BF16), k_norm, n_mem)
    (kt_p, vt_p, q_p, qx_p, g1_p, g2_p, mp_p, tail_p, ktb_p, vtb_p) = inproj(
        xp, norm1, w_main, w_kvt, wb[0], conv_w, q_norm, seq, 512)
    ysb_p = sb_prompt(q_p, ktb_p, vtb_p, sb_bias, seq)
    yx_p = xattn_prompt(qx_p, mk_p.astype(BF16), mv_p.astype(BF16), seq, n_mem, 512)

    xs = x_sample.reshape(ns, D_MODEL)
    (kt_s, vt_s, q_s, qx_s, g1_s, g2_s, mp_s, u_s) = inproj(
        xs, norm1, w_main, w_kvt, wb[0], conv_w, q_norm, ns, ns,
        conv_state=(state_conv[:, 0], state_conv[:, 1]))
    ck_t = cache_k.transpose(0, 2, 3, 1).reshape(cache_k.shape[0], sbw, PAGE_SIZE)
    cv_t = cache_v.transpose(0, 2, 3, 1).reshape(cache_v.shape[0], sbw, PAGE_SIZE)
    ysb_s = sb_sample(q_s.astype(F32), kt_s[0], vt_s[0], ck_t, cv_t, page_table, sb_bias,
                      new_key_visible=(past < past))
    yx_s = xattn_sample(qx_s, cache_mem_k, cache_mem_v)

    cnt0 = jnp.zeros((1, LANES), F32)
    x1_p, h_p, route_p, rt_p, cnt1 = merge(xp, mp_p, ysb_p, yx_p, g1_p, g2_p, wb[1], wb[2], wo_bf, norm2,
                                           w_route, b_route, cnt0, 512)
    x1_s, h_s, route_s, rt_s, cnt2 = merge(xs, mp_s, ysb_s, yx_s, g1_s, g2_s, wb[1], wb[2], wo_bf, norm2,
                                           w_route, b_route, cnt1, ns)

    n_slots = 2 * (nb * seq + ns)
    nblk = -(-(n_slots + N_EXPERTS * (MOE_BM - 1)) // MOE_BM)
    counts = cnt2[0, :N_EXPERTS].astype(I32)
    padded = (counts + MOE_BM - 1) // MOE_BM * MOE_BM
    ends = jnp.cumsum(padded)
    starts = ends - padded
    n_used = (ends[-1] // MOE_BM).astype(I32).reshape(1)
    blk_start = jnp.arange(nblk, dtype=I32) * MOE_BM
    block_e = jnp.sum(blk_start[:, None] >= ends[None, :], axis=1).astype(I32)
    block_e = jnp.minimum(block_e, block_e[jnp.maximum(n_used[0] - 1, 0)])

    def dests(route_t):
        e = route_t[_R_E1:_R_E2 + 1].astype(I32)
        rk = route_t[_R_R1:_R_R2 + 1].astype(I32)
        first = jnp.zeros_like(e)
        for x in range(N_EXPERTS):
            first = jnp.where(e == x, starts[x], first)
        return first + rk

    dest_p, dest_s = dests(rt_p), dests(rt_s)
    buf = jnp.zeros((nblk * MOE_BM, D_MODEL), F32)
    buf = dispatch(h_p, dest_p, buf, 512)
    buf = dispatch(h_s, dest_s, buf, ns)
    ybuf = experts(buf, block_e, n_used, w1, w3, w2)
    y_p = combine(x1_p, route_p, dest_p, ybuf, 1024)
    y_s = combine(x1_s, route_s, dest_s, ybuf, ns)

    def heads_last(t, b, s):
        return t.reshape(b, SB_HEADS, SB_HEAD_DIM, s).transpose(0, 3, 1, 2)

    conv_prompt = tail_p[:, SUBLANES - 2:, :]
    conv_sample = jnp.stack([state_conv[:, 1], u_s[0]], axis=1)
    return (y_p.reshape(nb, seq, D_MODEL), y_s.reshape(ns, 1, D_MODEL),
            heads_last(kt_p, nb, seq), heads_last(vt_p, nb, seq), conv_prompt,
            mk_p.reshape(nb, n_mem, X_HEADS, X_HEAD_DIM), mv_p.reshape(nb, n_mem, X_HEADS, X_HEAD_DIM),
            heads_last(kt_s, 1, ns).reshape(ns, 1, SB_HEADS, SB_HEAD_DIM),
            heads_last(vt_s, 1, ns).reshape(ns, 1, SB_HEADS, SB_HEAD_DIM),
            conv_sample)
```
